```python
import math, functools
import jax, jax.numpy as jnp
from jax import lax
import numpy as np

D_MODEL = 2048
BATCH = 2
SEQ = 8192
DEPTH = 4

GRID_W = 64
CTX_LEN = 256
HEAD_DIM = 128
RET_HEADS = 4
DN_HEADS = 4
ATT_HEADS = 8
ATT_KV_HEADS = 2
RET_W = RET_HEADS * HEAD_DIM
DN_W = DN_HEADS * HEAD_DIM
ATT_W = ATT_HEADS * HEAD_DIM
ATT_KV_W = ATT_KV_HEADS * HEAD_DIM
MIX_W = RET_W + DN_W + ATT_W
RET_CHUNK = 128
DN_CHUNK = 64
DN_CONV_K = 5
Q_BLOCK = 128
ROPE_THETA = 10000.0
D_FF = ((8 * D_MODEL + 3 * 256 - 1) // (3 * 256)) * 256
DEEPNORM_ALPHA = (2 * DEPTH) ** 0.25
DEEPNORM_BETA = (8 * DEPTH) ** -0.25
EPS = 1e-6
SPLIT_SIZES = (RET_W, RET_W, RET_W, RET_W, 3 * DN_W, DN_W, 2 * DN_HEADS, 2 * DN_HEADS, ATT_W, ATT_KV_W, ATT_KV_W)
PROJ_W = sum(SPLIT_SIZES)
SPLIT_POINTS = tuple(np.cumsum(SPLIT_SIZES)[:-1].tolist())

kernel_name = "hybrid_ret_gdn_gqa_diffusion_block"


def layer_norm(x, w, b):
    xf = x.astype(jnp.float32)
    mu = jnp.mean(xf, -1, keepdims=True)
    var = jnp.mean(jnp.square(xf - mu), -1, keepdims=True)
    return (xf - mu) * lax.rsqrt(var + EPS) * w + b


def rms_norm(x, w=None):
    xf = x.astype(jnp.float32)
    y = xf * lax.rsqrt(jnp.mean(xf * xf, -1, keepdims=True) + EPS)
    if w is not None:
        y = y * w
    return y.astype(x.dtype)


def l2_normalize(x):
    return x * lax.rsqrt(jnp.sum(x * x, -1, keepdims=True) + EPS)


def split_heads(a, n_heads):
    return a.reshape(a.shape[:-1] + (n_heads, HEAD_DIM))


def modulate(h, shift, scale):
    return h * (1.0 + scale) + shift


def post_norm(x, y, w, b):
    return layer_norm(DEEPNORM_ALPHA * x + y, w, b).astype(x.dtype)


def axial_rope(n_tokens):
    rows = n_tokens // GRID_W
    row = jnp.repeat(jnp.arange(rows, dtype=jnp.float32), GRID_W)
    col = jnp.tile(jnp.arange(GRID_W, dtype=jnp.float32), rows)
    n_freq = HEAD_DIM // 4
    inv = ROPE_THETA ** (-jnp.arange(n_freq, dtype=jnp.float32) / n_freq)
    ang = jnp.concatenate([row[:, None] * inv, col[:, None] * inv], -1)
    return jnp.cos(ang), jnp.sin(ang)


def apply_rope(x, cos, sin):
    xf = x.astype(jnp.float32)
    x1, x2 = jnp.split(xf, 2, -1)
    c = cos[None, :, None, :]
    s = sin[None, :, None, :]
    return jnp.concatenate([x1 * c - x2 * s, x1 * s + x2 * c], -1).astype(x.dtype)


def bidirectional(scan_f, scan_b, ctx_f, lat_f, ctx_b, lat_b, s0):
    flip = lambda seq: tuple(jnp.flip(a, axis=1) for a in seq)
    o_cf, s_cf = scan_f(*ctx_f, s0)
    o_lf, _ = scan_f(*lat_f, s_cf)
    o_cb, s_cb = scan_b(*flip(ctx_b), s0)
    o_lb, _ = scan_b(*flip(lat_b), s_cb)
    return o_cf + jnp.flip(o_cb, 1), o_lf + jnp.flip(o_lb, 1)


def retention_scan(q, k, v, s0, log_gamma):
    b, l, h, d = q.shape
    n = l // RET_CHUNK
    qc = q.reshape(b, n, RET_CHUNK, h, d)
    kc = k.reshape(b, n, RET_CHUNK, h, d)
    vc = v.reshape(b, n, RET_CHUNK, h, d)
    pos = jnp.arange(RET_CHUNK, dtype=jnp.float32)
    rel = pos[:, None] - pos[None, :]
    decay = jnp.where(rel >= 0, jnp.exp(jnp.maximum(rel, 0.0)[None] * log_gamma[:, None, None]), 0.0)
    intra = jnp.einsum('bnihd,bnjhd->bnhij', qc, kc) * decay
    o_intra = jnp.einsum('bnhij,bnjhd->bnihd', intra, vc)
    q_decay = jnp.exp((pos + 1.0)[:, None] * log_gamma[None, :])
    k_decay = jnp.exp((RET_CHUNK - 1.0 - pos)[:, None] * log_gamma[None, :])
    chunk_kv = jnp.einsum('bnjhd,jh,bnjhe->nbhde', kc, k_decay, vc)
    chunk_decay = jnp.exp(RET_CHUNK * log_gamma)[None, :, None, None]

    def step(s, u):
        return s * chunk_decay + u, s

    s_fin, s_prev = lax.scan(step, s0, chunk_kv)
    o_inter = jnp.einsum('bnihd,ih,nbhde->bnihe', qc, q_decay, s_prev)
    return (o_intra + o_inter).reshape(b, l, h, d), s_fin


def to_chunks(a, c):
    b, l = a.shape[:2]
    return jnp.swapaxes(a.reshape((b, l // c, c) + a.shape[2:]), 2, 3)


def gated_delta_scan(q, k, v, g, beta, s0):
    b, l, h, _ = q.shape
    c = DN_CHUNK
    qc, kc, vc = to_chunks(q, c), to_chunks(k, c), to_chunks(v, c)
    gc, bc = to_chunks(g, c), to_chunks(beta, c)
    g_cum = jnp.cumsum(gc, -1)
    tri = jnp.tril(jnp.ones((c, c), bool))
    strict = jnp.tril(jnp.ones((c, c), bool), -1)
    diff = g_cum[..., :, None] - g_cum[..., None, :]
    decay = jnp.where(tri, jnp.exp(jnp.where(tri, diff, 0.0)), 0.0)
    k_beta = kc * bc[..., None]
    v_beta = vc * bc[..., None]
    a = jnp.where(strict, jnp.einsum('bnhid,bnhjd->bnhij', k_beta, kc) * decay, 0.0)
    eye = jnp.eye(c, dtype=a.dtype)
    t = lax.linalg.triangular_solve(eye + a, jnp.broadcast_to(eye, a.shape), left_side=True,
                                    lower=True, unit_diagonal=True)
    w_val = jnp.einsum('bnhij,bnhjd->bnhid', t, v_beta)
    k_cum = jnp.einsum('bnhij,bnhjd->bnhid', t, k_beta * jnp.exp(g_cum)[..., None])
    qk = jnp.einsum('bnhid,bnhjd->bnhij', qc, kc) * decay
    q_g = qc * jnp.exp(g_cum)[..., None]
    k_g = kc * jnp.exp(g_cum[..., -1:] - g_cum)[..., None]
    g_last = jnp.exp(g_cum[..., -1])
    xs = tuple(jnp.moveaxis(z, 1, 0) for z in (w_val, k_cum, qk, q_g, k_g, g_last))

    def step(s, inp):
        w_i, kc_i, qk_i, qg_i, kg_i, gl_i = inp
        v_new = w_i - jnp.einsum('bhcd,bhde->bhce', kc_i, s)
        o = jnp.einsum('bhcd,bhde->bhce', qg_i, s) + jnp.einsum('bhij,bhje->bhie', qk_i, v_new)
        s = s * gl_i[..., None, None] + jnp.einsum('bhcd,bhce->bhde', kg_i, v_new)
        return s, o

    s_fin, o = lax.scan(step, s0, xs)
    return o.transpose(1, 0, 3, 2, 4).reshape(b, l, h, -1), s_fin


def short_conv(x, w):
    pad = DN_CONV_K // 2
    return lax.conv_general_dilated(x, w[:, None, :], window_strides=(1,), padding=[(pad, pad)],
                                    dimension_numbers=('NWC', 'WIO', 'NWC'),
                                    feature_group_count=x.shape[-1])


def block_attention(q, k, v):
    b, lq, h, d = q.shape
    kvh = k.shape[2]
    qb = q.reshape(b, lq // Q_BLOCK, Q_BLOCK, kvh, h // kvh, d).swapaxes(0, 1)

    def one_block(qi):
        s = jnp.einsum('bqhgd,bkhd->bhgqk', qi, k).astype(jnp.float32) * d ** -0.5
        p = jax.nn.softmax(s, axis=-1).astype(v.dtype)
        return jnp.einsum('bhgqk,bkhd->bqhgd', p, v)

    o = lax.map(one_block, qb)
    return o.swapaxes(0, 1).reshape(b, lq, h * d)


def retention_group(pc, pl, decay_logit, cos, sin):
    log_gamma = jax.nn.log_sigmoid(decay_logit.astype(jnp.float32))

    def qkv(p, rotate):
        q, k, v = (split_heads(a, RET_HEADS).astype(jnp.float32) for a in p[:3])
        if rotate:
            q, k = apply_rope(q, cos, sin), apply_rope(k, cos, sin)
        return q, k * HEAD_DIM ** -0.5, v

    ctx_seq = qkv(pc, False)
    lat_seq = qkv(pl, True)
    s0 = jnp.zeros((pc[0].shape[0], RET_HEADS, HEAD_DIM, HEAD_DIM), jnp.float32)
    o_c, o_l = bidirectional(functools.partial(retention_scan, log_gamma=log_gamma[0]),
                             functools.partial(retention_scan, log_gamma=log_gamma[1]),
                             ctx_seq, lat_seq, ctx_seq, lat_seq, s0)

    def out(o, g):
        y = rms_norm(o) * jax.nn.silu(split_heads(g, RET_HEADS).astype(jnp.float32))
        return y.reshape(y.shape[:2] + (RET_W,)).astype(g.dtype)

    return out(o_c, pc[3]), out(o_l, pl[3])


def deltanet_group(pc, pl, conv_w, a_log, dt_bias, norm_w):
    neg_a = -jnp.exp(a_log.astype(jnp.float32))
    dt_b = dt_bias.astype(jnp.float32)

    def prep(p):
        qkv, _, a, bb = p
        qkv = jax.nn.silu(short_conv(qkv, conv_w.astype(qkv.dtype)))
        q, k, v = (split_heads(t, DN_HEADS).astype(jnp.float32) for t in jnp.split(qkv, 3, -1))
        q = l2_normalize(q) * HEAD_DIM ** -0.5
        k = l2_normalize(k)
        bsz, n = a.shape[:2]
        g = neg_a * jax.nn.softplus(a.astype(jnp.float32).reshape(bsz, n, 2, DN_HEADS) + dt_b)
        beta = jax.nn.sigmoid(bb.astype(jnp.float32).reshape(bsz, n, 2, DN_HEADS))
        return (q, k, v, g[:, :, 0], beta[:, :, 0]), (q, k, v, g[:, :, 1], beta[:, :, 1])

    cf, cb = prep(pc)
    lf, lb = prep(pl)
    s0 = jnp.zeros((pc[0].shape[0], DN_HEADS, HEAD_DIM, HEAD_DIM), jnp.float32)
    o_c, o_l = bidirectional(gated_delta_scan, gated_delta_scan, cf, lf, cb, lb, s0)

    def out(o, z):
        y = rms_norm(o, norm_w) * jax.nn.silu(split_heads(z, DN_HEADS).astype(jnp.float32))
        return y.reshape(y.shape[:2] + (DN_W,)).astype(z.dtype)

    return out(o_c, pc[1]), out(o_l, pl[1])


def attention_group(pc, pl, qn_w, kn_w, cos, sin, keep_ctx):
    def qkv(p):
        q = rms_norm(split_heads(p[0], ATT_HEADS), qn_w)
        k = rms_norm(split_heads(p[1], ATT_KV_HEADS), kn_w)
        return q, k, split_heads(p[2], ATT_KV_HEADS)

    qc, kc, vc = qkv(pc)
    ql, kl, vl = qkv(pl)
    ql, kl = apply_rope(ql, cos, sin), apply_rope(kl, cos, sin)
    y_l = block_attention(ql, jnp.concatenate([kc, kl], 1), jnp.concatenate([vc, vl], 1))
    y_c = block_attention(qc, kc, vc) if keep_ctx else None
    return y_c, y_l


def hybrid_mixer(h_ctx, h_lat, w_in, ret_decay_logit, dn_conv_w, dn_a_log, dn_dt_bias, dn_norm_w,
                 att_qn_w, att_kn_w, cos, sin, keep_ctx):
    pc = jnp.split(h_ctx @ w_in, SPLIT_POINTS, axis=-1)
    pl = jnp.split(h_lat @ w_in, SPLIT_POINTS, axis=-1)
    rc, rl = retention_group(pc[0:4], pl[0:4], ret_decay_logit, cos, sin)
    dc, dl = deltanet_group(pc[4:8], pl[4:8], dn_conv_w, dn_a_log, dn_dt_bias, dn_norm_w)
    ac, al = attention_group(pc[8:11], pl[8:11], att_qn_w, att_kn_w, cos, sin, keep_ctx)
    y_lat = jnp.concatenate([rl, dl, al], -1)
    y_ctx = jnp.concatenate([rc, dc, ac], -1) if keep_ctx else None
    return y_ctx, y_lat


def swiglu(h, w_in, w_out):
    gate, up = jnp.split(h @ w_in, 2, -1)
    return (jax.nn.silu(gate) * up) @ w_out


def setup_inputs(seed: int = 0) -> dict:
    key = jax.random.key(seed)
    ks = jax.random.split(key, 24)
    f32 = jnp.float32

    def nrm(k, shape, scale):
        return jax.random.normal(k, shape, f32) * scale

    base_logit = jnp.log(2.0 ** (5.0 + jnp.arange(RET_HEADS, dtype=f32)) - 1.0)
    dt = jnp.exp(jax.random.uniform(ks[10], (DEPTH, 2, DN_HEADS), f32, math.log(1e-3), math.log(1e-1)))
    return {
        "x": nrm(ks[0], (BATCH, SEQ, D_MODEL), 1.0),
        "c": nrm(ks[1], (BATCH, D_MODEL), 1.0),
        "ctx": nrm(ks[2], (BATCH, CTX_LEN, D_MODEL), 1.0),
        "c_ctx": nrm(ks[3], (D_MODEL,), 1.0),
        "w_ada": nrm(ks[4], (DEPTH, D_MODEL, 6 * D_MODEL), 0.5 * D_MODEL ** -0.5),
        "b_ada": nrm(ks[5], (DEPTH, 6 * D_MODEL), 0.02),
        "w_in": nrm(ks[6], (DEPTH, D_MODEL, PROJ_W), D_MODEL ** -0.5),
        "ret_decay_logit": base_logit + nrm(ks[7], (DEPTH, 2, RET_HEADS), 0.1),
        "dn_conv_w": nrm(ks[8], (DEPTH, DN_CONV_K, 3 * DN_W), DN_CONV_K ** -0.5),
        "dn_a_log": jnp.log(jax.random.uniform(ks[9], (DEPTH, 2, DN_HEADS), f32, 1.0, 16.0)),
        "dn_dt_bias": dt + jnp.log(-jnp.expm1(-dt)),
        "dn_norm_w": 1.0 + nrm(ks[11], (DEPTH, HEAD_DIM), 0.02),
        "att_qn_w": 1.0 + nrm(ks[12], (DEPTH, HEAD_DIM), 0.02),
        "att_kn_w": 1.0 + nrm(ks[13], (DEPTH, HEAD_DIM), 0.02),
        "w_o": nrm(ks[14], (DEPTH, MIX_W, D_MODEL), MIX_W ** -0.5 * DEEPNORM_BETA),
        "ln1_w": 1.0 + nrm(ks[15], (DEPTH, D_MODEL), 0.02),
        "ln1_b": nrm(ks[16], (DEPTH, D_MODEL), 0.02),
        "w_ffn_in": nrm(ks[17], (DEPTH, D_MODEL, 2 * D_FF), D_MODEL ** -0.5),
        "w_ffn_out": nrm(ks[18], (DEPTH, D_FF, D_MODEL), D_FF ** -0.5 * DEEPNORM_BETA),
        "ln2_w": 1.0 + nrm(ks[19], (DEPTH, D_MODEL), 0.02),
        "ln2_b": nrm(ks[20], (DEPTH, D_MODEL), 0.02),
    }


def reference(x, c, ctx, c_ctx, w_ada, b_ada, w_in, ret_decay_logit, dn_conv_w, dn_a_log, dn_dt_bias,
              dn_norm_w, att_qn_w, att_kn_w, w_o, ln1_w, ln1_b, w_ffn_in, w_ffn_out, ln2_w, ln2_b):
    cos, sin = axial_rope(x.shape[1])
    cond_lat = jax.nn.silu(c)
    cond_ctx = jax.nn.silu(c_ctx)
    for i in range(DEPTH):
        keep_ctx = i < DEPTH - 1
        m_l = jnp.split((cond_lat @ w_ada[i] + b_ada[i])[:, None, :], 6, -1)
        m_c = jnp.split((cond_ctx @ w_ada[i] + b_ada[i])[None, None, :], 6, -1)
        y_c, y_l = hybrid_mixer(modulate(ctx, m_c[0], m_c[1]), modulate(x, m_l[0], m_l[1]), w_in[i],
                                ret_decay_logit[i], dn_conv_w[i], dn_a_log[i], dn_dt_bias[i], dn_norm_w[i],
                                att_qn_w[i], att_kn_w[i], cos, sin, keep_ctx)
        x = post_norm(x, m_l[2] * (y_l @ w_o[i]), ln1_w[i], ln1_b[i])
        x = post_norm(x, m_l[5] * swiglu(modulate(x, m_l[3], m_l[4]), w_ffn_in[i], w_ffn_out[i]),
                      ln2_w[i], ln2_b[i])
        if keep_ctx:
            ctx = post_norm(ctx, m_c[2] * (y_c @ w_o[i]), ln1_w[i], ln1_b[i])
            ctx = post_norm(ctx, m_c[5] * swiglu(modulate(ctx, m_c[3], m_c[4]), w_ffn_in[i], w_ffn_out[i]),
                            ln2_w[i], ln2_b[i])
    return x
```

```python
import functools

import jax
import jax.numpy as jnp
from jax import lax
from jax.experimental import pallas as pl
from jax.experimental.pallas import tpu as pltpu

F32 = jnp.float32
BF16 = jnp.bfloat16

HEAD_DIM = 128
RET_HEADS = 4
DN_HEADS = 4
ATT_HEADS = 8
ATT_KV_HEADS = 2
RET_W = RET_HEADS * HEAD_DIM
DN_W = DN_HEADS * HEAD_DIM
ATT_W = ATT_HEADS * HEAD_DIM
ATT_KV_W = ATT_KV_HEADS * HEAD_DIM
RET_CHUNK = 128
DN_CHUNK = 64
DN_CONV_K = 5
GRID_W = 64
ROPE_THETA = 10000.0
MODEL_DEPTH = 4
DEEPNORM_ALPHA = (2 * MODEL_DEPTH) ** 0.25
EPS = 1e-6
QK_SCALE = HEAD_DIM ** -0.5

COL_RET_Q = 0
COL_RET_K = RET_W
COL_RET_V = 2 * RET_W
COL_RET_G = 3 * RET_W
COL_DN_QKV = 4 * RET_W
COL_DN_Z = COL_DN_QKV + 3 * DN_W
COL_ATT_Q = COL_DN_Z + DN_W
COL_ATT_K = COL_ATT_Q + ATT_W
COL_ATT_V = COL_ATT_K + ATT_KV_W
MAIN_W = COL_ATT_V + ATT_KV_W
GATE_COLS = 128
N_DN_GATES = 2 * DN_HEADS

VMEM_LIMIT_MB = 56


def _params(sem, vmem_mb=VMEM_LIMIT_MB):
    return pltpu.CompilerParams(dimension_semantics=sem, vmem_limit_bytes=vmem_mb * 1024 * 1024)


def _dot(a, b):
    return jnp.dot(a, b, preferred_element_type=F32)


def _dot_nt(a, b):
    return lax.dot_general(a, b, (((1,), (1,)), ((), ())), preferred_element_type=F32)


def _silu(x):
    return x * jax.nn.sigmoid(x)


def _is_ctx_rows(i, tm, l_lat):
    rows = i * tm + lax.broadcasted_iota(jnp.int32, (tm, 1), 0)
    return rows >= l_lat


def _mod_row(is_ctx, ml_ref, mc_ref, k):
    return jnp.where(is_ctx, mc_ref[0, k:k + 1, :], ml_ref[0, k:k + 1, :])


def _layer_norm(r, w, b):
    mu = jnp.mean(r, -1, keepdims=True)
    rc = r - mu
    var = jnp.mean(rc * rc, -1, keepdims=True)
    return rc * lax.rsqrt(var + EPS) * w + b


def _ada_kernel(c_ref, w_ref, b_ref, o_ref):
    h = _silu(c_ref[...])
    o_ref[0] = jnp.dot(h, w_ref[0], preferred_element_type=F32, precision=lax.Precision.HIGHEST) + b_ref[0]


def _ada_call(cond, w_ada, b_ada):
    depth, d, n6 = w_ada.shape
    rows = cond.shape[0]
    tn = 1024
    return pl.pallas_call(
        _ada_kernel,
        grid=(depth, n6 // tn),
        in_specs=[pl.BlockSpec((rows, d), lambda l, j: (0, 0)),
                  pl.BlockSpec((1, d, tn), lambda l, j: (l, 0, j)),
                  pl.BlockSpec((1, 1, tn), lambda l, j: (l, 0, j))],
        out_specs=pl.BlockSpec((1, rows, tn), lambda l, j: (l, 0, j)),
        out_shape=jax.ShapeDtypeStruct((depth, rows, n6), F32),
        compiler_params=_params(("arbitrary", "arbitrary")),
        name="ada_mod",
    )(cond, w_ada, b_ada.reshape(depth, 1, n6))


def _inproj_kernel(x_ref, ml_ref, mc_ref, w_ref, wab_ref, pm_ref, pab_ref, h_ref, *, tm, l_lat):
    i = pl.program_id(1)
    j = pl.program_id(2)

    @pl.when(j == 0)
    def _():
        is_ctx = _is_ctx_rows(i, tm, l_lat)
        shift = _mod_row(is_ctx, ml_ref, mc_ref, 0)
        scale = _mod_row(is_ctx, ml_ref, mc_ref, 1)
        h = (x_ref[0] * (1.0 + scale) + shift).astype(BF16)
        h_ref[...] = h
        pab_ref[0] = _dot(h, wab_ref[...])

    pm_ref[0] = _dot(h_ref[...], w_ref[...]).astype(BF16)


def _inproj_call(x_all, ml, mc, w_main, w_ab, *, l_lat, tm, tn):
    b, l_all, d = x_all.shape
    kern = functools.partial(_inproj_kernel, tm=tm, l_lat=l_lat)
    return pl.pallas_call(
        kern,
        grid=(b, l_all // tm, MAIN_W // tn),
        in_specs=[pl.BlockSpec((1, tm, d), lambda bb, i, j: (bb, i, 0)),
                  pl.BlockSpec((1, 6, d), lambda bb, i, j: (bb, 0, 0)),
                  pl.BlockSpec((1, 6, d), lambda bb, i, j: (0, 0, 0)),
                  pl.BlockSpec((d, tn), lambda bb, i, j: (0, j)),
                  pl.BlockSpec((d, GATE_COLS), lambda bb, i, j: (0, 0))],
        out_specs=[pl.BlockSpec((1, tm, tn), lambda bb, i, j: (bb, i, j)),
                   pl.BlockSpec((1, tm, GATE_COLS), lambda bb, i, j: (bb, i, 0))],
        out_shape=[jax.ShapeDtypeStruct((b, l_all, MAIN_W), BF16),
                   jax.ShapeDtypeStruct((b, l_all, GATE_COLS), F32)],
        scratch_shapes=[pltpu.VMEM((tm, d), BF16)],
        compiler_params=_params(("arbitrary", "arbitrary", "arbitrary")),
        name="in_proj",
    )(x_all, ml, mc, w_main, w_ab)


def _wo_kernel(y_ref, a_ref, x_ref, ml_ref, mc_ref, w_ref, lnw_ref, lnb_ref, o_ref, *, tm, l_lat):
    i = pl.program_id(1)
    half = y_ref.shape[-1]
    acc = _dot(y_ref[0], w_ref[0:half, :]) + _dot(a_ref[0], w_ref[half:, :])
    is_ctx = _is_ctx_rows(i, tm, l_lat)
    gate = _mod_row(is_ctx, ml_ref, mc_ref, 2)
    r = DEEPNORM_ALPHA * x_ref[0] + gate * acc
    o_ref[0] = _layer_norm(r, lnw_ref[...], lnb_ref[...])


def _wo_call(y_rd, y_att, x_all, ml, mc, w_o, ln_w, ln_b, *, l_lat, tm):
    b, l_all, d = x_all.shape
    half = y_rd.shape[-1]
    kern = functools.partial(_wo_kernel, tm=tm, l_lat=l_lat)
    return pl.pallas_call(
        kern,
        grid=(b, l_all // tm),
        in_specs=[pl.BlockSpec((1, tm, half), lambda bb, i: (bb, i, 0)),
                  pl.BlockSpec((1, tm, half), lambda bb, i: (bb, i, 0)),
                  pl.BlockSpec((1, tm, d), lambda bb, i: (bb, i, 0)),
                  pl.BlockSpec((1, 6, d), lambda bb, i: (bb, 0, 0)),
                  pl.BlockSpec((1, 6, d), lambda bb, i: (0, 0, 0)),
                  pl.BlockSpec((2 * half, d), lambda bb, i: (0, 0)),
                  pl.BlockSpec((1, d), lambda bb, i: (0, 0)),
                  pl.BlockSpec((1, d), lambda bb, i: (0, 0))],
        out_specs=pl.BlockSpec((1, tm, d), lambda bb, i: (bb, i, 0)),
        out_shape=jax.ShapeDtypeStruct((b, l_all, d), F32),
        compiler_params=_params(("arbitrary", "arbitrary")),
        name="w_o_postnorm",
    )(y_rd, y_att, x_all, ml, mc, w_o, ln_w, ln_b)


def _ffn_kernel(x_ref, ml_ref, mc_ref, wg_ref, wu_ref, wo_ref, lnw_ref, lnb_ref, o_ref, h_ref, acc_ref,
                *, tm, l_lat):
    i = pl.program_id(1)
    f = pl.program_id(2)

    @pl.when(f == 0)
    def _():
        is_ctx = _is_ctx_rows(i, tm, l_lat)
        shift = _mod_row(is_ctx, ml_ref, mc_ref, 3)
        scale = _mod_row(is_ctx, ml_ref, mc_ref, 4)
        h_ref[...] = (x_ref[0] * (1.0 + scale) + shift).astype(BF16)
        acc_ref[...] = jnp.zeros_like(acc_ref)

    h = h_ref[...]
    g = _dot(h, wg_ref[...])
    u = _dot(h, wu_ref[...])
    acc_ref[...] += _dot((_silu(g) * u).astype(BF16), wo_ref[...])

    @pl.when(f == pl.num_programs(2) - 1)
    def _():
        is_ctx = _is_ctx_rows(i, tm, l_lat)
        gate = _mod_row(is_ctx, ml_ref, mc_ref, 5)
        r = DEEPNORM_ALPHA * x_ref[0] + gate * acc_ref[...]
        o_ref[0] = _layer_norm(r, lnw_ref[...], lnb_ref[...])


def _ffn_call(x_all, ml, mc, w_in, w_out, ln_w, ln_b, *, l_lat, out_rows, tm, tf):
    b, _, d = x_all.shape
    d_ff = w_out.shape[0]
    nf = d_ff // tf
    kern = functools.partial(_ffn_kernel, tm=tm, l_lat=l_lat)
    return pl.pallas_call(
        kern,
        grid=(b, pl.cdiv(out_rows, tm), nf),
        in_specs=[pl.BlockSpec((1, tm, d), lambda bb, i, f: (bb, i, 0)),
                  pl.BlockSpec((1, 6, d), lambda bb, i, f: (bb, 0, 0)),
                  pl.BlockSpec((1, 6, d), lambda bb, i, f: (0, 0, 0)),
                  pl.BlockSpec((d, tf), lambda bb, i, f: (0, f)),
                  pl.BlockSpec((d, tf), lambda bb, i, f: (0, f + nf)),
                  pl.BlockSpec((tf, d), lambda bb, i, f: (f, 0)),
                  pl.BlockSpec((1, d), lambda bb, i, f: (0, 0)),
                  pl.BlockSpec((1, d), lambda bb, i, f: (0, 0))],
        out_specs=pl.BlockSpec((1, tm, d), lambda bb, i, f: (bb, i, 0)),
        out_shape=jax.ShapeDtypeStruct((b, out_rows, d), F32),
        scratch_shapes=[pltpu.VMEM((tm, d), BF16), pltpu.VMEM((tm, d), F32)],
        compiler_params=_params(("arbitrary", "arbitrary", "arbitrary")),
        name="ffn_postnorm",
    )(x_all, ml, mc, w_in, w_in, w_out, ln_w, ln_b)


def _rope(x, cos2, sin2):
    return x * cos2 + pltpu.roll(x, HEAD_DIM // 2, 1) * sin2


def _ret_kernel(lg_ref, qf_ref, kf_ref, vf_ref, cf_ref, sf_ref, qb_ref, kb_ref, vb_ref, cb_ref, sb_ref,
                of_ref, ob_ref, s_ref, dm_ref, qd_ref, kd_ref):
    step = pl.program_id(1)
    c = RET_CHUNK

    @pl.when(step == 0)
    def _():
        s_ref[...] = jnp.zeros_like(s_ref)
        ii = lax.broadcasted_iota(jnp.int32, (c, c), 0).astype(F32)
        jj = lax.broadcasted_iota(jnp.int32, (c, c), 1).astype(F32)
        for d in range(2):
            for h in range(RET_HEADS):
                lg = lg_ref[d, h]
                if d == 0:
                    rel, qe, ke = ii - jj, ii + 1.0, (c - 1.0) - ii
                else:
                    rel, qe, ke = jj - ii, c - ii, ii
                idx = d * RET_HEADS + h
                dm_ref[idx] = jnp.where(rel >= 0, jnp.exp(jnp.maximum(rel, 0.0) * lg), 0.0)
                qd_ref[idx] = jnp.exp(qe * lg)
                kd_ref[idx] = jnp.exp(ke * lg)

    dirs = ((qf_ref, kf_ref, vf_ref, cf_ref, sf_ref, of_ref), (qb_ref, kb_ref, vb_ref, cb_ref, sb_ref, ob_ref))
    for d, (q_ref, k_ref, v_ref, c_ref, sn_ref, o_ref) in enumerate(dirs):
        cos2 = c_ref[...]
        sin2 = sn_ref[...]
        for h in range(RET_HEADS):
            sl = slice(h * HEAD_DIM, (h + 1) * HEAD_DIM)
            idx = d * RET_HEADS + h
            q = _rope(q_ref[0, :, sl].astype(F32), cos2, sin2)
            k = _rope(k_ref[0, :, sl].astype(F32), cos2, sin2) * QK_SCALE
            v = v_ref[0, :, sl]
            s_prev = s_ref[idx]
            qk = _dot_nt(q.astype(BF16), k.astype(BF16)) * dm_ref[idx]
            o = _dot(qk.astype(BF16), v) + _dot((q * qd_ref[idx]).astype(BF16), s_prev.astype(BF16))
            o_ref[0, :, sl] = o
            kt = (k * kd_ref[idx]).T
            chunk_decay = jnp.exp(jnp.full((1, HEAD_DIM), float(c), F32) * lg_ref[d, h])
            s_ref[idx] = s_prev * chunk_decay + _dot(kt.astype(BF16), v)


def _ret_call(log_gamma, p_main, cos2, sin2, *, l_lat):
    b, l_all, _ = p_main.shape
    c = RET_CHUNK
    n_all, n_lat = l_all // c, l_lat // c
    n_ctx = n_all - n_lat

    def fwd(s):
        return jnp.where(s < n_ctx, n_lat + s, s - n_ctx)

    def bwd(s):
        return n_all - 1 - s

    def pspec(colblk, order):
        return pl.BlockSpec((1, c, RET_W), lambda bb, s: (bb, order(s), colblk))

    def tspec(order):
        return pl.BlockSpec((c, HEAD_DIM), lambda bb, s: (order(s), 0))

    in_specs = [pl.BlockSpec(memory_space=pltpu.SMEM)]
    for order in (fwd, bwd):
        in_specs += [pspec(COL_RET_Q // RET_W, order), pspec(COL_RET_K // RET_W, order),
                     pspec(COL_RET_V // RET_W, order), tspec(order), tspec(order)]
    nhd = 2 * RET_HEADS
    return pl.pallas_call(
        _ret_kernel,
        grid=(b, n_all),
        in_specs=in_specs,
        out_specs=[pl.BlockSpec((1, c, RET_W), lambda bb, s: (bb, fwd(s), 0)),
                   pl.BlockSpec((1, c, RET_W), lambda bb, s: (bb, bwd(s), 0))],
        out_shape=[jax.ShapeDtypeStruct((b, l_all, RET_W), F32)] * 2,
        scratch_shapes=[pltpu.VMEM((nhd, HEAD_DIM, HEAD_DIM), F32), pltpu.VMEM((nhd, c, c), F32),
                        pltpu.VMEM((nhd, c, HEAD_DIM), F32), pltpu.VMEM((nhd, c, HEAD_DIM), F32)],
        compiler_params=_params(("arbitrary", "arbitrary")),
        name="retention_scan",
    )(log_gamma, p_main, p_main, p_main, cos2, sin2, p_main, p_main, p_main, cos2, sin2)


def _attprep_kernel(q_ref, k_ref, c_ref, s_ref, qw_ref, kw_ref, qo_ref, ko_ref):
    cos2 = c_ref[...]
    sin2 = s_ref[...]

    def norm_rope(x, w):
        xf = x.astype(F32)
        y = xf * lax.rsqrt(jnp.mean(xf * xf, -1, keepdims=True) + EPS) * w
        return _rope(y, cos2, sin2)

    for h in range(ATT_HEADS):
        sl = slice(h * HEAD_DIM, (h + 1) * HEAD_DIM)
        qo_ref[0, :, sl] = (norm_rope(q_ref[0, :, sl], qw_ref[...]) * QK_SCALE).astype(BF16)
    for h in range(ATT_KV_HEADS):
        sl = slice(h * HEAD_DIM, (h + 1) * HEAD_DIM)
        ko_ref[0, :, sl] = norm_rope(k_ref[0, :, sl], kw_ref[...]).astype(BF16)


def _attprep_call(p_main, cos2, sin2, qn_w, kn_w, *, tm):
    b, l_all, _ = p_main.shape
    return pl.pallas_call(
        _attprep_kernel,
        grid=(b, l_all // tm),
        in_specs=[pl.BlockSpec((1, tm, ATT_W), lambda bb, i: (bb, i, COL_ATT_Q // ATT_W)),
                  pl.BlockSpec((1, tm, ATT_KV_W), lambda bb, i: (bb, i, COL_ATT_K // ATT_KV_W)),
                  pl.BlockSpec((tm, HEAD_DIM), lambda bb, i: (i, 0)),
                  pl.BlockSpec((tm, HEAD_DIM), lambda bb, i: (i, 0)),
                  pl.BlockSpec((1, HEAD_DIM), lambda bb, i: (0, 0)),
                  pl.BlockSpec((1, HEAD_DIM), lambda bb, i: (0, 0))],
        out_specs=[pl.BlockSpec((1, tm, ATT_W), lambda bb, i: (bb, i, 0)),
                   pl.BlockSpec((1, tm, ATT_KV_W), lambda bb, i: (bb, i, 0))],
        out_shape=[jax.ShapeDtypeStruct((b, l_all, ATT_W), BF16),
                   jax.ShapeDtypeStruct((b, l_all, ATT_KV_W), BF16)],
        compiler_params=_params(("arbitrary", "arbitrary")),
        name="attn_prep",
    )(p_main, p_main, cos2, sin2, qn_w, kn_w)


def _softmax_pv(q, k, v):
    s = _dot_nt(q, k)
    m = jnp.max(s, -1, keepdims=True)
    p = jnp.exp(s - m)
    l = jnp.sum(p, -1, keepdims=True)
    return _dot(p.astype(BF16), v) / l


def _att_kernel(q_ref, k_ref, v_ref, o_ref, *, l_lat, tq):
    i = pl.program_id(2)
    n_lat_tiles = l_lat // tq

    @pl.when(i < n_lat_tiles)
    def _():
        o_ref[0] = _softmax_pv(q_ref[0], k_ref[0], v_ref[0]).astype(BF16)

    @pl.when(i >= n_lat_tiles)
    def _():
        o_ref[0] = _softmax_pv(q_ref[0], k_ref[0, l_lat:, :], v_ref[0, l_lat:, :]).astype(BF16)


def _att_call(qn, kn, p_main, *, l_lat, tq):
    b, l_all, _ = qn.shape
    group = ATT_HEADS // ATT_KV_HEADS
    kern = functools.partial(_att_kernel, l_lat=l_lat, tq=tq)
    return pl.pallas_call(
        kern,
        grid=(b, ATT_HEADS, l_all // tq),
        in_specs=[pl.BlockSpec((1, tq, HEAD_DIM), lambda bb, h, i: (bb, i, h)),
                  pl.BlockSpec((1, l_all, HEAD_DIM), lambda bb, h, i: (bb, 0, h // group)),
                  pl.BlockSpec((1, l_all, HEAD_DIM), lambda bb, h, i: (bb, 0, COL_ATT_V // HEAD_DIM + h // group))],
        out_specs=pl.BlockSpec((1, tq, HEAD_DIM), lambda bb, h, i: (bb, i, h)),
        out_shape=jax.ShapeDtypeStruct((b, l_all, ATT_W), BF16),
        compiler_params=_params(("arbitrary", "arbitrary", "arbitrary")),
        name="attention",
    )(qn, kn, p_main)


def _dnprep_kernel(*refs, tm, l_lat, l_all):
    mains, prevs, nexts = refs[0:3], refs[3:6], refs[6:9]
    cw_ref, pab_ref, alog_ref, dtb_ref = refs[9:13]
    outs = refs[13:16]
    g_ref, ext_ref = refs[16], refs[17]
    i = pl.program_id(1)
    halo = 16
    pad = DN_CONV_K // 2
    first = jnp.logical_or(i == 0, i == l_lat // tm)
    last = jnp.logical_or(i == l_lat // tm - 1, i == l_all // tm - 1)

    keep_prev = jnp.where(first, 0.0, 1.0)
    keep_next = jnp.where(last, 0.0, 1.0)

    for part in range(3):
        ext_ref[0:halo, :] = prevs[part][0].astype(F32) * keep_prev
        ext_ref[halo:halo + tm, :] = mains[part][0].astype(F32)
        ext_ref[halo + tm:, :] = nexts[part][0].astype(F32) * keep_next
        acc = jnp.zeros((tm, DN_W), F32)
        for j in range(DN_CONV_K):
            w_j = cw_ref[j:j + 1, part * DN_W:(part + 1) * DN_W]
            acc = acc + w_j * ext_ref[pl.ds(halo - pad + j, tm), :]
        y = _silu(acc)
        for h in range(DN_HEADS):
            sl = slice(h * HEAD_DIM, (h + 1) * HEAD_DIM)
            yh = y[:, sl]
            if part < 2:
                yh = yh * lax.rsqrt(jnp.sum(yh * yh, -1, keepdims=True) + EPS)
            if part == 0:
                yh = yh * QK_SCALE
            outs[part][0, :, sl] = yh.astype(BF16)

    a = pab_ref[0]
    col = lax.broadcasted_iota(jnp.int32, a.shape, 1)
    z = a + dtb_ref[...]
    softplus = jnp.maximum(z, 0.0) + jnp.log(1.0 + jnp.exp(-jnp.abs(z)))
    g_ref[0] = jnp.where(col < N_DN_GATES, -jnp.exp(alog_ref[...]) * softplus, jax.nn.sigmoid(a))


def _dnprep_call(p_main, p_ab, conv_w, alog_row, dtb_row, *, l_lat, tm):
    b, l_all, _ = p_main.shape
    halo = 16
    r = tm // halo
    cb = COL_DN_QKV // DN_W
    in_specs = []
    for part in range(3):
        in_specs.append(pl.BlockSpec((1, tm, DN_W), lambda bb, i, p=part: (bb, i, cb + p)))
    for part in range(3):
        in_specs.append(pl.BlockSpec((1, halo, DN_W), lambda bb, i, p=part: (bb, jnp.maximum(i * r - 1, 0), cb + p)))
    for part in range(3):
        in_specs.append(pl.BlockSpec(
            (1, halo, DN_W), lambda bb, i, p=part: (bb, jnp.minimum((i + 1) * r, l_all // halo - 1), cb + p)))
    in_specs += [pl.BlockSpec((DN_CONV_K, 3 * DN_W), lambda bb, i: (0, 0)),
                 pl.BlockSpec((1, tm, GATE_COLS), lambda bb, i: (bb, i, 0)),
                 pl.BlockSpec((1, GATE_COLS), lambda bb, i: (0, 0)),
                 pl.BlockSpec((1, GATE_COLS), lambda bb, i: (0, 0))]
    kern = functools.partial(_dnprep_kernel, tm=tm, l_lat=l_lat, l_all=l_all)
    return pl.pallas_call(
        kern,
        grid=(b, l_all // tm),
        in_specs=in_specs,
        out_specs=[pl.BlockSpec((1, tm, DN_W), lambda bb, i: (bb, i, 0))] * 3
        + [pl.BlockSpec((1, tm, GATE_COLS), lambda bb, i: (bb, i, 0))],
        out_shape=[jax.ShapeDtypeStruct((b, l_all, DN_W), BF16)] * 3
        + [jax.ShapeDtypeStruct((b, l_all, GATE_COLS), F32)],
        scratch_shapes=[pltpu.VMEM((tm + 2 * halo, DN_W), F32)],
        compiler_params=_params(("arbitrary", "arbitrary")),
        name="deltanet_conv_gates",
    )(*([p_main] * 9), conv_w, p_ab, alog_row, dtb_row)


def _neumann_inverse(a, eye):
    x = -a
    t = eye + x
    p = x
    for _ in range(5):
        pb = p.astype(BF16)
        p = _dot(pb, pb)
        t = t + _dot(t.astype(BF16), p.astype(BF16))
    return t


def _dnchunk_kernel(q_ref, k_ref, v_ref, g_ref, gt_ref, w_ref, kc_ref, qg_ref, kgt_ref, qk_ref, gl_ref):
    c = DN_CHUNK
    hi = lax.Precision.HIGHEST
    ii = lax.broadcasted_iota(jnp.int32, (c, c), 0)
    jj = lax.broadcasted_iota(jnp.int32, (c, c), 1)
    lower = (ii >= jj).astype(F32)
    upper = (ii <= jj).astype(F32)
    eye = (ii == jj).astype(F32)
    eye_hd = (lax.broadcasted_iota(jnp.int32, (HEAD_DIM, HEAD_DIM), 0)
              == lax.broadcasted_iota(jnp.int32, (HEAD_DIM, HEAD_DIM), 1)).astype(BF16)
    gcol = g_ref[0]
    grow = gt_ref[0, 0]
    cum_col = (jnp.dot(lower, gcol, preferred_element_type=F32, precision=hi),
               jnp.dot(upper, gcol, preferred_element_type=F32, precision=hi))
    cum_row = (jnp.dot(grow, upper, preferred_element_type=F32, precision=hi),
               jnp.dot(grow, lower, preferred_element_type=F32, precision=hi))
    tot = jnp.dot(grow, jnp.ones((c, HEAD_DIM), F32), preferred_element_type=F32, precision=hi)

    qk_ref[...] = jnp.zeros_like(qk_ref)
    kgt_ref[...] = jnp.zeros_like(kgt_ref)
    gl_ref[...] = jnp.zeros_like(gl_ref)
    for d in range(2):
        gl_ref[0, d, 0, 0:DN_HEADS, :] = jnp.exp(tot[d * DN_HEADS:(d + 1) * DN_HEADS, :])

    for h in range(DN_HEADS):
        sl = slice(h * HEAD_DIM, (h + 1) * HEAD_DIM)
        q16, k16 = q_ref[0, :, sl], k_ref[0, :, sl]
        q, k, v = q16.astype(F32), k16.astype(F32), v_ref[0, :, sl].astype(F32)
        kk = _dot_nt(k16, k16)
        qk0 = _dot_nt(q16, k16)
        for d in range(2):
            col = d * DN_HEADS + h
            gcc = cum_col[d][:, col:col + 1]
            gcr = cum_row[d][col:col + 1, :]
            beta = gcol[:, N_DN_GATES + col:N_DN_GATES + col + 1]
            incl = (ii >= jj) if d == 0 else (ii <= jj)
            strict = (ii > jj) if d == 0 else (ii < jj)
            decay = jnp.where(incl, jnp.exp(jnp.where(incl, gcc - gcr, 0.0)), 0.0)
            a = jnp.where(strict, kk * beta * decay, 0.0)
            t16 = _neumann_inverse(a, eye).astype(BF16)
            e_col = jnp.exp(gcc)
            w_ref[0, d, :, sl] = _dot(t16, (v * beta).astype(BF16))
            kc_ref[0, d, :, sl] = _dot(t16, (k * (beta * e_col)).astype(BF16)).astype(BF16)
            qg_ref[0, d, :, sl] = (q * e_col).astype(BF16)
            kg = (k * jnp.exp(tot[col:col + 1, 0:1] - gcc)).astype(BF16)
            kgt_ref[0, d, 0, :, pl.ds(h * HEAD_DIM, c)] = _dot_nt(eye_hd, kg).astype(BF16)
            qk_ref[0, d, :, pl.ds(h * HEAD_DIM, c)] = (qk0 * decay).astype(BF16)


def _dnchunk_call(qd, kd, vd, g, gt):
    b, l_all, _ = qd.shape
    c = DN_CHUNK
    n = l_all // c
    tok = lambda dt: jax.ShapeDtypeStruct((b, 2, l_all, DN_W), dt)
    tok_spec = pl.BlockSpec((1, 2, c, DN_W), lambda bb, s: (bb, 0, s, 0))
    return pl.pallas_call(
        _dnchunk_kernel,
        grid=(b, n),
        in_specs=[pl.BlockSpec((1, c, DN_W), lambda bb, s: (bb, s, 0))] * 3
        + [pl.BlockSpec((1, c, GATE_COLS), lambda bb, s: (bb, s, 0)),
           pl.BlockSpec((1, 1, 2 * N_DN_GATES, c), lambda bb, s: (bb, s, 0, 0))],
        out_specs=[tok_spec, tok_spec, tok_spec,
                   pl.BlockSpec((1, 2, 1, HEAD_DIM, DN_W), lambda bb, s: (bb, 0, s, 0, 0)),
                   tok_spec,
                   pl.BlockSpec((1, 2, 1, 8, HEAD_DIM), lambda bb, s: (bb, 0, s, 0, 0))],
        out_shape=[tok(F32), tok(BF16), tok(BF16),
                   jax.ShapeDtypeStruct((b, 2, n, HEAD_DIM, DN_W), BF16),
                   tok(BF16),
                   jax.ShapeDtypeStruct((b, 2, n, 8, HEAD_DIM), F32)],
        compiler_params=_params(("arbitrary", "arbitrary")),
        name="deltanet_chunk_factors",
    )(qd, kd, vd, g, gt)


def _dnscan_kernel(*refs):
    ins_f, ins_b = refs[0:6], refs[6:12]
    of_ref, ob_ref, s_ref = refs[12], refs[13], refs[14]
    c = DN_CHUNK
    step = pl.program_id(1)

    @pl.when(step == 0)
    def _():
        s_ref[...] = jnp.zeros_like(s_ref)

    for d, (ins, o_ref) in enumerate(((ins_f, of_ref), (ins_b, ob_ref))):
        w_ref, kc_ref, qg_ref, kgt_ref, qk_ref, gl_ref = ins
        for h in range(DN_HEADS):
            sl = slice(h * HEAD_DIM, (h + 1) * HEAD_DIM)
            idx = d * DN_HEADS + h
            s_prev = s_ref[idx]
            s16 = s_prev.astype(BF16)
            v_new = w_ref[0, 0, :, sl] - _dot(kc_ref[0, 0, :, sl], s16)
            v16 = v_new.astype(BF16)
            qk = qk_ref[0, 0, :, pl.ds(h * HEAD_DIM, c)]
            o_ref[0, :, sl] = _dot(qg_ref[0, 0, :, sl], s16) + _dot(qk, v16)
            kgt = kgt_ref[0, 0, 0, :, pl.ds(h * HEAD_DIM, c)]
            s_ref[idx] = s_prev * gl_ref[0, 0, 0, h:h + 1, :] + _dot(kgt, v16)


def _dnscan_call(w, kc, qg, kgt, qk, gl, *, l_lat):
    b, _, l_all, _ = w.shape
    c = DN_CHUNK
    n_all, n_lat = l_all // c, l_lat // c
    n_ctx = n_all - n_lat

    def fwd(s):
        return jnp.where(s < n_ctx, n_lat + s, s - n_ctx)

    def bwd(s):
        return n_all - 1 - s

    in_specs = []
    for d, order in enumerate((fwd, bwd)):
        tok_spec = pl.BlockSpec((1, 1, c, DN_W), lambda bb, s, d=d, o=order: (bb, d, o(s), 0))
        in_specs += [tok_spec, tok_spec, tok_spec,
                     pl.BlockSpec((1, 1, 1, HEAD_DIM, DN_W), lambda bb, s, d=d, o=order: (bb, d, o(s), 0, 0)),
                     tok_spec,
                     pl.BlockSpec((1, 1, 1, 8, HEAD_DIM), lambda bb, s, d=d, o=order: (bb, d, o(s), 0, 0))]
    return pl.pallas_call(
        _dnscan_kernel,
        grid=(b, n_all),
        in_specs=in_specs,
        out_specs=[pl.BlockSpec((1, c, DN_W), lambda bb, s: (bb, fwd(s), 0)),
                   pl.BlockSpec((1, c, DN_W), lambda bb, s: (bb, bwd(s), 0))],
        out_shape=[jax.ShapeDtypeStruct((b, l_all, DN_W), F32)] * 2,
        scratch_shapes=[pltpu.VMEM((2 * DN_HEADS, HEAD_DIM, HEAD_DIM), F32)],
        compiler_params=_params(("arbitrary", "arbitrary")),
        name="deltanet_scan",
    )(w, kc, qg, kgt, qk, gl, w, kc, qg, kgt, qk, gl)


def _mixout_kernel(rf_ref, rb_ref, rg_ref, df_ref, db_ref, dz_ref, nw_ref, y_ref):
    def head_norm(o):
        return o * lax.rsqrt(jnp.mean(o * o, -1, keepdims=True) + EPS)

    for h in range(RET_HEADS):
        sl = slice(h * HEAD_DIM, (h + 1) * HEAD_DIM)
        o = rf_ref[0, :, sl] + rb_ref[0, :, sl]
        y_ref[0, :, sl] = (head_norm(o) * _silu(rg_ref[0, :, sl].astype(F32))).astype(BF16)
    for h in range(DN_HEADS):
        sl = slice(h * HEAD_DIM, (h + 1) * HEAD_DIM)
        o = df_ref[0, :, sl] + db_ref[0, :, sl]
        y = head_norm(o) * nw_ref[...] * _silu(dz_ref[0, :, sl].astype(F32))
        y_ref[0, :, RET_W + h * HEAD_DIM:RET_W + (h + 1) * HEAD_DIM] = y.astype(BF16)


def _mixout_call(ret_f, ret_b, dn_f, dn_b, p_main, dn_norm_w, *, tm):
    b, l_all, _ = ret_f.shape
    o_spec = pl.BlockSpec((1, tm, RET_W), lambda bb, i: (bb, i, 0))
    return pl.pallas_call(
        _mixout_kernel,
        grid=(b, l_all // tm),
        in_specs=[o_spec, o_spec,
                  pl.BlockSpec((1, tm, RET_W), lambda bb, i: (bb, i, COL_RET_G // RET_W)),
                  o_spec, o_spec,
                  pl.BlockSpec((1, tm, DN_W), lambda bb, i: (bb, i, COL_DN_Z // DN_W)),
                  pl.BlockSpec((1, HEAD_DIM), lambda bb, i: (0, 0))],
        out_specs=pl.BlockSpec((1, tm, RET_W + DN_W), lambda bb, i: (bb, i, 0)),
        out_shape=jax.ShapeDtypeStruct((b, l_all, RET_W + DN_W), BF16),
        compiler_params=_params(("arbitrary", "arbitrary")),
        name="mixer_out_norm",
    )(ret_f, ret_b, p_main, dn_f, dn_b, p_main, dn_norm_w)


def _rope_tables(l_lat, l_ctx):
    rows = l_lat // GRID_W
    row = jnp.repeat(jnp.arange(rows, dtype=F32), GRID_W)
    col = jnp.tile(jnp.arange(GRID_W, dtype=F32), rows)
    n_freq = HEAD_DIM // 4
    inv = ROPE_THETA ** (-jnp.arange(n_freq, dtype=F32) / n_freq)
    ang = jnp.concatenate([row[:, None] * inv, col[:, None] * inv], -1)
    cos, sin = jnp.cos(ang), jnp.sin(ang)
    cos2 = jnp.concatenate([cos, cos], -1)
    sin2 = jnp.concatenate([-sin, sin], -1)
    cos2 = jnp.concatenate([cos2, jnp.ones((l_ctx, HEAD_DIM), F32)], 0)
    sin2 = jnp.concatenate([sin2, jnp.zeros((l_ctx, HEAD_DIM), F32)], 0)
    return cos2, sin2


def _pad_row(v, width):
    v = v.reshape(1, -1).astype(F32)
    return jnp.pad(v, ((0, 0), (0, width - v.shape[1])))


def _row_tile(l_lat, l_ctx):
    l_all = l_lat + l_ctx
    for tm in (768, 512, 256, 128):
        if l_all % tm == 0:
            return tm
    raise ValueError("token count must be a multiple of 128")


def kernel(x, c, ctx, c_ctx, w_ada, b_ada, w_in, ret_decay_logit, dn_conv_w, dn_a_log, dn_dt_bias, dn_norm_w,
           att_qn_w, att_kn_w, w_o, ln1_w, ln1_b, w_ffn_in, w_ffn_out, ln2_w, ln2_b):
    bsz, l_lat, d = x.shape
    l_ctx = ctx.shape[1]
    l_all = l_lat + l_ctx
    depth = w_ada.shape[0]
    assert l_lat % 256 == 0 and l_ctx % 256 == 0 and l_lat % GRID_W == 0
    tm = _row_tile(l_lat, l_ctx)
    tn = MAIN_W // 4

    cos2, sin2 = _rope_tables(l_lat, l_ctx)
    x_all = jnp.concatenate([x, ctx], axis=1)

    cond_rows = 8 * pl.cdiv(bsz + 1, 8)
    cond = jnp.concatenate([c, c_ctx[None, :]], 0)
    cond = jnp.pad(cond, ((0, cond_rows - bsz - 1), (0, 0)))
    mod = _ada_call(cond, w_ada, b_ada)

    gate_lo = COL_DN_Z + DN_W
    for i in range(depth):
        last = i == depth - 1
        ml = mod[i, :bsz].reshape(bsz, 6, d)
        mc = mod[i, bsz].reshape(1, 6, d)
        w_main = jnp.concatenate([w_in[i, :, :gate_lo], w_in[i, :, gate_lo + 2 * N_DN_GATES:]], 1).astype(BF16)
        w_ab = jnp.pad(w_in[i, :, gate_lo:gate_lo + 2 * N_DN_GATES],
                       ((0, 0), (0, GATE_COLS - 2 * N_DN_GATES))).astype(BF16)

        p_main, p_ab = _inproj_call(x_all, ml, mc, w_main, w_ab, l_lat=l_lat, tm=tm, tn=tn)

        log_gamma = jax.nn.log_sigmoid(ret_decay_logit[i].astype(F32))
        ret_f, ret_b = _ret_call(log_gamma, p_main, cos2, sin2, l_lat=l_lat)

        qn, kn = _attprep_call(p_main, cos2, sin2, att_qn_w[i].reshape(1, -1), att_kn_w[i].reshape(1, -1), tm=tm)
        y_att = _att_call(qn, kn, p_main, l_lat=l_lat, tq=256)

        qd, kd, vd, g = _dnprep_call(p_main, p_ab, dn_conv_w[i], _pad_row(dn_a_log[i], GATE_COLS),
                                     _pad_row(dn_dt_bias[i], GATE_COLS), l_lat=l_lat, tm=256)
        gt = g[:, :, :2 * N_DN_GATES].reshape(bsz, l_all // DN_CHUNK, DN_CHUNK, 2 * N_DN_GATES)
        gt = jnp.swapaxes(gt, 2, 3)
        dn_f, dn_b = _dnscan_call(*_dnchunk_call(qd, kd, vd, g, gt), l_lat=l_lat)

        y_rd = _mixout_call(ret_f, ret_b, dn_f, dn_b, p_main, dn_norm_w[i].reshape(1, -1), tm=256)

        x_all = _wo_call(y_rd, y_att, x_all, ml, mc, w_o[i].astype(BF16), ln1_w[i].reshape(1, -1),
                         ln1_b[i].reshape(1, -1), l_lat=l_lat, tm=tm)
        x_all = _ffn_call(x_all, ml, mc, w_ffn_in[i].astype(BF16), w_ffn_out[i].astype(BF16),
                          ln2_w[i].reshape(1, -1), ln2_b[i].reshape(1, -1), l_lat=l_lat,
                          out_rows=l_lat if last else l_all, tm=tm, tf=512)
    return x_all
```

```python
import functools

import jax
import jax.numpy as jnp
from jax import lax
from jax.experimental import pallas as pl
from jax.experimental.pallas import tpu as pltpu

F32 = jnp.float32
BF16 = jnp.bfloat16

HEAD_DIM = 128
RET_HEADS = 4
DN_HEADS = 4
ATT_HEADS = 8
ATT_KV_HEADS = 2
RET_W = RET_HEADS * HEAD_DIM
DN_W = DN_HEADS * HEAD_DIM
ATT_W = ATT_HEADS * HEAD_DIM
ATT_KV_W = ATT_KV_HEADS * HEAD_DIM
RET_CHUNK = 128
DN_CHUNK = 64
DN_CONV_K = 5
GRID_W = 64
ROPE_THETA = 10000.0
MODEL_DEPTH = 4
DEEPNORM_ALPHA = (2 * MODEL_DEPTH) ** 0.25
EPS = 1e-6
QK_SCALE = HEAD_DIM ** -0.5
LOG2_E = 1.4426950408889634

COL_RET_Q = 0
COL_RET_K = RET_W
COL_RET_V = 2 * RET_W
COL_RET_G = 3 * RET_W
COL_DN_QKV = 4 * RET_W
COL_DN_Z = COL_DN_QKV + 3 * DN_W
COL_ATT_Q = COL_DN_Z + DN_W
COL_ATT_K = COL_ATT_Q + ATT_W
COL_ATT_V = COL_ATT_K + ATT_KV_W
MAIN_W = COL_ATT_V + ATT_KV_W
GATE_COLS = 128
N_DN_GATES = 2 * DN_HEADS

VMEM_LIMIT_MB = 56


def _params(sem, vmem_mb=VMEM_LIMIT_MB):
    return pltpu.CompilerParams(dimension_semantics=sem, vmem_limit_bytes=vmem_mb * 1024 * 1024)


def _dot(a, b):
    return jnp.dot(a, b, preferred_element_type=F32)


def _dot_nt(a, b):
    return lax.dot_general(a, b, (((1,), (1,)), ((), ())), preferred_element_type=F32)


def _silu(x):
    return x * jax.nn.sigmoid(x)


def _is_ctx_rows(i, tm, l_lat):
    rows = i * tm + lax.broadcasted_iota(jnp.int32, (tm, 1), 0)
    return rows >= l_lat


def _mod_row(is_ctx, ml_ref, mc_ref, k):
    return jnp.where(is_ctx, mc_ref[0, k:k + 1, :], ml_ref[0, k:k + 1, :])


def _layer_norm(r, w, b):
    mu = jnp.mean(r, -1, keepdims=True)
    rc = r - mu
    var = jnp.mean(rc * rc, -1, keepdims=True)
    return rc * lax.rsqrt(var + EPS) * w + b


def _ada_kernel(c_ref, w_ref, b_ref, o_ref):
    h = _silu(c_ref[...])
    o_ref[0] = jnp.dot(h, w_ref[0], preferred_element_type=F32, precision=lax.Precision.HIGHEST) + b_ref[0]


def _ada_call(cond, w_ada, b_ada):
    depth, d, n6 = w_ada.shape
    rows = cond.shape[0]
    tn = 1024
    return pl.pallas_call(
        _ada_kernel,
        grid=(depth, n6 // tn),
        in_specs=[pl.BlockSpec((rows, d), lambda l, j: (0, 0)),
                  pl.BlockSpec((1, d, tn), lambda l, j: (l, 0, j)),
                  pl.BlockSpec((1, 1, tn), lambda l, j: (l, 0, j))],
        out_specs=pl.BlockSpec((1, rows, tn), lambda l, j: (l, 0, j)),
        out_shape=jax.ShapeDtypeStruct((depth, rows, n6), F32),
        compiler_params=_params(("arbitrary", "arbitrary")),
        name="ada_mod",
    )(cond, w_ada, b_ada.reshape(depth, 1, n6))


def _inproj_kernel(x_ref, ml_ref, mc_ref, w_ref, wab_ref, pm_ref, pab_ref, h_ref, *, tm, l_lat):
    i = pl.program_id(1)
    j = pl.program_id(2)

    @pl.when(j == 0)
    def _():
        is_ctx = _is_ctx_rows(i, tm, l_lat)
        shift = _mod_row(is_ctx, ml_ref, mc_ref, 0)
        scale = _mod_row(is_ctx, ml_ref, mc_ref, 1)
        h = (x_ref[0] * (1.0 + scale) + shift).astype(BF16)
        h_ref[...] = h
        pab_ref[0] = _dot(h, wab_ref[...])

    pm_ref[0] = _dot(h_ref[...], w_ref[...]).astype(BF16)


def _inproj_call(x_all, ml, mc, w_main, w_ab, *, l_lat, tm, tn):
    b, l_all, d = x_all.shape
    kern = functools.partial(_inproj_kernel, tm=tm, l_lat=l_lat)
    return pl.pallas_call(
        kern,
        grid=(b, l_all // tm, MAIN_W // tn),
        in_specs=[pl.BlockSpec((1, tm, d), lambda bb, i, j: (bb, i, 0)),
                  pl.BlockSpec((1, 6, d), lambda bb, i, j: (bb, 0, 0)),
                  pl.BlockSpec((1, 6, d), lambda bb, i, j: (0, 0, 0)),
                  pl.BlockSpec((d, tn), lambda bb, i, j: (0, j)),
                  pl.BlockSpec((d, GATE_COLS), lambda bb, i, j: (0, 0))],
        out_specs=[pl.BlockSpec((1, tm, tn), lambda bb, i, j: (bb, i, j)),
                   pl.BlockSpec((1, tm, GATE_COLS), lambda bb, i, j: (bb, i, 0))],
        out_shape=[jax.ShapeDtypeStruct((b, l_all, MAIN_W), BF16),
                   jax.ShapeDtypeStruct((b, l_all, GATE_COLS), F32)],
        scratch_shapes=[pltpu.VMEM((tm, d), BF16)],
        compiler_params=_params(("arbitrary", "arbitrary", "arbitrary")),
        name="in_proj",
    )(x_all, ml, mc, w_main, w_ab)


def _wo_kernel(y_ref, a_ref, x_ref, ml_ref, mc_ref, w_ref, lnw_ref, lnb_ref, o_ref, *, tm, l_lat):
    i = pl.program_id(1)
    half = y_ref.shape[-1]
    acc = _dot(y_ref[0], w_ref[0:half, :]) + _dot(a_ref[0], w_ref[half:, :])
    is_ctx = _is_ctx_rows(i, tm, l_lat)
    gate = _mod_row(is_ctx, ml_ref, mc_ref, 2)
    r = DEEPNORM_ALPHA * x_ref[0] + gate * acc
    o_ref[0] = _layer_norm(r, lnw_ref[...], lnb_ref[...])


def _wo_call(y_rd, y_att, x_all, ml, mc, w_o, ln_w, ln_b, *, l_lat, tm):
    b, l_all, d = x_all.shape
    half = y_rd.shape[-1]
    kern = functools.partial(_wo_kernel, tm=tm, l_lat=l_lat)
    return pl.pallas_call(
        kern,
        grid=(b, l_all // tm),
        in_specs=[pl.BlockSpec((1, tm, half), lambda bb, i: (bb, i, 0)),
                  pl.BlockSpec((1, tm, half), lambda bb, i: (bb, i, 0)),
                  pl.BlockSpec((1, tm, d), lambda bb, i: (bb, i, 0)),
                  pl.BlockSpec((1, 6, d), lambda bb, i: (bb, 0, 0)),
                  pl.BlockSpec((1, 6, d), lambda bb, i: (0, 0, 0)),
                  pl.BlockSpec((2 * half, d), lambda bb, i: (0, 0)),
                  pl.BlockSpec((1, d), lambda bb, i: (0, 0)),
                  pl.BlockSpec((1, d), lambda bb, i: (0, 0))],
        out_specs=pl.BlockSpec((1, tm, d), lambda bb, i: (bb, i, 0)),
        out_shape=jax.ShapeDtypeStruct((b, l_all, d), F32),
        compiler_params=_params(("arbitrary", "arbitrary")),
        name="w_o_postnorm",
    )(y_rd, y_att, x_all, ml, mc, w_o, ln_w, ln_b)


def _ffn_kernel(x_ref, ml_ref, mc_ref, wg_ref, wu_ref, wo_ref, lnw_ref, lnb_ref, o_ref, h_ref, acc_ref,
                *, tm, l_lat):
    i = pl.program_id(1)
    f = pl.program_id(2)

    @pl.when(f == 0)
    def _():
        is_ctx = _is_ctx_rows(i, tm, l_lat)
        shift = _mod_row(is_ctx, ml_ref, mc_ref, 3)
        scale = _mod_row(is_ctx, ml_ref, mc_ref, 4)
        h_ref[...] = (x_ref[0] * (1.0 + scale) + shift).astype(BF16)
        acc_ref[...] = jnp.zeros_like(acc_ref)

    h = h_ref[...]
    g = _dot(h, wg_ref[...])
    u = _dot(h, wu_ref[...])
    acc_ref[...] += _dot((_silu(g) * u).astype(BF16), wo_ref[...])

    @pl.when(f == pl.num_programs(2) - 1)
    def _():
        is_ctx = _is_ctx_rows(i, tm, l_lat)
        gate = _mod_row(is_ctx, ml_ref, mc_ref, 5)
        r = DEEPNORM_ALPHA * x_ref[0] + gate * acc_ref[...]
        o_ref[0] = _layer_norm(r, lnw_ref[...], lnb_ref[...])


def _ffn_call(x_all, ml, mc, w_in, w_out, ln_w, ln_b, *, l_lat, out_rows, tm, tf):
    b, _, d = x_all.shape
    d_ff = w_out.shape[0]
    nf = d_ff // tf
    kern = functools.partial(_ffn_kernel, tm=tm, l_lat=l_lat)
    return pl.pallas_call(
        kern,
        grid=(b, pl.cdiv(out_rows, tm), nf),
        in_specs=[pl.BlockSpec((1, tm, d), lambda bb, i, f: (bb, i, 0)),
                  pl.BlockSpec((1, 6, d), lambda bb, i, f: (bb, 0, 0)),
                  pl.BlockSpec((1, 6, d), lambda bb, i, f: (0, 0, 0)),
                  pl.BlockSpec((d, tf), lambda bb, i, f: (0, f)),
                  pl.BlockSpec((d, tf), lambda bb, i, f: (0, f + nf)),
                  pl.BlockSpec((tf, d), lambda bb, i, f: (f, 0)),
                  pl.BlockSpec((1, d), lambda bb, i, f: (0, 0)),
                  pl.BlockSpec((1, d), lambda bb, i, f: (0, 0))],
        out_specs=pl.BlockSpec((1, tm, d), lambda bb, i, f: (bb, i, 0)),
        out_shape=jax.ShapeDtypeStruct((b, out_rows, d), F32),
        scratch_shapes=[pltpu.VMEM((tm, d), BF16), pltpu.VMEM((tm, d), F32)],
        compiler_params=_params(("arbitrary", "arbitrary", "arbitrary")),
        name="ffn_postnorm",
    )(x_all, ml, mc, w_in, w_in, w_out, ln_w, ln_b)


def _rope(x, cos2, sin2):
    return x * cos2 + pltpu.roll(x, HEAD_DIM // 2, 1) * sin2


def _ret_kernel(lg_ref, qf_ref, kf_ref, vf_ref, cf_ref, sf_ref, qb_ref, kb_ref, vb_ref, cb_ref, sb_ref,
                of_ref, ob_ref, s_ref, dm_ref, qd_ref, kd_ref):
    step = pl.program_id(1)
    c = RET_CHUNK

    @pl.when(step == 0)
    def _():
        s_ref[...] = jnp.zeros_like(s_ref)
        ii = lax.broadcasted_iota(jnp.int32, (c, c), 0).astype(F32)
        jj = lax.broadcasted_iota(jnp.int32, (c, c), 1).astype(F32)
        for d in range(2):
            for h in range(RET_HEADS):
                lg = lg_ref[d, h]
                if d == 0:
                    rel, qe, ke = ii - jj, ii + 1.0, (c - 1.0) - ii
                else:
                    rel, qe, ke = jj - ii, c - ii, ii
                idx = d * RET_HEADS + h
                dm_ref[idx] = jnp.where(rel >= 0, jnp.exp(jnp.maximum(rel, 0.0) * lg), 0.0)
                qd_ref[idx] = jnp.exp(qe * lg)
                kd_ref[idx] = jnp.exp(ke * lg)

    dirs = ((qf_ref, kf_ref, vf_ref, cf_ref, sf_ref, of_ref), (qb_ref, kb_ref, vb_ref, cb_ref, sb_ref, ob_ref))
    chains = [(d, h) for d in range(2) for h in range(RET_HEADS)]
    sls = [slice(h * HEAD_DIM, (h + 1) * HEAD_DIM) for h in range(RET_HEADS)]
    tabs = [(dirs[d][3][...], dirs[d][4][...]) for d in range(2)]
    qs = [_rope(dirs[d][0][0, :, sls[h]].astype(F32), *tabs[d]) for d, h in chains]
    ks = [_rope(dirs[d][1][0, :, sls[h]].astype(F32), *tabs[d]) * QK_SCALE for d, h in chains]
    vs = [dirs[d][2][0, :, sls[h]] for d, h in chains]
    s_prev = [s_ref[i] for i in range(len(chains))]
    qk = [_dot_nt(q.astype(BF16), k.astype(BF16)) * dm_ref[i] for i, (q, k) in enumerate(zip(qs, ks))]
    o_inter = [_dot((q * qd_ref[i]).astype(BF16), s.astype(BF16)) for i, (q, s) in enumerate(zip(qs, s_prev))]
    kv = [_dot((k * kd_ref[i]).T.astype(BF16), v) for i, (k, v) in enumerate(zip(ks, vs))]
    for i, (d, h) in enumerate(chains):
        dirs[d][5][0, :, sls[h]] = _dot(qk[i].astype(BF16), vs[i]) + o_inter[i]
        chunk_decay = jnp.exp(jnp.full((1, HEAD_DIM), float(c), F32) * lg_ref[d, h])
        s_ref[i] = s_prev[i] * chunk_decay + kv[i]


def _ret_call(log_gamma, p_main, cos2, sin2, *, l_lat):
    b, l_all, _ = p_main.shape
    c = RET_CHUNK
    n_all, n_lat = l_all // c, l_lat // c
    n_ctx = n_all - n_lat

    def fwd(s):
        return jnp.where(s < n_ctx, n_lat + s, s - n_ctx)

    def bwd(s):
        return n_all - 1 - s

    def pspec(colblk, order):
        return pl.BlockSpec((1, c, RET_W), lambda bb, s: (bb, order(s), colblk))

    def tspec(order):
        return pl.BlockSpec((c, HEAD_DIM), lambda bb, s: (order(s), 0))

    in_specs = [pl.BlockSpec(memory_space=pltpu.SMEM)]
    for order in (fwd, bwd):
        in_specs += [pspec(COL_RET_Q // RET_W, order), pspec(COL_RET_K // RET_W, order),
                     pspec(COL_RET_V // RET_W, order), tspec(order), tspec(order)]
    nhd = 2 * RET_HEADS
    return pl.pallas_call(
        _ret_kernel,
        grid=(b, n_all),
        in_specs=in_specs,
        out_specs=[pl.BlockSpec((1, c, RET_W), lambda bb, s: (bb, fwd(s), 0)),
                   pl.BlockSpec((1, c, RET_W), lambda bb, s: (bb, bwd(s), 0))],
        out_shape=[jax.ShapeDtypeStruct((b, l_all, RET_W), F32)] * 2,
        scratch_shapes=[pltpu.VMEM((nhd, HEAD_DIM, HEAD_DIM), F32), pltpu.VMEM((nhd, c, c), F32),
                        pltpu.VMEM((nhd, c, HEAD_DIM), F32), pltpu.VMEM((nhd, c, HEAD_DIM), F32)],
        compiler_params=_params(("arbitrary", "arbitrary")),
        name="retention_scan",
    )(log_gamma, p_main, p_main, p_main, cos2, sin2, p_main, p_main, p_main, cos2, sin2)


def _attprep_kernel(q_ref, k_ref, v_ref, c_ref, s_ref, qw_ref, kw_ref, qo_ref, ko_ref, vo_ref):
    cos2 = c_ref[...]
    sin2 = s_ref[...]

    def norm_rope(x, w):
        xf = x.astype(F32)
        y = xf * lax.rsqrt(jnp.mean(xf * xf, -1, keepdims=True) + EPS) * w
        return _rope(y, cos2, sin2)

    for h in range(ATT_HEADS):
        sl = slice(h * HEAD_DIM, (h + 1) * HEAD_DIM)
        qo_ref[0, :, sl] = (norm_rope(q_ref[0, :, sl], qw_ref[...]) * (QK_SCALE * LOG2_E)).astype(BF16)
    for h in range(ATT_KV_HEADS):
        sl = slice(h * HEAD_DIM, (h + 1) * HEAD_DIM)
        ko_ref[0, sl, :] = norm_rope(k_ref[0, :, sl], kw_ref[...]).T.astype(BF16)
        vo_ref[0, :, 2 * h * HEAD_DIM:(2 * h + 1) * HEAD_DIM] = v_ref[0, :, sl]
        vo_ref[0, :, (2 * h + 1) * HEAD_DIM:(2 * h + 2) * HEAD_DIM] = jnp.ones((v_ref.shape[1], HEAD_DIM), BF16)


def _attprep_call(p_main, cos2, sin2, qn_w, kn_w, *, tm):
    b, l_all, _ = p_main.shape
    return pl.pallas_call(
        _attprep_kernel,
        grid=(b, l_all // tm),
        in_specs=[pl.BlockSpec((1, tm, ATT_W), lambda bb, i: (bb, i, COL_ATT_Q // ATT_W)),
                  pl.BlockSpec((1, tm, ATT_KV_W), lambda bb, i: (bb, i, COL_ATT_K // ATT_KV_W)),
                  pl.BlockSpec((1, tm, ATT_KV_W), lambda bb, i: (bb, i, COL_ATT_V // ATT_KV_W)),
                  pl.BlockSpec((tm, HEAD_DIM), lambda bb, i: (i, 0)),
                  pl.BlockSpec((tm, HEAD_DIM), lambda bb, i: (i, 0)),
                  pl.BlockSpec((1, HEAD_DIM), lambda bb, i: (0, 0)),
                  pl.BlockSpec((1, HEAD_DIM), lambda bb, i: (0, 0))],
        out_specs=[pl.BlockSpec((1, tm, ATT_W), lambda bb, i: (bb, i, 0)),
                   pl.BlockSpec((1, ATT_KV_W, tm), lambda bb, i: (bb, 0, i)),
                   pl.BlockSpec((1, tm, 2 * ATT_KV_W), lambda bb, i: (bb, i, 0))],
        out_shape=[jax.ShapeDtypeStruct((b, l_all, ATT_W), BF16),
                   jax.ShapeDtypeStruct((b, ATT_KV_W, l_all), BF16),
                   jax.ShapeDtypeStruct((b, l_all, 2 * ATT_KV_W), BF16)],
        compiler_params=_params(("arbitrary", "arbitrary")),
        name="attn_prep",
    )(p_main, p_main, p_main, cos2, sin2, qn_w, kn_w)


def _att_kernel(*refs, tk):
    q_ref, kt_ref, v_ref = refs[0], refs[1], refs[2]
    o_ref, s_ref, mx_ref, m_ref, acc_ref = refs[-5], refs[-4], refs[-3], refs[-2], refs[-1]
    tq = q_ref.shape[1]
    n = kt_ref.shape[2] // tk
    lanes = tk // HEAD_DIM
    rb = 64

    def put_scores(c):
        col = pl.multiple_of(c * tk, tk)
        s_ref[c] = _dot(q_ref[0], kt_ref[0, :, pl.ds(col, tk)])

    def fold_max(c):
        for r in range(tq // rb):
            rows = slice(r * rb, (r + 1) * rb)
            acc = mx_ref[rows, :]
            for j in range(lanes):
                acc = jnp.maximum(acc, s_ref[c, rows, j * HEAD_DIM:(j + 1) * HEAD_DIM])
            mx_ref[rows, :] = acc

    mx_ref[...] = jnp.full(mx_ref.shape, -jnp.inf, F32)
    put_scores(0)

    @pl.loop(1, n)
    def _(c):
        put_scores(c)
        fold_max(c - 1)

    fold_max(n - 1)
    m_ref[...] = jnp.max(mx_ref[...], -1, keepdims=True)

    acc_ref[...] = jnp.zeros_like(acc_ref)

    @pl.loop(0, n)
    def _(c):
        row = pl.multiple_of(c * tk, tk)
        p = jnp.exp2((s_ref[c] - m_ref[...]).astype(BF16))
        acc_ref[...] += _dot(p, v_ref[0, pl.ds(row, tk), :])

    o_ref[0] = (acc_ref[:, :HEAD_DIM] / acc_ref[:, HEAD_DIM:]).astype(BF16)


def _key_chunk(n_keys):
    for tk in (2816, 768, 512, 256):
        if n_keys % tk == 0:
            return tk
    raise ValueError("key count must be a multiple of 256")


def _att_call(qn, kn, v1, *, l_lat, tq):
    b, l_all, _ = qn.shape
    l_ctx = l_all - l_lat
    group = ATT_HEADS // ATT_KV_HEADS
    tk = _key_chunk(l_all)
    y_lat = pl.pallas_call(
        functools.partial(_att_kernel, tk=tk),
        grid=(b, ATT_HEADS, l_lat // tq),
        in_specs=[pl.BlockSpec((1, tq, HEAD_DIM), lambda bb, h, i: (bb, i, h)),
                  pl.BlockSpec((1, HEAD_DIM, l_all), lambda bb, h, i: (bb, h // group, 0)),
                  pl.BlockSpec((1, l_all, 2 * HEAD_DIM), lambda bb, h, i: (bb, 0, h // group))],
        out_specs=pl.BlockSpec((1, tq, HEAD_DIM), lambda bb, h, i: (bb, i, h)),
        out_shape=jax.ShapeDtypeStruct((b, l_all, ATT_W), BF16),
        scratch_shapes=[pltpu.VMEM((l_all // tk, tq, tk), F32), pltpu.VMEM((tq, HEAD_DIM), F32), pltpu.VMEM((tq, 1), F32),
                        pltpu.VMEM((tq, 2 * HEAD_DIM), F32)],
        compiler_params=_params(("arbitrary", "arbitrary", "arbitrary")),
        name="attention",
    )(qn, kn, v1)
    ctx_blk = l_lat // l_ctx
    return pl.pallas_call(
        functools.partial(_att_kernel, tk=l_ctx),
        grid=(b, ATT_HEADS),
        in_specs=[pl.BlockSpec((1, l_ctx, HEAD_DIM), lambda bb, h: (bb, ctx_blk, h)),
                  pl.BlockSpec((1, HEAD_DIM, l_ctx), lambda bb, h: (bb, h // group, ctx_blk)),
                  pl.BlockSpec((1, l_ctx, 2 * HEAD_DIM), lambda bb, h: (bb, ctx_blk, h // group)),
                  pl.BlockSpec(memory_space=pl.ANY)],
        out_specs=pl.BlockSpec((1, l_ctx, HEAD_DIM), lambda bb, h: (bb, ctx_blk, h)),
        out_shape=jax.ShapeDtypeStruct((b, l_all, ATT_W), BF16),
        scratch_shapes=[pltpu.VMEM((1, l_ctx, l_ctx), F32), pltpu.VMEM((l_ctx, HEAD_DIM), F32), pltpu.VMEM((l_ctx, 1), F32),
                        pltpu.VMEM((l_ctx, 2 * HEAD_DIM), F32)],
        input_output_aliases={3: 0},
        compiler_params=_params(("arbitrary", "arbitrary")),
        name="attention_ctx",
    )(qn, kn, v1, y_lat)


def _dnprep_kernel(*refs, tm, l_lat, l_all):
    mains, prevs, nexts = refs[0:3], refs[3:6], refs[6:9]
    cw_ref, pab_ref, alog_ref, dtb_ref = refs[9:13]
    outs = refs[13:16]
    g_ref, ext_ref = refs[16], refs[17]
    i = pl.program_id(1)
    halo = 16
    pad = DN_CONV_K // 2
    first = jnp.logical_or(i == 0, i == l_lat // tm)
    last = jnp.logical_or(i == l_lat // tm - 1, i == l_all // tm - 1)

    keep_prev = jnp.where(first, 0.0, 1.0)
    keep_next = jnp.where(last, 0.0, 1.0)

    for part in range(3):
        ext_ref[0:halo, :] = prevs[part][0].astype(F32) * keep_prev
        ext_ref[halo:halo + tm, :] = mains[part][0].astype(F32)
        ext_ref[halo + tm:, :] = nexts[part][0].astype(F32) * keep_next
        acc = jnp.zeros((tm, DN_W), F32)
        for j in range(DN_CONV_K):
            w_j = cw_ref[j:j + 1, part * DN_W:(part + 1) * DN_W]
            acc = acc + w_j * ext_ref[pl.ds(halo - pad + j, tm), :]
        y = _silu(acc)
        for h in range(DN_HEADS):
            sl = slice(h * HEAD_DIM, (h + 1) * HEAD_DIM)
            yh = y[:, sl]
            if part < 2:
                yh = yh * lax.rsqrt(jnp.sum(yh * yh, -1, keepdims=True) + EPS)
            if part == 0:
                yh = yh * QK_SCALE
            outs[part][0, :, sl] = yh.astype(BF16)

    a = pab_ref[0]
    col = lax.broadcasted_iota(jnp.int32, a.shape, 1)
    z = a + dtb_ref[...]
    softplus = jnp.maximum(z, 0.0) + jnp.log(1.0 + jnp.exp(-jnp.abs(z)))
    g_ref[0] = jnp.where(col < N_DN_GATES, -jnp.exp(alog_ref[...]) * softplus, jax.nn.sigmoid(a))


def _dnprep_call(p_main, p_ab, conv_w, alog_row, dtb_row, *, l_lat, tm):
    b, l_all, _ = p_main.shape
    halo = 16
    r = tm // halo
    cb = COL_DN_QKV // DN_W
    in_specs = []
    for part in range(3):
        in_specs.append(pl.BlockSpec((1, tm, DN_W), lambda bb, i, p=part: (bb, i, cb + p)))
    for part in range(3):
        in_specs.append(pl.BlockSpec((1, halo, DN_W), lambda bb, i, p=part: (bb, jnp.maximum(i * r - 1, 0), cb + p)))
    for part in range(3):
        in_specs.append(pl.BlockSpec(
            (1, halo, DN_W), lambda bb, i, p=part: (bb, jnp.minimum((i + 1) * r, l_all // halo - 1), cb + p)))
    in_specs += [pl.BlockSpec((DN_CONV_K, 3 * DN_W), lambda bb, i: (0, 0)),
                 pl.BlockSpec((1, tm, GATE_COLS), lambda bb, i: (bb, i, 0)),
                 pl.BlockSpec((1, GATE_COLS), lambda bb, i: (0, 0)),
                 pl.BlockSpec((1, GATE_COLS), lambda bb, i: (0, 0))]
    kern = functools.partial(_dnprep_kernel, tm=tm, l_lat=l_lat, l_all=l_all)
    return pl.pallas_call(
        kern,
        grid=(b, l_all // tm),
        in_specs=in_specs,
        out_specs=[pl.BlockSpec((1, tm, DN_W), lambda bb, i: (bb, i, 0))] * 3
        + [pl.BlockSpec((1, tm, GATE_COLS), lambda bb, i: (bb, i, 0))],
        out_shape=[jax.ShapeDtypeStruct((b, l_all, DN_W), BF16)] * 3
        + [jax.ShapeDtypeStruct((b, l_all, GATE_COLS), F32)],
        scratch_shapes=[pltpu.VMEM((tm + 2 * halo, DN_W), F32)],
        compiler_params=_params(("arbitrary", "arbitrary")),
        name="deltanet_conv_gates",
    )(*([p_main] * 9), conv_w, p_ab, alog_row, dtb_row)


def _neumann_inverse_many(a_list, eye):
    ps = [-a for a in a_list]
    ts = [eye + p for p in ps]
    for _ in range(5):
        pbs = [p.astype(BF16) for p in ps]
        ps = [_dot(pb, pb) for pb in pbs]
        ts = [t + _dot(t.astype(BF16), p.astype(BF16)) for t, p in zip(ts, ps)]
    return ts


def _dnchunk_kernel(q_ref, k_ref, v_ref, g_ref, gt_ref, w_ref, kc_ref, qg_ref, kgt_ref, qk_ref, gl_ref):
    c = DN_CHUNK
    hi = lax.Precision.HIGHEST
    ii = lax.broadcasted_iota(jnp.int32, (c, c), 0)
    jj = lax.broadcasted_iota(jnp.int32, (c, c), 1)
    lower = (ii >= jj).astype(F32)
    upper = (ii <= jj).astype(F32)
    eye = (ii == jj).astype(F32)
    eye_hd = (lax.broadcasted_iota(jnp.int32, (HEAD_DIM, HEAD_DIM), 0)
              == lax.broadcasted_iota(jnp.int32, (HEAD_DIM, HEAD_DIM), 1)).astype(BF16)
    gcol = g_ref[0]
    grow = gt_ref[0, 0]
    cum_col = (jnp.dot(lower, gcol, preferred_element_type=F32, precision=hi),
               jnp.dot(upper, gcol, preferred_element_type=F32, precision=hi))
    cum_row = (jnp.dot(grow, upper, preferred_element_type=F32, precision=hi),
               jnp.dot(grow, lower, preferred_element_type=F32, precision=hi))
    tot = jnp.dot(grow, jnp.ones((c, HEAD_DIM), F32), preferred_element_type=F32, precision=hi)

    qk_ref[...] = jnp.zeros_like(qk_ref)
    kgt_ref[...] = jnp.zeros_like(kgt_ref)
    gl_ref[...] = jnp.zeros_like(gl_ref)
    for d in range(2):
        gl_ref[0, d, 0, 0:DN_HEADS, :] = jnp.exp(tot[d * DN_HEADS:(d + 1) * DN_HEADS, :])

    heads = range(DN_HEADS)
    chains = [(d, h) for h in heads for d in range(2)]
    sls = [slice(h * HEAD_DIM, (h + 1) * HEAD_DIM) for h in heads]
    q16 = [q_ref[0, :, sl] for sl in sls]
    k16 = [k_ref[0, :, sl] for sl in sls]
    kk = [_dot_nt(k16[h], k16[h]) for h in heads]
    qk0 = [_dot_nt(q16[h], k16[h]) for h in heads]

    gcc, beta, decay, a_list = {}, {}, {}, []
    for d, h in chains:
        col = d * DN_HEADS + h
        gcc[d, h] = cum_col[d][:, col:col + 1]
        gcr = cum_row[d][col:col + 1, :]
        beta[d, h] = gcol[:, N_DN_GATES + col:N_DN_GATES + col + 1]
        incl = (ii >= jj) if d == 0 else (ii <= jj)
        strict = (ii > jj) if d == 0 else (ii < jj)
        decay[d, h] = jnp.where(incl, jnp.exp(jnp.where(incl, gcc[d, h] - gcr, 0.0)), 0.0)
        a_list.append(jnp.where(strict, kk[h] * beta[d, h] * decay[d, h], 0.0))

    t16 = [t.astype(BF16) for t in _neumann_inverse_many(a_list, eye)]

    for (d, h), t in zip(chains, t16):
        sl = sls[h]
        col = d * DN_HEADS + h
        k = k16[h].astype(F32)
        e_col = jnp.exp(gcc[d, h])
        w_ref[0, d, :, sl] = _dot(t, (v_ref[0, :, sl].astype(F32) * beta[d, h]).astype(BF16))
        kc_ref[0, d, :, sl] = _dot(t, (k * (beta[d, h] * e_col)).astype(BF16)).astype(BF16)
        qg_ref[0, d, :, sl] = (q16[h].astype(F32) * e_col).astype(BF16)
        kg = (k * jnp.exp(tot[col:col + 1, 0:1] - gcc[d, h])).astype(BF16)
        kgt_ref[0, d, 0, :, pl.ds(h * HEAD_DIM, c)] = _dot_nt(eye_hd, kg).astype(BF16)
        qk_ref[0, d, :, pl.ds(h * HEAD_DIM, c)] = (qk0[h] * decay[d, h]).astype(BF16)


def _dnchunk_call(qd, kd, vd, g, gt):
    b, l_all, _ = qd.shape
    c = DN_CHUNK
    n = l_all // c
    tok = lambda dt: jax.ShapeDtypeStruct((b, 2, l_all, DN_W), dt)
    tok_spec = pl.BlockSpec((1, 2, c, DN_W), lambda bb, s: (bb, 0, s, 0))
    return pl.pallas_call(
        _dnchunk_kernel,
        grid=(b, n),
        in_specs=[pl.BlockSpec((1, c, DN_W), lambda bb, s: (bb, s, 0))] * 3
        + [pl.BlockSpec((1, c, GATE_COLS), lambda bb, s: (bb, s, 0)),
           pl.BlockSpec((1, 1, 2 * N_DN_GATES, c), lambda bb, s: (bb, s, 0, 0))],
        out_specs=[tok_spec, tok_spec, tok_spec,
                   pl.BlockSpec((1, 2, 1, HEAD_DIM, DN_W), lambda bb, s: (bb, 0, s, 0, 0)),
                   tok_spec,
                   pl.BlockSpec((1, 2, 1, 8, HEAD_DIM), lambda bb, s: (bb, 0, s, 0, 0))],
        out_shape=[tok(F32), tok(BF16), tok(BF16),
                   jax.ShapeDtypeStruct((b, 2, n, HEAD_DIM, DN_W), BF16),
                   tok(BF16),
                   jax.ShapeDtypeStruct((b, 2, n, 8, HEAD_DIM), F32)],
        compiler_params=_params(("arbitrary", "arbitrary")),
        name="deltanet_chunk_factors",
    )(qd, kd, vd, g, gt)


def _dnscan_kernel(*refs):
    ins_f, ins_b = refs[0:6], refs[6:12]
    of_ref, ob_ref, s_ref = refs[12], refs[13], refs[14]
    c = DN_CHUNK
    step = pl.program_id(1)

    @pl.when(step == 0)
    def _():
        s_ref[...] = jnp.zeros_like(s_ref)

    ins, outs = (ins_f, ins_b), (of_ref, ob_ref)
    chains = [(d, h) for d in range(2) for h in range(DN_HEADS)]
    sls = [slice(h * HEAD_DIM, (h + 1) * HEAD_DIM) for h in range(DN_HEADS)]
    s_prev = [s_ref[d * DN_HEADS + h] for d, h in chains]
    s16 = [s.astype(BF16) for s in s_prev]
    v16 = [(ins[d][0][0, 0, :, sls[h]] - _dot(ins[d][1][0, 0, :, sls[h]], s)).astype(BF16)
           for (d, h), s in zip(chains, s16)]
    o_inter = [_dot(ins[d][2][0, 0, :, sls[h]], s) for (d, h), s in zip(chains, s16)]
    for (d, h), s, v, oi in zip(chains, s_prev, v16, o_inter):
        qk = ins[d][4][0, 0, :, pl.ds(h * HEAD_DIM, c)]
        outs[d][0, :, sls[h]] = oi + _dot(qk, v)
        kgt = ins[d][3][0, 0, 0, :, pl.ds(h * HEAD_DIM, c)]
        s_ref[d * DN_HEADS + h] = s * ins[d][5][0, 0, 0, h:h + 1, :] + _dot(kgt, v)


def _dnscan_call(w, kc, qg, kgt, qk, gl, *, l_lat):
    b, _, l_all, _ = w.shape
    c = DN_CHUNK
    n_all, n_lat = l_all // c, l_lat // c
    n_ctx = n_all - n_lat

    def fwd(s):
        return jnp.where(s < n_ctx, n_lat + s, s - n_ctx)

    def bwd(s):
        return n_all - 1 - s

    in_specs = []
    for d, order in enumerate((fwd, bwd)):
        tok_spec = pl.BlockSpec((1, 1, c, DN_W), lambda bb, s, d=d, o=order: (bb, d, o(s), 0))
        in_specs += [tok_spec, tok_spec, tok_spec,
                     pl.BlockSpec((1, 1, 1, HEAD_DIM, DN_W), lambda bb, s, d=d, o=order: (bb, d, o(s), 0, 0)),
                     tok_spec,
                     pl.BlockSpec((1, 1, 1, 8, HEAD_DIM), lambda bb, s, d=d, o=order: (bb, d, o(s), 0, 0))]
    return pl.pallas_call(
        _dnscan_kernel,
        grid=(b, n_all),
        in_specs=in_specs,
        out_specs=[pl.BlockSpec((1, c, DN_W), lambda bb, s: (bb, fwd(s), 0)),
                   pl.BlockSpec((1, c, DN_W), lambda bb, s: (bb, bwd(s), 0))],
        out_shape=[jax.ShapeDtypeStruct((b, l_all, DN_W), F32)] * 2,
        scratch_shapes=[pltpu.VMEM((2 * DN_HEADS, HEAD_DIM, HEAD_DIM), F32)],
        compiler_params=_params(("arbitrary", "arbitrary")),
        name="deltanet_scan",
    )(w, kc, qg, kgt, qk, gl, w, kc, qg, kgt, qk, gl)


def _mixout_kernel(rf_ref, rb_ref, rg_ref, df_ref, db_ref, dz_ref, nw_ref, y_ref):
    def head_norm(o):
        return o * lax.rsqrt(jnp.mean(o * o, -1, keepdims=True) + EPS)

    for h in range(RET_HEADS):
        sl = slice(h * HEAD_DIM, (h + 1) * HEAD_DIM)
        o = rf_ref[0, :, sl] + rb_ref[0, :, sl]
        y_ref[0, :, sl] = (head_norm(o) * _silu(rg_ref[0, :, sl].astype(F32))).astype(BF16)
    for h in range(DN_HEADS):
        sl = slice(h * HEAD_DIM, (h + 1) * HEAD_DIM)
        o = df_ref[0, :, sl] + db_ref[0, :, sl]
        y = head_norm(o) * nw_ref[...] * _silu(dz_ref[0, :, sl].astype(F32))
        y_ref[0, :, RET_W + h * HEAD_DIM:RET_W + (h + 1) * HEAD_DIM] = y.astype(BF16)


def _mixout_call(ret_f, ret_b, dn_f, dn_b, p_main, dn_norm_w, *, tm):
    b, l_all, _ = ret_f.shape
    o_spec = pl.BlockSpec((1, tm, RET_W), lambda bb, i: (bb, i, 0))
    return pl.pallas_call(
        _mixout_kernel,
        grid=(b, l_all // tm),
        in_specs=[o_spec, o_spec,
                  pl.BlockSpec((1, tm, RET_W), lambda bb, i: (bb, i, COL_RET_G // RET_W)),
                  o_spec, o_spec,
                  pl.BlockSpec((1, tm, DN_W), lambda bb, i: (bb, i, COL_DN_Z // DN_W)),
                  pl.BlockSpec((1, HEAD_DIM), lambda bb, i: (0, 0))],
        out_specs=pl.BlockSpec((1, tm, RET_W + DN_W), lambda bb, i: (bb, i, 0)),
        out_shape=jax.ShapeDtypeStruct((b, l_all, RET_W + DN_W), BF16),
        compiler_params=_params(("arbitrary", "arbitrary")),
        name="mixer_out_norm",
    )(ret_f, ret_b, p_main, dn_f, dn_b, p_main, dn_norm_w)


def _rope_tables(l_lat, l_ctx):
    rows = l_lat // GRID_W
    row = jnp.repeat(jnp.arange(rows, dtype=F32), GRID_W)
    col = jnp.tile(jnp.arange(GRID_W, dtype=F32), rows)
    n_freq = HEAD_DIM // 4
    inv = ROPE_THETA ** (-jnp.arange(n_freq, dtype=F32) / n_freq)
    ang = jnp.concatenate([row[:, None] * inv, col[:, None] * inv], -1)
    cos, sin = jnp.cos(ang), jnp.sin(ang)
    cos2 = jnp.concatenate([cos, cos], -1)
    sin2 = jnp.concatenate([-sin, sin], -1)
    cos2 = jnp.concatenate([cos2, jnp.ones((l_ctx, HEAD_DIM), F32)], 0)
    sin2 = jnp.concatenate([sin2, jnp.zeros((l_ctx, HEAD_DIM), F32)], 0)
    return cos2, sin2


def _pad_row(v, width):
    v = v.reshape(1, -1).astype(F32)
    return jnp.pad(v, ((0, 0), (0, width - v.shape[1])))


def _row_tile(l_lat, l_ctx):
    l_all = l_lat + l_ctx
    for tm in (768, 512, 256, 128):
        if l_all % tm == 0:
            return tm
    raise ValueError("token count must be a multiple of 128")


def kernel(x, c, ctx, c_ctx, w_ada, b_ada, w_in, ret_decay_logit, dn_conv_w, dn_a_log, dn_dt_bias, dn_norm_w,
           att_qn_w, att_kn_w, w_o, ln1_w, ln1_b, w_ffn_in, w_ffn_out, ln2_w, ln2_b):
    bsz, l_lat, d = x.shape
    l_ctx = ctx.shape[1]
    l_all = l_lat + l_ctx
    depth = w_ada.shape[0]
    assert l_lat % 256 == 0 and l_ctx % 256 == 0 and l_lat % GRID_W == 0
    tm = _row_tile(l_lat, l_ctx)
    tn = MAIN_W // 4

    cos2, sin2 = _rope_tables(l_lat, l_ctx)
    x_all = jnp.concatenate([x, ctx], axis=1)

    cond_rows = 8 * pl.cdiv(bsz + 1, 8)
    cond = jnp.concatenate([c, c_ctx[None, :]], 0)
    cond = jnp.pad(cond, ((0, cond_rows - bsz - 1), (0, 0)))
    mod = _ada_call(cond, w_ada, b_ada)

    gate_lo = COL_DN_Z + DN_W
    for i in range(depth):
        last = i == depth - 1
        ml = mod[i, :bsz].reshape(bsz, 6, d)
        mc = mod[i, bsz].reshape(1, 6, d)
        w_main = jnp.concatenate([w_in[i, :, :gate_lo], w_in[i, :, gate_lo + 2 * N_DN_GATES:]], 1).astype(BF16)
        w_ab = jnp.pad(w_in[i, :, gate_lo:gate_lo + 2 * N_DN_GATES],
                       ((0, 0), (0, GATE_COLS - 2 * N_DN_GATES))).astype(BF16)

        p_main, p_ab = _inproj_call(x_all, ml, mc, w_main, w_ab, l_lat=l_lat, tm=tm, tn=tn)

        log_gamma = jax.nn.log_sigmoid(ret_decay_logit[i].astype(F32))
        ret_f, ret_b = _ret_call(log_gamma, p_main, cos2, sin2, l_lat=l_lat)

        qn, kn, v1 = _attprep_call(p_main, cos2, sin2, att_qn_w[i].reshape(1, -1), att_kn_w[i].reshape(1, -1),
                                   tm=tm)
        y_att = _att_call(qn, kn, v1, l_lat=l_lat, tq=512 if l_lat % 512 == 0 else 256)

        qd, kd, vd, g = _dnprep_call(p_main, p_ab, dn_conv_w[i], _pad_row(dn_a_log[i], GATE_COLS),
                                     _pad_row(dn_dt_bias[i], GATE_COLS), l_lat=l_lat, tm=256)
        gt = g[:, :, :2 * N_DN_GATES].reshape(bsz, l_all // DN_CHUNK, DN_CHUNK, 2 * N_DN_GATES)
        gt = jnp.swapaxes(gt, 2, 3)
        dn_f, dn_b = _dnscan_call(*_dnchunk_call(qd, kd, vd, g, gt), l_lat=l_lat)

        y_rd = _mixout_call(ret_f, ret_b, dn_f, dn_b, p_main, dn_norm_w[i].reshape(1, -1), tm=256)

        x_all = _wo_call(y_rd, y_att, x_all, ml, mc, w_o[i].astype(BF16), ln1_w[i].reshape(1, -1),
                         ln1_b[i].reshape(1, -1), l_lat=l_lat, tm=tm)
        x_all = _ffn_call(x_all, ml, mc, w_ffn_in[i].astype(BF16), w_ffn_out[i].astype(BF16),
                          ln2_w[i].reshape(1, -1), ln2_b[i].reshape(1, -1), l_lat=l_lat,
                          out_rows=l_lat if last else l_all, tm=tm, tf=512)
    return x_all
```

```python
import functools

import jax
import jax.numpy as jnp
from jax import lax
from jax.experimental import pallas as pl
from jax.experimental.pallas import tpu as pltpu

F32 = jnp.float32
BF16 = jnp.bfloat16

HEAD_DIM = 128
RET_HEADS = 4
DN_HEADS = 4
ATT_HEADS = 8
ATT_KV_HEADS = 2
RET_W = RET_HEADS * HEAD_DIM
DN_W = DN_HEADS * HEAD_DIM
ATT_W = ATT_HEADS * HEAD_DIM
ATT_KV_W = ATT_KV_HEADS * HEAD_DIM
RET_CHUNK = 128
DN_CHUNK = 64
DN_CONV_K = 5
GRID_W = 64
ROPE_THETA = 10000.0
MODEL_DEPTH = 4
DEEPNORM_ALPHA = (2 * MODEL_DEPTH) ** 0.25
EPS = 1e-6
QK_SCALE = HEAD_DIM ** -0.5
LOG2_E = 1.4426950408889634

COL_RET_Q = 0
COL_RET_K = RET_W
COL_RET_V = 2 * RET_W
COL_RET_G = 3 * RET_W
COL_DN_QKV = 4 * RET_W
COL_DN_Z = COL_DN_QKV + 3 * DN_W
COL_ATT_Q = COL_DN_Z + DN_W
COL_ATT_K = COL_ATT_Q + ATT_W
COL_ATT_V = COL_ATT_K + ATT_KV_W
MAIN_W = COL_ATT_V + ATT_KV_W
GATE_COLS = 128
N_DN_GATES = 2 * DN_HEADS

VMEM_LIMIT_MB = 56


def _params(sem, vmem_mb=VMEM_LIMIT_MB):
    return pltpu.CompilerParams(dimension_semantics=sem, vmem_limit_bytes=vmem_mb * 1024 * 1024)


def _dot(a, b):
    return jnp.dot(a, b, preferred_element_type=F32)


def _dot_nt(a, b):
    return lax.dot_general(a, b, (((1,), (1,)), ((), ())), preferred_element_type=F32)


def _silu(x):
    return x * jax.nn.sigmoid(x)


def _is_ctx_rows(i, tm, l_lat):
    rows = i * tm + lax.broadcasted_iota(jnp.int32, (tm, 1), 0)
    return rows >= l_lat


def _mod_row(is_ctx, ml_ref, mc_ref, k):
    return jnp.where(is_ctx, mc_ref[0, k:k + 1, :], ml_ref[0, k:k + 1, :])


def _layer_norm(r, w, b):
    mu = jnp.mean(r, -1, keepdims=True)
    rc = r - mu
    var = jnp.mean(rc * rc, -1, keepdims=True)
    return rc * lax.rsqrt(var + EPS) * w + b


def _ada_kernel(c_ref, w_ref, b_ref, o_ref):
    h = _silu(c_ref[...])
    o_ref[0] = jnp.dot(h, w_ref[0], preferred_element_type=F32, precision=lax.Precision.HIGHEST) + b_ref[0]


def _ada_call(cond, w_ada, b_ada):
    depth, d, n6 = w_ada.shape
    rows = cond.shape[0]
    tn = 1024
    return pl.pallas_call(
        _ada_kernel,
        grid=(depth, n6 // tn),
        in_specs=[pl.BlockSpec((rows, d), lambda l, j: (0, 0)),
                  pl.BlockSpec((1, d, tn), lambda l, j: (l, 0, j)),
                  pl.BlockSpec((1, 1, tn), lambda l, j: (l, 0, j))],
        out_specs=pl.BlockSpec((1, rows, tn), lambda l, j: (l, 0, j)),
        out_shape=jax.ShapeDtypeStruct((depth, rows, n6), F32),
        compiler_params=_params(("arbitrary", "arbitrary")),
        name="ada_mod",
    )(cond, w_ada, b_ada.reshape(depth, 1, n6))


def _inproj_kernel(x_ref, ml_ref, mc_ref, w_ref, wab_ref, pm_ref, pab_ref, h_ref, *, tm, l_lat):
    i = pl.program_id(1)
    j = pl.program_id(2)

    @pl.when(j == 0)
    def _():
        is_ctx = _is_ctx_rows(i, tm, l_lat)
        shift = _mod_row(is_ctx, ml_ref, mc_ref, 0)
        scale = _mod_row(is_ctx, ml_ref, mc_ref, 1)
        h = (x_ref[0] * (1.0 + scale) + shift).astype(BF16)
        h_ref[...] = h
        pab_ref[0] = _dot(h, wab_ref[...])

    tn = w_ref.shape[1]
    step = 512
    for lo in range(0, tn, step):
        hi = min(lo + step, tn)
        pm_ref[0, :, lo:hi] = _dot(h_ref[...], w_ref[:, lo:hi]).astype(BF16)


def _inproj_call(x_all, ml, mc, w_main, w_ab, *, l_lat, tm, tn):
    b, l_all, d = x_all.shape
    kern = functools.partial(_inproj_kernel, tm=tm, l_lat=l_lat)
    return pl.pallas_call(
        kern,
        grid=(b, l_all // tm, MAIN_W // tn),
        in_specs=[pl.BlockSpec((1, tm, d), lambda bb, i, j: (bb, i, 0)),
                  pl.BlockSpec((1, 6, d), lambda bb, i, j: (bb, 0, 0)),
                  pl.BlockSpec((1, 6, d), lambda bb, i, j: (0, 0, 0)),
                  pl.BlockSpec((d, tn), lambda bb, i, j: (0, j)),
                  pl.BlockSpec((d, GATE_COLS), lambda bb, i, j: (0, 0))],
        out_specs=[pl.BlockSpec((1, tm, tn), lambda bb, i, j: (bb, i, j)),
                   pl.BlockSpec((1, tm, GATE_COLS), lambda bb, i, j: (bb, i, 0))],
        out_shape=[jax.ShapeDtypeStruct((b, l_all, MAIN_W), BF16),
                   jax.ShapeDtypeStruct((b, l_all, GATE_COLS), F32)],
        scratch_shapes=[pltpu.VMEM((tm, d), BF16)],
        compiler_params=_params(("arbitrary", "arbitrary", "arbitrary")),
        name="in_proj",
    )(x_all, ml, mc, w_main, w_ab)


def _wo_kernel(y_ref, a_ref, x_ref, ml_ref, mc_ref, w_ref, lnw_ref, lnb_ref, o_ref, *, tm, l_lat):
    i = pl.program_id(1)
    half = y_ref.shape[-1]
    acc = _dot(y_ref[0], w_ref[0:half, :]) + _dot(a_ref[0], w_ref[half:, :])
    is_ctx = _is_ctx_rows(i, tm, l_lat)
    gate = _mod_row(is_ctx, ml_ref, mc_ref, 2)
    r = DEEPNORM_ALPHA * x_ref[0] + gate * acc
    o_ref[0] = _layer_norm(r, lnw_ref[...], lnb_ref[...])


def _wo_call(y_rd, y_att, x_all, ml, mc, w_o, ln_w, ln_b, *, l_lat, tm):
    b, l_all, d = x_all.shape
    half = y_rd.shape[-1]
    kern = functools.partial(_wo_kernel, tm=tm, l_lat=l_lat)
    return pl.pallas_call(
        kern,
        grid=(b, l_all // tm),
        in_specs=[pl.BlockSpec((1, tm, half), lambda bb, i: (bb, i, 0)),
                  pl.BlockSpec((1, tm, half), lambda bb, i: (bb, i, 0)),
                  pl.BlockSpec((1, tm, d), lambda bb, i: (bb, i, 0)),
                  pl.BlockSpec((1, 6, d), lambda bb, i: (bb, 0, 0)),
                  pl.BlockSpec((1, 6, d), lambda bb, i: (0, 0, 0)),
                  pl.BlockSpec((2 * half, d), lambda bb, i: (0, 0)),
                  pl.BlockSpec((1, d), lambda bb, i: (0, 0)),
                  pl.BlockSpec((1, d), lambda bb, i: (0, 0))],
        out_specs=pl.BlockSpec((1, tm, d), lambda bb, i: (bb, i, 0)),
        out_shape=jax.ShapeDtypeStruct((b, l_all, d), F32),
        compiler_params=_params(("arbitrary", "arbitrary")),
        name="w_o_postnorm",
    )(y_rd, y_att, x_all, ml, mc, w_o, ln_w, ln_b)


def _ffn_kernel(x_ref, ml_ref, mc_ref, wg_ref, wu_ref, wo_ref, lnw_ref, lnb_ref, o_ref, h_ref, acc_ref,
                *, tm, l_lat):
    i = pl.program_id(1)
    f = pl.program_id(2)

    @pl.when(f == 0)
    def _():
        is_ctx = _is_ctx_rows(i, tm, l_lat)
        shift = _mod_row(is_ctx, ml_ref, mc_ref, 3)
        scale = _mod_row(is_ctx, ml_ref, mc_ref, 4)
        h_ref[...] = (x_ref[0] * (1.0 + scale) + shift).astype(BF16)
        acc_ref[...] = jnp.zeros_like(acc_ref)

    h = h_ref[...]
    g = _dot(h, wg_ref[...])
    u = _dot(h, wu_ref[...])
    acc_ref[...] += _dot((_silu(g) * u).astype(BF16), wo_ref[...])

    @pl.when(f == pl.num_programs(2) - 1)
    def _():
        is_ctx = _is_ctx_rows(i, tm, l_lat)
        gate = _mod_row(is_ctx, ml_ref, mc_ref, 5)
        r = DEEPNORM_ALPHA * x_ref[0] + gate * acc_ref[...]
        o_ref[0] = _layer_norm(r, lnw_ref[...], lnb_ref[...])


def _ffn_call(x_all, ml, mc, w_in, w_out, ln_w, ln_b, *, l_lat, out_rows, tm, tf):
    b, _, d = x_all.shape
    d_ff = w_out.shape[0]
    nf = d_ff // tf
    kern = functools.partial(_ffn_kernel, tm=tm, l_lat=l_lat)
    return pl.pallas_call(
        kern,
        grid=(b, pl.cdiv(out_rows, tm), nf),
        in_specs=[pl.BlockSpec((1, tm, d), lambda bb, i, f: (bb, i, 0)),
                  pl.BlockSpec((1, 6, d), lambda bb, i, f: (bb, 0, 0)),
                  pl.BlockSpec((1, 6, d), lambda bb, i, f: (0, 0, 0)),
                  pl.BlockSpec((d, tf), lambda bb, i, f: (0, f)),
                  pl.BlockSpec((d, tf), lambda bb, i, f: (0, f + nf)),
                  pl.BlockSpec((tf, d), lambda bb, i, f: (f, 0)),
                  pl.BlockSpec((1, d), lambda bb, i, f: (0, 0)),
                  pl.BlockSpec((1, d), lambda bb, i, f: (0, 0))],
        out_specs=pl.BlockSpec((1, tm, d), lambda bb, i, f: (bb, i, 0)),
        out_shape=jax.ShapeDtypeStruct((b, out_rows, d), F32),
        scratch_shapes=[pltpu.VMEM((tm, d), BF16), pltpu.VMEM((tm, d), F32)],
        compiler_params=_params(("arbitrary", "arbitrary", "arbitrary")),
        name="ffn_postnorm",
    )(x_all, ml, mc, w_in, w_in, w_out, ln_w, ln_b)


def _rope(x, cos2, sin2):
    return x * cos2 + pltpu.roll(x, HEAD_DIM // 2, 1) * sin2


def _ret_kernel(lg_ref, qf_ref, kf_ref, vf_ref, cf_ref, sf_ref, qb_ref, kb_ref, vb_ref, cb_ref, sb_ref,
                of_ref, ob_ref, s_ref, dm_ref, qd_ref, kd_ref):
    step = pl.program_id(1)
    c = RET_CHUNK

    @pl.when(step == 0)
    def _():
        s_ref[...] = jnp.zeros_like(s_ref)
        ii = lax.broadcasted_iota(jnp.int32, (c, c), 0).astype(F32)
        jj = lax.broadcasted_iota(jnp.int32, (c, c), 1).astype(F32)
        for d in range(2):
            for h in range(RET_HEADS):
                lg = lg_ref[d, h]
                if d == 0:
                    rel, qe, ke = ii - jj, ii + 1.0, (c - 1.0) - ii
                else:
                    rel, qe, ke = jj - ii, c - ii, ii
                idx = d * RET_HEADS + h
                dm_ref[idx] = jnp.where(rel >= 0, jnp.exp(jnp.maximum(rel, 0.0) * lg), 0.0)
                qd_ref[idx] = jnp.exp(qe * lg)
                kd_ref[idx] = jnp.exp(ke * lg)

    dirs = ((qf_ref, kf_ref, vf_ref, cf_ref, sf_ref, of_ref), (qb_ref, kb_ref, vb_ref, cb_ref, sb_ref, ob_ref))
    chains = [(d, h) for d in range(2) for h in range(RET_HEADS)]
    sls = [slice(h * HEAD_DIM, (h + 1) * HEAD_DIM) for h in range(RET_HEADS)]
    tabs = [(dirs[d][3][...], dirs[d][4][...]) for d in range(2)]
    qs = [_rope(dirs[d][0][0, :, sls[h]].astype(F32), *tabs[d]) for d, h in chains]
    ks = [_rope(dirs[d][1][0, :, sls[h]].astype(F32), *tabs[d]) * QK_SCALE for d, h in chains]
    vs = [dirs[d][2][0, :, sls[h]] for d, h in chains]
    s_prev = [s_ref[i] for i in range(len(chains))]
    qk = [_dot_nt(q.astype(BF16), k.astype(BF16)) * dm_ref[i] for i, (q, k) in enumerate(zip(qs, ks))]
    o_inter = [_dot((q * qd_ref[i]).astype(BF16), s.astype(BF16)) for i, (q, s) in enumerate(zip(qs, s_prev))]
    kv = [_dot((k * kd_ref[i]).T.astype(BF16), v) for i, (k, v) in enumerate(zip(ks, vs))]
    for i, (d, h) in enumerate(chains):
        dirs[d][5][0, :, sls[h]] = _dot(qk[i].astype(BF16), vs[i]) + o_inter[i]
        chunk_decay = jnp.exp(jnp.full((1, HEAD_DIM), float(c), F32) * lg_ref[d, h])
        s_ref[i] = s_prev[i] * chunk_decay + kv[i]


def _ret_call(log_gamma, p_main, cos2, sin2, *, l_lat):
    b, l_all, _ = p_main.shape
    c = RET_CHUNK
    n_all, n_lat = l_all // c, l_lat // c
    n_ctx = n_all - n_lat

    def fwd(s):
        return jnp.where(s < n_ctx, n_lat + s, s - n_ctx)

    def bwd(s):
        return n_all - 1 - s

    def pspec(colblk, order):
        return pl.BlockSpec((1, c, RET_W), lambda bb, s: (bb, order(s), colblk))

    def tspec(order):
        return pl.BlockSpec((c, HEAD_DIM), lambda bb, s: (order(s), 0))

    in_specs = [pl.BlockSpec(memory_space=pltpu.SMEM)]
    for order in (fwd, bwd):
        in_specs += [pspec(COL_RET_Q // RET_W, order), pspec(COL_RET_K // RET_W, order),
                     pspec(COL_RET_V // RET_W, order), tspec(order), tspec(order)]
    nhd = 2 * RET_HEADS
    return pl.pallas_call(
        _ret_kernel,
        grid=(b, n_all),
        in_specs=in_specs,
        out_specs=[pl.BlockSpec((1, c, RET_W), lambda bb, s: (bb, fwd(s), 0)),
                   pl.BlockSpec((1, c, RET_W), lambda bb, s: (bb, bwd(s), 0))],
        out_shape=[jax.ShapeDtypeStruct((b, l_all, RET_W), F32)] * 2,
        scratch_shapes=[pltpu.VMEM((nhd, HEAD_DIM, HEAD_DIM), F32), pltpu.VMEM((nhd, c, c), F32),
                        pltpu.VMEM((nhd, c, HEAD_DIM), F32), pltpu.VMEM((nhd, c, HEAD_DIM), F32)],
        compiler_params=_params(("arbitrary", "arbitrary")),
        name="retention_scan",
    )(log_gamma, p_main, p_main, p_main, cos2, sin2, p_main, p_main, p_main, cos2, sin2)


def _attprep_kernel(q_ref, k_ref, v_ref, c_ref, s_ref, qw_ref, kw_ref, qo_ref, ko_ref, vo_ref):
    cos2 = c_ref[...]
    sin2 = s_ref[...]

    def norm_rope(x, w):
        xf = x.astype(F32)
        y = xf * lax.rsqrt(jnp.mean(xf * xf, -1, keepdims=True) + EPS) * w
        return _rope(y, cos2, sin2)

    for h in range(ATT_HEADS):
        sl = slice(h * HEAD_DIM, (h + 1) * HEAD_DIM)
        qo_ref[0, :, sl] = (norm_rope(q_ref[0, :, sl], qw_ref[...]) * (QK_SCALE * LOG2_E)).astype(BF16)
    for h in range(ATT_KV_HEADS):
        sl = slice(h * HEAD_DIM, (h + 1) * HEAD_DIM)
        ko_ref[0, sl, :] = norm_rope(k_ref[0, :, sl], kw_ref[...]).T.astype(BF16)
        vo_ref[0, :, 2 * h * HEAD_DIM:(2 * h + 1) * HEAD_DIM] = v_ref[0, :, sl]
        vo_ref[0, :, (2 * h + 1) * HEAD_DIM:(2 * h + 2) * HEAD_DIM] = jnp.ones((v_ref.shape[1], HEAD_DIM), BF16)


def _attprep_call(p_main, cos2, sin2, qn_w, kn_w, *, tm):
    b, l_all, _ = p_main.shape
    return pl.pallas_call(
        _attprep_kernel,
        grid=(b, l_all // tm),
        in_specs=[pl.BlockSpec((1, tm, ATT_W), lambda bb, i: (bb, i, COL_ATT_Q // ATT_W)),
                  pl.BlockSpec((1, tm, ATT_KV_W), lambda bb, i: (bb, i, COL_ATT_K // ATT_KV_W)),
                  pl.BlockSpec((1, tm, ATT_KV_W), lambda bb, i: (bb, i, COL_ATT_V // ATT_KV_W)),
                  pl.BlockSpec((tm, HEAD_DIM), lambda bb, i: (i, 0)),
                  pl.BlockSpec((tm, HEAD_DIM), lambda bb, i: (i, 0)),
                  pl.BlockSpec((1, HEAD_DIM), lambda bb, i: (0, 0)),
                  pl.BlockSpec((1, HEAD_DIM), lambda bb, i: (0, 0))],
        out_specs=[pl.BlockSpec((1, tm, ATT_W), lambda bb, i: (bb, i, 0)),
                   pl.BlockSpec((1, ATT_KV_W, tm), lambda bb, i: (bb, 0, i)),
                   pl.BlockSpec((1, tm, 2 * ATT_KV_W), lambda bb, i: (bb, i, 0))],
        out_shape=[jax.ShapeDtypeStruct((b, l_all, ATT_W), BF16),
                   jax.ShapeDtypeStruct((b, ATT_KV_W, l_all), BF16),
                   jax.ShapeDtypeStruct((b, l_all, 2 * ATT_KV_W), BF16)],
        compiler_params=_params(("arbitrary", "arbitrary")),
        name="attn_prep",
    )(p_main, p_main, p_main, cos2, sin2, qn_w, kn_w)


def _att_kernel(*refs, tk):
    q_ref, kt_ref, v_ref = refs[0], refs[1], refs[2]
    o_ref, s_ref, mx_ref, m_ref, acc_ref = refs[-5], refs[-4], refs[-3], refs[-2], refs[-1]
    tq = q_ref.shape[1]
    n = kt_ref.shape[2] // tk
    lanes = tk // HEAD_DIM
    rb = 64

    def put_scores(c):
        col = pl.multiple_of(c * tk, tk)
        s_ref[c] = _dot(q_ref[0], kt_ref[0, :, pl.ds(col, tk)])

    def fold_max(c):
        for r in range(tq // rb):
            rows = slice(r * rb, (r + 1) * rb)
            acc = mx_ref[rows, :]
            for j in range(lanes):
                acc = jnp.maximum(acc, s_ref[c, rows, j * HEAD_DIM:(j + 1) * HEAD_DIM])
            mx_ref[rows, :] = acc

    mx_ref[...] = jnp.full(mx_ref.shape, -jnp.inf, F32)
    put_scores(0)

    @pl.loop(1, n)
    def _(c):
        put_scores(c)
        fold_max(c - 1)

    fold_max(n - 1)
    m_ref[...] = jnp.max(mx_ref[...], -1, keepdims=True)

    acc_ref[...] = jnp.zeros_like(acc_ref)

    @pl.loop(0, n)
    def _(c):
        row = pl.multiple_of(c * tk, tk)
        p = jnp.exp2((s_ref[c] - m_ref[...]).astype(BF16))
        acc_ref[...] += _dot(p, v_ref[0, pl.ds(row, tk), :])

    o_ref[0] = (acc_ref[:, :HEAD_DIM] / acc_ref[:, HEAD_DIM:]).astype(BF16)


def _key_chunk(n_keys):
    for tk in (4224, 768, 512, 256):
        if n_keys % tk == 0:
            return tk
    raise ValueError("key count must be a multiple of 256")


def _att_call(qn, kn, v1, *, l_lat, tq):
    b, l_all, _ = qn.shape
    l_ctx = l_all - l_lat
    group = ATT_HEADS // ATT_KV_HEADS
    tk = _key_chunk(l_all)
    y_lat = pl.pallas_call(
        functools.partial(_att_kernel, tk=tk),
        grid=(b, ATT_HEADS, l_lat // tq),
        in_specs=[pl.BlockSpec((1, tq, HEAD_DIM), lambda bb, h, i: (bb, i, h)),
                  pl.BlockSpec((1, HEAD_DIM, l_all), lambda bb, h, i: (bb, h // group, 0)),
                  pl.BlockSpec((1, l_all, 2 * HEAD_DIM), lambda bb, h, i: (bb, 0, h // group))],
        out_specs=pl.BlockSpec((1, tq, HEAD_DIM), lambda bb, h, i: (bb, i, h)),
        out_shape=jax.ShapeDtypeStruct((b, l_all, ATT_W), BF16),
        scratch_shapes=[pltpu.VMEM((l_all // tk, tq, tk), F32), pltpu.VMEM((tq, HEAD_DIM), F32), pltpu.VMEM((tq, 1), F32),
                        pltpu.VMEM((tq, 2 * HEAD_DIM), F32)],
        compiler_params=_params(("arbitrary", "arbitrary", "arbitrary")),
        name="attention",
    )(qn, kn, v1)
    ctx_blk = l_lat // l_ctx
    return pl.pallas_call(
        functools.partial(_att_kernel, tk=l_ctx),
        grid=(b, ATT_HEADS),
        in_specs=[pl.BlockSpec((1, l_ctx, HEAD_DIM), lambda bb, h: (bb, ctx_blk, h)),
                  pl.BlockSpec((1, HEAD_DIM, l_ctx), lambda bb, h: (bb, h // group, ctx_blk)),
                  pl.BlockSpec((1, l_ctx, 2 * HEAD_DIM), lambda bb, h: (bb, ctx_blk, h // group)),
                  pl.BlockSpec(memory_space=pl.ANY)],
        out_specs=pl.BlockSpec((1, l_ctx, HEAD_DIM), lambda bb, h: (bb, ctx_blk, h)),
        out_shape=jax.ShapeDtypeStruct((b, l_all, ATT_W), BF16),
        scratch_shapes=[pltpu.VMEM((1, l_ctx, l_ctx), F32), pltpu.VMEM((l_ctx, HEAD_DIM), F32), pltpu.VMEM((l_ctx, 1), F32),
                        pltpu.VMEM((l_ctx, 2 * HEAD_DIM), F32)],
        input_output_aliases={3: 0},
        compiler_params=_params(("arbitrary", "arbitrary")),
        name="attention_ctx",
    )(qn, kn, v1, y_lat)


def _dnprep_kernel(*refs, tm, l_lat, l_all):
    mains, prevs, nexts = refs[0:3], refs[3:6], refs[6:9]
    cw_ref, pab_ref, alog_ref, dtb_ref = refs[9:13]
    outs = refs[13:16]
    g_ref, ext_ref = refs[16], refs[17]
    i = pl.program_id(1)
    halo = 16
    pad = DN_CONV_K // 2
    first = jnp.logical_or(i == 0, i == l_lat // tm)
    last = jnp.logical_or(i == l_lat // tm - 1, i == l_all // tm - 1)

    keep_prev = jnp.where(first, 0.0, 1.0)
    keep_next = jnp.where(last, 0.0, 1.0)

    for part in range(3):
        ext_ref[0:halo, :] = prevs[part][0].astype(F32) * keep_prev
        ext_ref[halo:halo + tm, :] = mains[part][0].astype(F32)
        ext_ref[halo + tm:, :] = nexts[part][0].astype(F32) * keep_next
        acc = jnp.zeros((tm, DN_W), F32)
        for j in range(DN_CONV_K):
            w_j = cw_ref[j:j + 1, part * DN_W:(part + 1) * DN_W]
            acc = acc + w_j * ext_ref[pl.ds(halo - pad + j, tm), :]
        y = _silu(acc)
        for h in range(DN_HEADS):
            sl = slice(h * HEAD_DIM, (h + 1) * HEAD_DIM)
            yh = y[:, sl]
            if part < 2:
                yh = yh * lax.rsqrt(jnp.sum(yh * yh, -1, keepdims=True) + EPS)
            if part == 0:
                yh = yh * QK_SCALE
            outs[part][0, :, sl] = yh.astype(BF16)

    a = pab_ref[0]
    col = lax.broadcasted_iota(jnp.int32, a.shape, 1)
    z = a + dtb_ref[...]
    softplus = jnp.maximum(z, 0.0) + jnp.log(1.0 + jnp.exp(-jnp.abs(z)))
    g = -jnp.exp(alog_ref[...]) * softplus
    ii = lax.broadcasted_iota(jnp.int32, (tm, tm), 0)
    rr = lax.broadcasted_iota(jnp.int32, (tm, tm), 1)
    same_chunk = (ii // DN_CHUNK) == (rr // DN_CHUNK)
    prefix = jnp.logical_and(same_chunk, rr <= ii).astype(F32)
    suffix = jnp.logical_and(same_chunk, rr >= ii).astype(F32)
    hi = lax.Precision.HIGHEST
    cum_f = jnp.dot(prefix, g, preferred_element_type=F32, precision=hi)
    cum_b = jnp.dot(suffix, g, preferred_element_type=F32, precision=hi)
    g_ref[0] = jnp.where(col < DN_HEADS, cum_f, jnp.where(col < N_DN_GATES, cum_b, jax.nn.sigmoid(a)))


def _dnprep_call(p_main, p_ab, conv_w, alog_row, dtb_row, *, l_lat, tm):
    b, l_all, _ = p_main.shape
    halo = 16
    r = tm // halo
    cb = COL_DN_QKV // DN_W
    in_specs = []
    for part in range(3):
        in_specs.append(pl.BlockSpec((1, tm, DN_W), lambda bb, i, p=part: (bb, i, cb + p)))
    for part in range(3):
        in_specs.append(pl.BlockSpec((1, halo, DN_W), lambda bb, i, p=part: (bb, jnp.maximum(i * r - 1, 0), cb + p)))
    for part in range(3):
        in_specs.append(pl.BlockSpec(
            (1, halo, DN_W), lambda bb, i, p=part: (bb, jnp.minimum((i + 1) * r, l_all // halo - 1), cb + p)))
    in_specs += [pl.BlockSpec((DN_CONV_K, 3 * DN_W), lambda bb, i: (0, 0)),
                 pl.BlockSpec((1, tm, GATE_COLS), lambda bb, i: (bb, i, 0)),
                 pl.BlockSpec((1, GATE_COLS), lambda bb, i: (0, 0)),
                 pl.BlockSpec((1, GATE_COLS), lambda bb, i: (0, 0))]
    kern = functools.partial(_dnprep_kernel, tm=tm, l_lat=l_lat, l_all=l_all)
    return pl.pallas_call(
        kern,
        grid=(b, l_all // tm),
        in_specs=in_specs,
        out_specs=[pl.BlockSpec((1, tm, DN_W), lambda bb, i: (bb, i, 0))] * 3
        + [pl.BlockSpec((1, tm, GATE_COLS), lambda bb, i: (bb, i, 0))],
        out_shape=[jax.ShapeDtypeStruct((b, l_all, DN_W), BF16)] * 3
        + [jax.ShapeDtypeStruct((b, l_all, GATE_COLS), F32)],
        scratch_shapes=[pltpu.VMEM((tm + 2 * halo, DN_W), F32)],
        compiler_params=_params(("arbitrary", "arbitrary")),
        name="deltanet_conv_gates",
    )(*([p_main] * 9), conv_w, p_ab, alog_row, dtb_row)


def _neumann_inverse_many(a_list, eye):
    ps = [-a for a in a_list]
    ts = [eye + p for p in ps]
    for _ in range(5):
        pbs = [p.astype(BF16) for p in ps]
        ps = [_dot(pb, pb) for pb in pbs]
        ts = [t + _dot(t.astype(BF16), p.astype(BF16)) for t, p in zip(ts, ps)]
    return ts


def _dnchunk_kernel(q_ref, k_ref, v_ref, g_ref, gt_ref, w_ref, kc_ref, qg_ref, kgt_ref, qk_ref, gl_ref):
    c = DN_CHUNK
    n_sub = q_ref.shape[1] // c
    ii = lax.broadcasted_iota(jnp.int32, (c, c), 0)
    jj = lax.broadcasted_iota(jnp.int32, (c, c), 1)
    eye = (ii == jj).astype(F32)

    qk_ref[...] = jnp.zeros_like(qk_ref)
    kgt_ref[...] = jnp.zeros_like(kgt_ref)
    gl_ref[...] = jnp.zeros_like(gl_ref)

    heads = range(DN_HEADS)
    subs = range(n_sub)
    chains = [(u, d, h) for u in subs for h in heads for d in range(2)]
    rows = [slice(u * c, (u + 1) * c) for u in subs]
    sls = [slice(h * HEAD_DIM, (h + 1) * HEAD_DIM) for h in heads]
    gcol = [g_ref[0, rows[u], :] for u in subs]
    grow = [gt_ref[0, u] for u in subs]
    q16 = {(u, h): q_ref[0, rows[u], sls[h]] for u in subs for h in heads}
    k16 = {(u, h): k_ref[0, rows[u], sls[h]] for u in subs for h in heads}
    kk = {uh: _dot_nt(k16[uh], k16[uh]) for uh in k16}
    qk0 = {uh: _dot_nt(q16[uh], k16[uh]) for uh in k16}

    gcc, tot, beta, decay, a_list = {}, {}, {}, {}, []
    for u, d, h in chains:
        col = d * DN_HEADS + h
        key = (u, d, h)
        gcc[key] = gcol[u][:, col:col + 1]
        gcr = grow[u][col:col + 1, :]
        tot[key] = gcol[u][c - 1:c, col:col + 1] if d == 0 else gcol[u][0:1, col:col + 1]
        beta[key] = gcol[u][:, N_DN_GATES + col:N_DN_GATES + col + 1]
        incl = (ii >= jj) if d == 0 else (ii <= jj)
        strict = (ii > jj) if d == 0 else (ii < jj)
        decay[key] = jnp.where(incl, jnp.exp(jnp.where(incl, gcc[key] - gcr, 0.0)), 0.0)
        a_list.append(jnp.where(strict, kk[u, h] * beta[key] * decay[key], 0.0))
        gl_ref[0, d, u, h:h + 1, :] = jnp.broadcast_to(jnp.exp(tot[key]), (1, HEAD_DIM))

    t16 = [t.astype(BF16) for t in _neumann_inverse_many(a_list, eye)]

    for key, t in zip(chains, t16):
        u, d, h = key
        sl = sls[h]
        k = k16[u, h].astype(F32)
        e_col = jnp.exp(gcc[key])
        rhs = jnp.concatenate([(v_ref[0, rows[u], sl].astype(F32) * beta[key]).astype(BF16),
                               (k * (beta[key] * e_col)).astype(BF16)], axis=1)
        wk = _dot(t, rhs)
        w_ref[0, d, rows[u], sl] = wk[:, :HEAD_DIM]
        kc_ref[0, d, rows[u], sl] = wk[:, HEAD_DIM:].astype(BF16)
        qg_ref[0, d, rows[u], sl] = (q16[u, h].astype(F32) * e_col).astype(BF16)
        kg = k * jnp.exp(tot[key] - gcc[key])
        kgt_ref[0, d, u, :, pl.ds(h * HEAD_DIM, c)] = kg.T.astype(BF16)
        qk_ref[0, d, rows[u], pl.ds(h * HEAD_DIM, c)] = (qk0[u, h] * decay[key]).astype(BF16)


DN_STEP_CHUNKS = 2


def _dnchunk_call(qd, kd, vd, g, gt):
    b, l_all, _ = qd.shape
    c = DN_CHUNK
    n = l_all // c
    ns = DN_STEP_CHUNKS
    tok = lambda dt: jax.ShapeDtypeStruct((b, 2, l_all, DN_W), dt)
    tok_spec = pl.BlockSpec((1, 2, ns * c, DN_W), lambda bb, s: (bb, 0, s, 0))
    return pl.pallas_call(
        _dnchunk_kernel,
        grid=(b, n // ns),
        in_specs=[pl.BlockSpec((1, ns * c, DN_W), lambda bb, s: (bb, s, 0))] * 3
        + [pl.BlockSpec((1, ns * c, GATE_COLS), lambda bb, s: (bb, s, 0)),
           pl.BlockSpec((1, ns, 2 * N_DN_GATES, c), lambda bb, s: (bb, s, 0, 0))],
        out_specs=[tok_spec, tok_spec, tok_spec,
                   pl.BlockSpec((1, 2, ns, HEAD_DIM, DN_W), lambda bb, s: (bb, 0, s, 0, 0)),
                   tok_spec,
                   pl.BlockSpec((1, 2, ns, 8, HEAD_DIM), lambda bb, s: (bb, 0, s, 0, 0))],
        out_shape=[tok(F32), tok(BF16), tok(BF16),
                   jax.ShapeDtypeStruct((b, 2, n, HEAD_DIM, DN_W), BF16),
                   tok(BF16),
                   jax.ShapeDtypeStruct((b, 2, n, 8, HEAD_DIM), F32)],
        compiler_params=_params(("arbitrary", "arbitrary")),
        name="deltanet_chunk_factors",
    )(qd, kd, vd, g, gt)


def _dnscan_kernel(*refs):
    ins_f, ins_b = refs[0:6], refs[6:12]
    of_ref, ob_ref, s_ref = refs[12], refs[13], refs[14]
    c = DN_CHUNK
    n_sub = of_ref.shape[1] // c
    step = pl.program_id(1)

    @pl.when(step == 0)
    def _():
        s_ref[...] = jnp.zeros_like(s_ref)

    ins, outs = (ins_f, ins_b), (of_ref, ob_ref)
    chains = [(d, h) for d in range(2) for h in range(DN_HEADS)]
    sls = [slice(h * HEAD_DIM, (h + 1) * HEAD_DIM) for h in range(DN_HEADS)]
    state = [s_ref[d * DN_HEADS + h] for d, h in chains]
    for j in range(n_sub):
        sub = [j, n_sub - 1 - j]
        rows = [slice(u * c, (u + 1) * c) for u in sub]
        s16 = [s.astype(BF16) for s in state]
        v16 = [(ins[d][0][0, 0, rows[d], sls[h]] - _dot(ins[d][1][0, 0, rows[d], sls[h]], s)).astype(BF16)
               for (d, h), s in zip(chains, s16)]
        o_inter = [_dot(ins[d][2][0, 0, rows[d], sls[h]], s) for (d, h), s in zip(chains, s16)]
        new_state = []
        for (d, h), s, v, oi in zip(chains, state, v16, o_inter):
            qk = ins[d][4][0, 0, rows[d], pl.ds(h * HEAD_DIM, c)]
            outs[d][0, rows[d], sls[h]] = oi + _dot(qk, v)
            kgt = ins[d][3][0, 0, sub[d], :, pl.ds(h * HEAD_DIM, c)]
            new_state.append(s * ins[d][5][0, 0, sub[d], h:h + 1, :] + _dot(kgt, v))
        state = new_state
    for i, s in enumerate(state):
        s_ref[i] = s


def _dnscan_call(w, kc, qg, kgt, qk, gl, *, l_lat):
    b, _, l_all, _ = w.shape
    ns = DN_STEP_CHUNKS
    blk = ns * DN_CHUNK
    n_all, n_lat = l_all // blk, l_lat // blk
    n_ctx = n_all - n_lat

    def fwd(s):
        return jnp.where(s < n_ctx, n_lat + s, s - n_ctx)

    def bwd(s):
        return n_all - 1 - s

    in_specs = []
    for d, order in enumerate((fwd, bwd)):
        tok_spec = pl.BlockSpec((1, 1, blk, DN_W), lambda bb, s, d=d, o=order: (bb, d, o(s), 0))
        in_specs += [tok_spec, tok_spec, tok_spec,
                     pl.BlockSpec((1, 1, ns, HEAD_DIM, DN_W), lambda bb, s, d=d, o=order: (bb, d, o(s), 0, 0)),
                     tok_spec,
                     pl.BlockSpec((1, 1, ns, 8, HEAD_DIM), lambda bb, s, d=d, o=order: (bb, d, o(s), 0, 0))]
    return pl.pallas_call(
        _dnscan_kernel,
        grid=(b, n_all),
        in_specs=in_specs,
        out_specs=[pl.BlockSpec((1, blk, DN_W), lambda bb, s: (bb, fwd(s), 0)),
                   pl.BlockSpec((1, blk, DN_W), lambda bb, s: (bb, bwd(s), 0))],
        out_shape=[jax.ShapeDtypeStruct((b, l_all, DN_W), F32)] * 2,
        scratch_shapes=[pltpu.VMEM((2 * DN_HEADS, HEAD_DIM, HEAD_DIM), F32)],
        compiler_params=_params(("arbitrary", "arbitrary")),
        name="deltanet_scan",
    )(w, kc, qg, kgt, qk, gl, w, kc, qg, kgt, qk, gl)


def _mixout_kernel(rf_ref, rb_ref, rg_ref, df_ref, db_ref, dz_ref, nw_ref, y_ref):
    def head_norm(o):
        return o * lax.rsqrt(jnp.mean(o * o, -1, keepdims=True) + EPS)

    for h in range(RET_HEADS):
        sl = slice(h * HEAD_DIM, (h + 1) * HEAD_DIM)
        o = rf_ref[0, :, sl] + rb_ref[0, :, sl]
        y_ref[0, :, sl] = (head_norm(o) * _silu(rg_ref[0, :, sl].astype(F32))).astype(BF16)
    for h in range(DN_HEADS):
        sl = slice(h * HEAD_DIM, (h + 1) * HEAD_DIM)
        o = df_ref[0, :, sl] + db_ref[0, :, sl]
        y = head_norm(o) * nw_ref[...] * _silu(dz_ref[0, :, sl].astype(F32))
        y_ref[0, :, RET_W + h * HEAD_DIM:RET_W + (h + 1) * HEAD_DIM] = y.astype(BF16)


def _mixout_call(ret_f, ret_b, dn_f, dn_b, p_main, dn_norm_w, *, tm):
    b, l_all, _ = ret_f.shape
    o_spec = pl.BlockSpec((1, tm, RET_W), lambda bb, i: (bb, i, 0))
    return pl.pallas_call(
        _mixout_kernel,
        grid=(b, l_all // tm),
        in_specs=[o_spec, o_spec,
                  pl.BlockSpec((1, tm, RET_W), lambda bb, i: (bb, i, COL_RET_G // RET_W)),
                  o_spec, o_spec,
                  pl.BlockSpec((1, tm, DN_W), lambda bb, i: (bb, i, COL_DN_Z // DN_W)),
                  pl.BlockSpec((1, HEAD_DIM), lambda bb, i: (0, 0))],
        out_specs=pl.BlockSpec((1, tm, RET_W + DN_W), lambda bb, i: (bb, i, 0)),
        out_shape=jax.ShapeDtypeStruct((b, l_all, RET_W + DN_W), BF16),
        compiler_params=_params(("arbitrary", "arbitrary")),
        name="mixer_out_norm",
    )(ret_f, ret_b, p_main, dn_f, dn_b, p_main, dn_norm_w)


def _rope_tables(l_lat, l_ctx):
    rows = l_lat // GRID_W
    row = jnp.repeat(jnp.arange(rows, dtype=F32), GRID_W)
    col = jnp.tile(jnp.arange(GRID_W, dtype=F32), rows)
    n_freq = HEAD_DIM // 4
    inv = ROPE_THETA ** (-jnp.arange(n_freq, dtype=F32) / n_freq)
    ang = jnp.concatenate([row[:, None] * inv, col[:, None] * inv], -1)
    cos, sin = jnp.cos(ang), jnp.sin(ang)
    cos2 = jnp.concatenate([cos, cos], -1)
    sin2 = jnp.concatenate([-sin, sin], -1)
    cos2 = jnp.concatenate([cos2, jnp.ones((l_ctx, HEAD_DIM), F32)], 0)
    sin2 = jnp.concatenate([sin2, jnp.zeros((l_ctx, HEAD_DIM), F32)], 0)
    return cos2, sin2


def _pad_row(v, width):
    v = v.reshape(1, -1).astype(F32)
    return jnp.pad(v, ((0, 0), (0, width - v.shape[1])))


def _row_tile(l_lat, l_ctx):
    l_all = l_lat + l_ctx
    for tm in (768, 512, 256, 128):
        if l_all % tm == 0:
            return tm
    raise ValueError("token count must be a multiple of 128")


def kernel(x, c, ctx, c_ctx, w_ada, b_ada, w_in, ret_decay_logit, dn_conv_w, dn_a_log, dn_dt_bias, dn_norm_w,
           att_qn_w, att_kn_w, w_o, ln1_w, ln1_b, w_ffn_in, w_ffn_out, ln2_w, ln2_b):
    bsz, l_lat, d = x.shape
    l_ctx = ctx.shape[1]
    l_all = l_lat + l_ctx
    depth = w_ada.shape[0]
    assert l_lat % 256 == 0 and l_ctx % 256 == 0 and l_lat % GRID_W == 0
    tm = _row_tile(l_lat, l_ctx)
    tn = MAIN_W // 2

    cos2, sin2 = _rope_tables(l_lat, l_ctx)
    x_all = jnp.concatenate([x, ctx], axis=1)

    cond_rows = 8 * pl.cdiv(bsz + 1, 8)
    cond = jnp.concatenate([c, c_ctx[None, :]], 0)
    cond = jnp.pad(cond, ((0, cond_rows - bsz - 1), (0, 0)))
    mod = _ada_call(cond, w_ada, b_ada)

    gate_lo = COL_DN_Z + DN_W
    for i in range(depth):
        last = i == depth - 1
        ml = mod[i, :bsz].reshape(bsz, 6, d)
        mc = mod[i, bsz].reshape(1, 6, d)
        w_main = jnp.concatenate([w_in[i, :, :gate_lo], w_in[i, :, gate_lo + 2 * N_DN_GATES:]], 1).astype(BF16)
        w_ab = jnp.pad(w_in[i, :, gate_lo:gate_lo + 2 * N_DN_GATES],
                       ((0, 0), (0, GATE_COLS - 2 * N_DN_GATES))).astype(BF16)

        p_main, p_ab = _inproj_call(x_all, ml, mc, w_main, w_ab, l_lat=l_lat, tm=tm, tn=tn)

        log_gamma = jax.nn.log_sigmoid(ret_decay_logit[i].astype(F32))
        ret_f, ret_b = _ret_call(log_gamma, p_main, cos2, sin2, l_lat=l_lat)

        qn, kn, v1 = _attprep_call(p_main, cos2, sin2, att_qn_w[i].reshape(1, -1), att_kn_w[i].reshape(1, -1),
                                   tm=tm)
        y_att = _att_call(qn, kn, v1, l_lat=l_lat, tq=512 if l_lat % 512 == 0 else 256)

        qd, kd, vd, g = _dnprep_call(p_main, p_ab, dn_conv_w[i], _pad_row(dn_a_log[i], GATE_COLS),
                                     _pad_row(dn_dt_bias[i], GATE_COLS), l_lat=l_lat, tm=256)
        gt = g[:, :, :2 * N_DN_GATES].reshape(bsz, l_all // DN_CHUNK, DN_CHUNK, 2 * N_DN_GATES)
        gt = jnp.swapaxes(gt, 2, 3)
        dn_f, dn_b = _dnscan_call(*_dnchunk_call(qd, kd, vd, g, gt), l_lat=l_lat)

        y_rd = _mixout_call(ret_f, ret_b, dn_f, dn_b, p_main, dn_norm_w[i].reshape(1, -1), tm=256)

        x_all = _wo_call(y_rd, y_att, x_all, ml, mc, w_o[i].astype(BF16), ln1_w[i].reshape(1, -1),
                         ln1_b[i].reshape(1, -1), l_lat=l_lat, tm=tm)
        x_all = _ffn_call(x_all, ml, mc, w_ffn_in[i].astype(BF16), w_ffn_out[i].astype(BF16),
                          ln2_w[i].reshape(1, -1), ln2_b[i].reshape(1, -1), l_lat=l_lat,
                          out_rows=l_lat if last else l_all, tm=tm, tf=512)
    return x_all
```

```python
import functools

import jax
import jax.numpy as jnp
from jax import lax
from jax.experimental import pallas as pl
from jax.experimental.pallas import tpu as pltpu

F32 = jnp.float32
BF16 = jnp.bfloat16

HEAD_DIM = 128
RET_HEADS = 4
DN_HEADS = 4
ATT_HEADS = 8
ATT_KV_HEADS = 2
RET_W = RET_HEADS * HEAD_DIM
DN_W = DN_HEADS * HEAD_DIM
ATT_W = ATT_HEADS * HEAD_DIM
ATT_KV_W = ATT_KV_HEADS * HEAD_DIM
RET_CHUNK = 128
DN_CHUNK = 64
DN_CONV_K = 5
GRID_W = 64
ROPE_THETA = 10000.0
MODEL_DEPTH = 4
DEEPNORM_ALPHA = (2 * MODEL_DEPTH) ** 0.25
EPS = 1e-6
QK_SCALE = HEAD_DIM ** -0.5
LOG2_E = 1.4426950408889634

COL_RET_Q = 0
COL_RET_K = RET_W
COL_RET_V = 2 * RET_W
COL_RET_G = 3 * RET_W
COL_DN_QKV = 4 * RET_W
COL_DN_Z = COL_DN_QKV + 3 * DN_W
COL_ATT_Q = COL_DN_Z + DN_W
COL_ATT_K = COL_ATT_Q + ATT_W
COL_ATT_V = COL_ATT_K + ATT_KV_W
MAIN_W = COL_ATT_V + ATT_KV_W
GATE_COLS = 128
N_DN_GATES = 2 * DN_HEADS

VMEM_LIMIT_MB = 56


def _params(sem, vmem_mb=VMEM_LIMIT_MB):
    return pltpu.CompilerParams(dimension_semantics=sem, vmem_limit_bytes=vmem_mb * 1024 * 1024)


def _dot(a, b):
    return jnp.dot(a, b, preferred_element_type=F32)


def _dot_nt(a, b):
    return lax.dot_general(a, b, (((1,), (1,)), ((), ())), preferred_element_type=F32)


def _silu(x):
    return x * jax.nn.sigmoid(x)


def _is_ctx_rows(i, tm, l_lat):
    rows = i * tm + lax.broadcasted_iota(jnp.int32, (tm, 1), 0)
    return rows >= l_lat


def _mod_row(is_ctx, ml_ref, mc_ref, k):
    return jnp.where(is_ctx, mc_ref[0, k:k + 1, :], ml_ref[0, k:k + 1, :])


def _layer_norm(r, w, b):
    mu = jnp.mean(r, -1, keepdims=True)
    rc = r - mu
    var = jnp.mean(rc * rc, -1, keepdims=True)
    return rc * lax.rsqrt(var + EPS) * w + b


def _ada_kernel(c_ref, w_ref, b_ref, o_ref):
    h = _silu(c_ref[...])
    o_ref[0] = jnp.dot(h, w_ref[0], preferred_element_type=F32, precision=lax.Precision.HIGHEST) + b_ref[0]


def _ada_call(cond, w_ada, b_ada):
    depth, d, n6 = w_ada.shape
    rows = cond.shape[0]
    tn = 1024
    return pl.pallas_call(
        _ada_kernel,
        grid=(depth, n6 // tn),
        in_specs=[pl.BlockSpec((rows, d), lambda l, j: (0, 0)),
                  pl.BlockSpec((1, d, tn), lambda l, j: (l, 0, j)),
                  pl.BlockSpec((1, 1, tn), lambda l, j: (l, 0, j))],
        out_specs=pl.BlockSpec((1, rows, tn), lambda l, j: (l, 0, j)),
        out_shape=jax.ShapeDtypeStruct((depth, rows, n6), F32),
        compiler_params=_params(("arbitrary", "arbitrary")),
        name="ada_mod",
    )(cond, w_ada, b_ada.reshape(depth, 1, n6))


def _inproj_kernel(x_ref, ml_ref, mc_ref, w_ref, wab_ref, pm_ref, pab_ref, h_ref, *, tm, l_lat):
    i = pl.program_id(1)
    j = pl.program_id(2)

    @pl.when(j == 0)
    def _():
        is_ctx = _is_ctx_rows(i, tm, l_lat)
        shift = _mod_row(is_ctx, ml_ref, mc_ref, 0)
        scale = _mod_row(is_ctx, ml_ref, mc_ref, 1)
        h = (x_ref[0] * (1.0 + scale) + shift).astype(BF16)
        h_ref[...] = h
        pab_ref[0] = _dot(h, wab_ref[...])

    tn = w_ref.shape[1]
    step = 512
    for lo in range(0, tn, step):
        hi = min(lo + step, tn)
        pm_ref[0, :, lo:hi] = _dot(h_ref[...], w_ref[:, lo:hi]).astype(BF16)


def _inproj_call(x_all, ml, mc, w_main, w_ab, *, l_lat, tm, tn):
    b, l_all, d = x_all.shape
    kern = functools.partial(_inproj_kernel, tm=tm, l_lat=l_lat)
    return pl.pallas_call(
        kern,
        grid=(b, l_all // tm, MAIN_W // tn),
        in_specs=[pl.BlockSpec((1, tm, d), lambda bb, i, j: (bb, i, 0)),
                  pl.BlockSpec((1, 6, d), lambda bb, i, j: (bb, 0, 0)),
                  pl.BlockSpec((1, 6, d), lambda bb, i, j: (0, 0, 0)),
                  pl.BlockSpec((d, tn), lambda bb, i, j: (0, j)),
                  pl.BlockSpec((d, GATE_COLS), lambda bb, i, j: (0, 0))],
        out_specs=[pl.BlockSpec((1, tm, tn), lambda bb, i, j: (bb, i, j)),
                   pl.BlockSpec((1, tm, GATE_COLS), lambda bb, i, j: (bb, i, 0))],
        out_shape=[jax.ShapeDtypeStruct((b, l_all, MAIN_W), BF16),
                   jax.ShapeDtypeStruct((b, l_all, GATE_COLS), F32)],
        scratch_shapes=[pltpu.VMEM((tm, d), BF16)],
        compiler_params=_params(("arbitrary", "arbitrary", "arbitrary")),
        name="in_proj",
    )(x_all, ml, mc, w_main, w_ab)


WO_SUB_ROWS = 256


def _wo_kernel(rf_ref, rb_ref, rg_ref, df_ref, db_ref, dz_ref, nw_ref, a_ref, x_ref, ml_ref, mc_ref, w_ref,
               lnw_ref, lnb_ref, o_ref, *, tm, l_lat):
    i = pl.program_id(1)
    sub = WO_SUB_ROWS
    n_rd = RET_W + DN_W

    def head_norm(o):
        return o * lax.rsqrt(jnp.mean(o * o, -1, keepdims=True) + EPS)

    for lo in range(0, tm, sub):
        rows = slice(lo, lo + sub)
        ys = []
        for h in range(RET_HEADS):
            sl = slice(h * HEAD_DIM, (h + 1) * HEAD_DIM)
            o = rf_ref[0, rows, sl].astype(F32) + rb_ref[0, rows, sl].astype(F32)
            ys.append((head_norm(o) * _silu(rg_ref[0, rows, sl].astype(F32))).astype(BF16))
        for h in range(DN_HEADS):
            sl = slice(h * HEAD_DIM, (h + 1) * HEAD_DIM)
            o = df_ref[0, rows, sl].astype(F32) + db_ref[0, rows, sl].astype(F32)
            ys.append((head_norm(o) * nw_ref[...] * _silu(dz_ref[0, rows, sl].astype(F32))).astype(BF16))
        y_rd = jnp.concatenate(ys, axis=1)
        acc = _dot(y_rd, w_ref[0:n_rd, :]) + _dot(a_ref[0, rows, :], w_ref[n_rd:, :])
        rid = i * tm + lo + lax.broadcasted_iota(jnp.int32, (sub, 1), 0)
        gate = _mod_row(rid >= l_lat, ml_ref, mc_ref, 2)
        r = DEEPNORM_ALPHA * x_ref[0, rows, :] + gate * acc
        o_ref[0, rows, :] = _layer_norm(r, lnw_ref[...], lnb_ref[...])


def _wo_call(ret_f, ret_b, dn_f, dn_b, p_main, dn_norm_w, y_att, x_all, ml, mc, w_o, ln_w, ln_b, *, l_lat, tm):
    b, l_all, d = x_all.shape
    kern = functools.partial(_wo_kernel, tm=tm, l_lat=l_lat)
    o_spec = pl.BlockSpec((1, tm, RET_W), lambda bb, i: (bb, i, 0))
    return pl.pallas_call(
        kern,
        grid=(b, l_all // tm),
        in_specs=[o_spec, o_spec,
                  pl.BlockSpec((1, tm, RET_W), lambda bb, i: (bb, i, COL_RET_G // RET_W)),
                  o_spec, o_spec,
                  pl.BlockSpec((1, tm, DN_W), lambda bb, i: (bb, i, COL_DN_Z // DN_W)),
                  pl.BlockSpec((1, HEAD_DIM), lambda bb, i: (0, 0)),
                  pl.BlockSpec((1, tm, ATT_W), lambda bb, i: (bb, i, 0)),
                  pl.BlockSpec((1, tm, d), lambda bb, i: (bb, i, 0)),
                  pl.BlockSpec((1, 6, d), lambda bb, i: (bb, 0, 0)),
                  pl.BlockSpec((1, 6, d), lambda bb, i: (0, 0, 0)),
                  pl.BlockSpec((RET_W + DN_W + ATT_W, d), lambda bb, i: (0, 0), pipeline_mode=pl.Buffered(1)),
                  pl.BlockSpec((1, d), lambda bb, i: (0, 0)),
                  pl.BlockSpec((1, d), lambda bb, i: (0, 0))],
        out_specs=pl.BlockSpec((1, tm, d), lambda bb, i: (bb, i, 0)),
        out_shape=jax.ShapeDtypeStruct((b, l_all, d), F32),
        compiler_params=_params(("arbitrary", "arbitrary")),
        name="w_o_postnorm",
    )(ret_f, ret_b, p_main, dn_f, dn_b, p_main, dn_norm_w, y_att, x_all, ml, mc, w_o, ln_w, ln_b)


def _ffn_kernel(x_ref, ml_ref, mc_ref, wg_ref, wu_ref, wo_ref, lnw_ref, lnb_ref, o_ref, h_ref, acc_ref,
                *, tm, l_lat):
    i = pl.program_id(1)
    f = pl.program_id(2)

    @pl.when(f == 0)
    def _():
        is_ctx = _is_ctx_rows(i, tm, l_lat)
        shift = _mod_row(is_ctx, ml_ref, mc_ref, 3)
        scale = _mod_row(is_ctx, ml_ref, mc_ref, 4)
        h_ref[...] = (x_ref[0] * (1.0 + scale) + shift).astype(BF16)
        acc_ref[...] = jnp.zeros_like(acc_ref)

    h = h_ref[...]
    g = _dot(h, wg_ref[...])
    u = _dot(h, wu_ref[...])
    acc_ref[...] += _dot((_silu(g) * u).astype(BF16), wo_ref[...])

    @pl.when(f == pl.num_programs(2) - 1)
    def _():
        is_ctx = _is_ctx_rows(i, tm, l_lat)
        gate = _mod_row(is_ctx, ml_ref, mc_ref, 5)
        r = DEEPNORM_ALPHA * x_ref[0] + gate * acc_ref[...]
        o_ref[0] = _layer_norm(r, lnw_ref[...], lnb_ref[...])


def _ffn_call(x_all, ml, mc, w_in, w_out, ln_w, ln_b, *, l_lat, out_rows, tm, tf):
    b, _, d = x_all.shape
    d_ff = w_out.shape[0]
    nf = d_ff // tf
    kern = functools.partial(_ffn_kernel, tm=tm, l_lat=l_lat)
    return pl.pallas_call(
        kern,
        grid=(b, pl.cdiv(out_rows, tm), nf),
        in_specs=[pl.BlockSpec((1, tm, d), lambda bb, i, f: (bb, i, 0)),
                  pl.BlockSpec((1, 6, d), lambda bb, i, f: (bb, 0, 0)),
                  pl.BlockSpec((1, 6, d), lambda bb, i, f: (0, 0, 0)),
                  pl.BlockSpec((d, tf), lambda bb, i, f: (0, f)),
                  pl.BlockSpec((d, tf), lambda bb, i, f: (0, f + nf)),
                  pl.BlockSpec((tf, d), lambda bb, i, f: (f, 0)),
                  pl.BlockSpec((1, d), lambda bb, i, f: (0, 0)),
                  pl.BlockSpec((1, d), lambda bb, i, f: (0, 0))],
        out_specs=pl.BlockSpec((1, tm, d), lambda bb, i, f: (bb, i, 0)),
        out_shape=jax.ShapeDtypeStruct((b, out_rows, d), F32),
        scratch_shapes=[pltpu.VMEM((tm, d), BF16), pltpu.VMEM((tm, d), F32)],
        compiler_params=_params(("arbitrary", "arbitrary", "arbitrary")),
        name="ffn_postnorm",
    )(x_all, ml, mc, w_in, w_in, w_out, ln_w, ln_b)


def _rope(x, cos2, sin2):
    return x * cos2 + pltpu.roll(x, HEAD_DIM // 2, 1) * sin2


def _ret_kernel(lg_ref, qf_ref, kf_ref, vf_ref, cf_ref, sf_ref, qb_ref, kb_ref, vb_ref, cb_ref, sb_ref,
                of_ref, ob_ref, s_ref, dm_ref, qd_ref, kd_ref):
    step = pl.program_id(1)
    c = RET_CHUNK

    @pl.when(step == 0)
    def _():
        s_ref[...] = jnp.zeros_like(s_ref)
        ii = lax.broadcasted_iota(jnp.int32, (c, c), 0).astype(F32)
        jj = lax.broadcasted_iota(jnp.int32, (c, c), 1).astype(F32)
        for d in range(2):
            for h in range(RET_HEADS):
                lg = lg_ref[d, h]
                if d == 0:
                    rel, qe, ke = ii - jj, ii + 1.0, (c - 1.0) - ii
                else:
                    rel, qe, ke = jj - ii, c - ii, ii
                idx = d * RET_HEADS + h
                dm_ref[idx] = jnp.where(rel >= 0, jnp.exp(jnp.maximum(rel, 0.0) * lg), 0.0)
                qd_ref[idx] = jnp.exp(qe * lg)
                kd_ref[idx] = jnp.exp(ke * lg)

    dirs = ((qf_ref, kf_ref, vf_ref, cf_ref, sf_ref, of_ref), (qb_ref, kb_ref, vb_ref, cb_ref, sb_ref, ob_ref))
    n_sub = of_ref.shape[1] // c
    chains = [(d, h) for d in range(2) for h in range(RET_HEADS)]
    sls = [slice(h * HEAD_DIM, (h + 1) * HEAD_DIM) for h in range(RET_HEADS)]
    order = [(j, n_sub - 1 - j) for j in range(n_sub)]
    rows = [[slice(u * c, (u + 1) * c) for u in sub] for sub in order]
    qs, ks, vs, qk, kv = {}, {}, {}, {}, {}
    for j in range(n_sub):
        for i, (d, h) in enumerate(chains):
            r = rows[j][d]
            tab = (dirs[d][3][r, :], dirs[d][4][r, :])
            qs[j, i] = _rope(dirs[d][0][0, r, sls[h]].astype(F32), *tab)
            ks[j, i] = _rope(dirs[d][1][0, r, sls[h]].astype(F32), *tab) * QK_SCALE
            vs[j, i] = dirs[d][2][0, r, sls[h]]
    for key in qs:
        qk[key] = (_dot_nt(qs[key].astype(BF16), ks[key].astype(BF16)) * dm_ref[key[1]]).astype(BF16)
        kv[key] = _dot((ks[key] * kd_ref[key[1]]).T.astype(BF16), vs[key])
    state = [s_ref[i] for i in range(len(chains))]
    for j in range(n_sub):
        o_inter = [_dot((qs[j, i] * qd_ref[i]).astype(BF16), s.astype(BF16)) for i, s in enumerate(state)]
        for i, (d, h) in enumerate(chains):
            dirs[d][5][0, rows[j][d], sls[h]] = (_dot(qk[j, i], vs[j, i]) + o_inter[i]).astype(dirs[d][5].dtype)
            chunk_decay = jnp.exp(jnp.full((1, HEAD_DIM), float(c), F32) * lg_ref[d, h])
            state[i] = state[i] * chunk_decay + kv[j, i]
    for i, st in enumerate(state):
        s_ref[i] = st


RET_STEP_CHUNKS = 2


def _ret_call(log_gamma, p_main, cos2, sin2, *, l_lat):
    b, l_all, _ = p_main.shape
    c = RET_CHUNK
    blk = RET_STEP_CHUNKS * c
    n_all, n_lat = l_all // blk, l_lat // blk
    n_ctx = n_all - n_lat

    def fwd(s):
        return jnp.where(s < n_ctx, n_lat + s, s - n_ctx)

    def bwd(s):
        return n_all - 1 - s

    def pspec(colblk, order):
        return pl.BlockSpec((1, blk, RET_W), lambda bb, s: (bb, order(s), colblk))

    def tspec(order):
        return pl.BlockSpec((blk, HEAD_DIM), lambda bb, s: (order(s), 0))

    in_specs = [pl.BlockSpec(memory_space=pltpu.SMEM)]
    for order in (fwd, bwd):
        in_specs += [pspec(COL_RET_Q // RET_W, order), pspec(COL_RET_K // RET_W, order),
                     pspec(COL_RET_V // RET_W, order), tspec(order), tspec(order)]
    nhd = 2 * RET_HEADS
    return pl.pallas_call(
        _ret_kernel,
        grid=(b, n_all),
        in_specs=in_specs,
        out_specs=[pl.BlockSpec((1, blk, RET_W), lambda bb, s: (bb, fwd(s), 0)),
                   pl.BlockSpec((1, blk, RET_W), lambda bb, s: (bb, bwd(s), 0))],
        out_shape=[jax.ShapeDtypeStruct((b, l_all, RET_W), BF16)] * 2,
        scratch_shapes=[pltpu.VMEM((nhd, HEAD_DIM, HEAD_DIM), F32), pltpu.VMEM((nhd, c, c), F32),
                        pltpu.VMEM((nhd, c, HEAD_DIM), F32), pltpu.VMEM((nhd, c, HEAD_DIM), F32)],
        compiler_params=_params(("arbitrary", "arbitrary")),
        name="retention_scan",
    )(log_gamma, p_main, p_main, p_main, cos2, sin2, p_main, p_main, p_main, cos2, sin2)


def _attprep_kernel(q_ref, k_ref, v_ref, c_ref, s_ref, qw_ref, kw_ref, qo_ref, ko_ref, vo_ref):
    cos2 = c_ref[...]
    sin2 = s_ref[...]

    def norm_rope(x, w):
        xf = x.astype(F32)
        y = xf * lax.rsqrt(jnp.mean(xf * xf, -1, keepdims=True) + EPS) * w
        return _rope(y, cos2, sin2)

    for h in range(ATT_HEADS):
        sl = slice(h * HEAD_DIM, (h + 1) * HEAD_DIM)
        qo_ref[0, :, sl] = (norm_rope(q_ref[0, :, sl], qw_ref[...]) * (QK_SCALE * LOG2_E)).astype(BF16)
    for h in range(ATT_KV_HEADS):
        sl = slice(h * HEAD_DIM, (h + 1) * HEAD_DIM)
        ko_ref[0, sl, :] = norm_rope(k_ref[0, :, sl], kw_ref[...]).T.astype(BF16)
        vo_ref[0, :, 2 * h * HEAD_DIM:(2 * h + 1) * HEAD_DIM] = v_ref[0, :, sl]
        vo_ref[0, :, (2 * h + 1) * HEAD_DIM:(2 * h + 2) * HEAD_DIM] = jnp.ones((v_ref.shape[1], HEAD_DIM), BF16)


def _attprep_call(p_main, cos2, sin2, qn_w, kn_w, *, tm):
    b, l_all, _ = p_main.shape
    return pl.pallas_call(
        _attprep_kernel,
        grid=(b, l_all // tm),
        in_specs=[pl.BlockSpec((1, tm, ATT_W), lambda bb, i: (bb, i, COL_ATT_Q // ATT_W)),
                  pl.BlockSpec((1, tm, ATT_KV_W), lambda bb, i: (bb, i, COL_ATT_K // ATT_KV_W)),
                  pl.BlockSpec((1, tm, ATT_KV_W), lambda bb, i: (bb, i, COL_ATT_V // ATT_KV_W)),
                  pl.BlockSpec((tm, HEAD_DIM), lambda bb, i: (i, 0)),
                  pl.BlockSpec((tm, HEAD_DIM), lambda bb, i: (i, 0)),
                  pl.BlockSpec((1, HEAD_DIM), lambda bb, i: (0, 0)),
                  pl.BlockSpec((1, HEAD_DIM), lambda bb, i: (0, 0))],
        out_specs=[pl.BlockSpec((1, tm, ATT_W), lambda bb, i: (bb, i, 0)),
                   pl.BlockSpec((1, ATT_KV_W, tm), lambda bb, i: (bb, 0, i)),
                   pl.BlockSpec((1, tm, 2 * ATT_KV_W), lambda bb, i: (bb, i, 0))],
        out_shape=[jax.ShapeDtypeStruct((b, l_all, ATT_W), BF16),
                   jax.ShapeDtypeStruct((b, ATT_KV_W, l_all), BF16),
                   jax.ShapeDtypeStruct((b, l_all, 2 * ATT_KV_W), BF16)],
        compiler_params=_params(("arbitrary", "arbitrary")),
        name="attn_prep",
    )(p_main, p_main, p_main, cos2, sin2, qn_w, kn_w)


def _att_kernel(*refs, tk):
    q_ref, kt_ref, v_ref = refs[0], refs[1], refs[2]
    o_ref, s_ref, mx_ref, m_ref, acc_ref = refs[-5], refs[-4], refs[-3], refs[-2], refs[-1]
    tq = q_ref.shape[1]
    n = kt_ref.shape[2] // tk
    lanes = tk // HEAD_DIM
    rb = 64

    def put_scores(c):
        col = pl.multiple_of(c * tk, tk)
        s_ref[c] = _dot(q_ref[0], kt_ref[0, :, pl.ds(col, tk)])

    def fold_max(c):
        for r in range(tq // rb):
            rows = slice(r * rb, (r + 1) * rb)
            acc = mx_ref[rows, :]
            for j in range(lanes):
                acc = jnp.maximum(acc, s_ref[c, rows, j * HEAD_DIM:(j + 1) * HEAD_DIM])
            mx_ref[rows, :] = acc

    mx_ref[...] = jnp.full(mx_ref.shape, -jnp.inf, F32)
    put_scores(0)

    @pl.loop(1, n)
    def _(c):
        put_scores(c)
        fold_max(c - 1)

    fold_max(n - 1)
    m_ref[...] = jnp.max(mx_ref[...], -1, keepdims=True)

    acc_ref[...] = jnp.zeros_like(acc_ref)

    @pl.loop(0, n)
    def _(c):
        row = pl.multiple_of(c * tk, tk)
        p = jnp.exp2((s_ref[c] - m_ref[...]).astype(BF16))
        acc_ref[...] += _dot(p, v_ref[0, pl.ds(row, tk), :])

    o_ref[0] = (acc_ref[:, :HEAD_DIM] / acc_ref[:, HEAD_DIM:]).astype(BF16)


def _key_chunk(n_keys):
    for tk in (4224, 768, 512, 256):
        if n_keys % tk == 0:
            return tk
    raise ValueError("key count must be a multiple of 256")


def _att_call(qn, kn, v1, *, l_lat, tq):
    b, l_all, _ = qn.shape
    l_ctx = l_all - l_lat
    group = ATT_HEADS // ATT_KV_HEADS
    tk = _key_chunk(l_all)
    y_lat = pl.pallas_call(
        functools.partial(_att_kernel, tk=tk),
        grid=(b, ATT_HEADS, l_lat // tq),
        in_specs=[pl.BlockSpec((1, tq, HEAD_DIM), lambda bb, h, i: (bb, i, h)),
                  pl.BlockSpec((1, HEAD_DIM, l_all), lambda bb, h, i: (bb, h // group, 0),
                               pipeline_mode=pl.Buffered(1)),
                  pl.BlockSpec((1, l_all, 2 * HEAD_DIM), lambda bb, h, i: (bb, 0, h // group),
                               pipeline_mode=pl.Buffered(1))],
        out_specs=pl.BlockSpec((1, tq, HEAD_DIM), lambda bb, h, i: (bb, i, h)),
        out_shape=jax.ShapeDtypeStruct((b, l_all, ATT_W), BF16),
        scratch_shapes=[pltpu.VMEM((l_all // tk, tq, tk), F32), pltpu.VMEM((tq, HEAD_DIM), F32), pltpu.VMEM((tq, 1), F32),
                        pltpu.VMEM((tq, 2 * HEAD_DIM), F32)],
        compiler_params=_params(("arbitrary", "arbitrary", "arbitrary")),
        name="attention",
    )(qn, kn, v1)
    ctx_blk = l_lat // l_ctx
    return pl.pallas_call(
        functools.partial(_att_kernel, tk=l_ctx),
        grid=(b, ATT_HEADS),
        in_specs=[pl.BlockSpec((1, l_ctx, HEAD_DIM), lambda bb, h: (bb, ctx_blk, h)),
                  pl.BlockSpec((1, HEAD_DIM, l_ctx), lambda bb, h: (bb, h // group, ctx_blk)),
                  pl.BlockSpec((1, l_ctx, 2 * HEAD_DIM), lambda bb, h: (bb, ctx_blk, h // group)),
                  pl.BlockSpec(memory_space=pl.ANY)],
        out_specs=pl.BlockSpec((1, l_ctx, HEAD_DIM), lambda bb, h: (bb, ctx_blk, h)),
        out_shape=jax.ShapeDtypeStruct((b, l_all, ATT_W), BF16),
        scratch_shapes=[pltpu.VMEM((1, l_ctx, l_ctx), F32), pltpu.VMEM((l_ctx, HEAD_DIM), F32), pltpu.VMEM((l_ctx, 1), F32),
                        pltpu.VMEM((l_ctx, 2 * HEAD_DIM), F32)],
        input_output_aliases={3: 0},
        compiler_params=_params(("arbitrary", "arbitrary")),
        name="attention_ctx",
    )(qn, kn, v1, y_lat)


def _dnprep_kernel(*refs, tm, l_lat, l_all):
    mains, prevs, nexts = refs[0:3], refs[3:6], refs[6:9]
    cw_ref, pab_ref, alog_ref, dtb_ref = refs[9:13]
    outs = refs[13:16]
    g_ref, ext_ref = refs[16], refs[17]
    i = pl.program_id(1)
    halo = 16
    pad = DN_CONV_K // 2
    first = jnp.logical_or(i == 0, i == l_lat // tm)
    last = jnp.logical_or(i == l_lat // tm - 1, i == l_all // tm - 1)

    keep_prev = jnp.where(first, 0.0, 1.0)
    keep_next = jnp.where(last, 0.0, 1.0)

    for part in range(3):
        ext_ref[0:halo, :] = prevs[part][0].astype(F32) * keep_prev
        ext_ref[halo:halo + tm, :] = mains[part][0].astype(F32)
        ext_ref[halo + tm:, :] = nexts[part][0].astype(F32) * keep_next
        acc = jnp.zeros((tm, DN_W), F32)
        for j in range(DN_CONV_K):
            w_j = cw_ref[j:j + 1, part * DN_W:(part + 1) * DN_W]
            acc = acc + w_j * ext_ref[pl.ds(halo - pad + j, tm), :]
        y = _silu(acc)
        for h in range(DN_HEADS):
            sl = slice(h * HEAD_DIM, (h + 1) * HEAD_DIM)
            yh = y[:, sl]
            if part < 2:
                yh = yh * lax.rsqrt(jnp.sum(yh * yh, -1, keepdims=True) + EPS)
            if part == 0:
                yh = yh * QK_SCALE
            outs[part][0, :, sl] = yh.astype(BF16)

    a = pab_ref[0]
    col = lax.broadcasted_iota(jnp.int32, a.shape, 1)
    z = a + dtb_ref[...]
    softplus = jnp.maximum(z, 0.0) + jnp.log(1.0 + jnp.exp(-jnp.abs(z)))
    g = -jnp.exp(alog_ref[...]) * softplus
    ii = lax.broadcasted_iota(jnp.int32, (tm, tm), 0)
    rr = lax.broadcasted_iota(jnp.int32, (tm, tm), 1)
    same_chunk = (ii // DN_CHUNK) == (rr // DN_CHUNK)
    prefix = jnp.logical_and(same_chunk, rr <= ii).astype(F32)
    suffix = jnp.logical_and(same_chunk, rr >= ii).astype(F32)
    hi = lax.Precision.HIGHEST
    cum_f = jnp.dot(prefix, g, preferred_element_type=F32, precision=hi)
    cum_b = jnp.dot(suffix, g, preferred_element_type=F32, precision=hi)
    g_ref[0] = jnp.where(col < DN_HEADS, cum_f, jnp.where(col < N_DN_GATES, cum_b, jax.nn.sigmoid(a)))


def _dnprep_call(p_main, p_ab, conv_w, alog_row, dtb_row, *, l_lat, tm):
    b, l_all, _ = p_main.shape
    halo = 16
    r = tm // halo
    cb = COL_DN_QKV // DN_W
    in_specs = []
    for part in range(3):
        in_specs.append(pl.BlockSpec((1, tm, DN_W), lambda bb, i, p=part: (bb, i, cb + p)))
    for part in range(3):
        in_specs.append(pl.BlockSpec((1, halo, DN_W), lambda bb, i, p=part: (bb, jnp.maximum(i * r - 1, 0), cb + p)))
    for part in range(3):
        in_specs.append(pl.BlockSpec(
            (1, halo, DN_W), lambda bb, i, p=part: (bb, jnp.minimum((i + 1) * r, l_all // halo - 1), cb + p)))
    in_specs += [pl.BlockSpec((DN_CONV_K, 3 * DN_W), lambda bb, i: (0, 0)),
                 pl.BlockSpec((1, tm, GATE_COLS), lambda bb, i: (bb, i, 0)),
                 pl.BlockSpec((1, GATE_COLS), lambda bb, i: (0, 0)),
                 pl.BlockSpec((1, GATE_COLS), lambda bb, i: (0, 0))]
    kern = functools.partial(_dnprep_kernel, tm=tm, l_lat=l_lat, l_all=l_all)
    return pl.pallas_call(
        kern,
        grid=(b, l_all // tm),
        in_specs=in_specs,
        out_specs=[pl.BlockSpec((1, tm, DN_W), lambda bb, i: (bb, i, 0))] * 3
        + [pl.BlockSpec((1, tm, GATE_COLS), lambda bb, i: (bb, i, 0))],
        out_shape=[jax.ShapeDtypeStruct((b, l_all, DN_W), BF16)] * 3
        + [jax.ShapeDtypeStruct((b, l_all, GATE_COLS), F32)],
        scratch_shapes=[pltpu.VMEM((tm + 2 * halo, DN_W), F32)],
        compiler_params=_params(("arbitrary", "arbitrary")),
        name="deltanet_conv_gates",
    )(*([p_main] * 9), conv_w, p_ab, alog_row, dtb_row)


def _neumann_inverse_many(a_list, eye):
    ps = [-a for a in a_list]
    ts = [eye + p for p in ps]
    for _ in range(5):
        pbs = [p.astype(BF16) for p in ps]
        ps = [_dot(pb, pb) for pb in pbs]
        ts = [t + _dot(t.astype(BF16), p.astype(BF16)) for t, p in zip(ts, ps)]
    return ts


def _dnchunk_kernel(q_ref, k_ref, v_ref, g_ref, gt_ref, w_ref, kc_ref, qg_ref, kgt_ref, qk_ref, gl_ref):
    c = DN_CHUNK
    n_sub = q_ref.shape[1] // c
    ii = lax.broadcasted_iota(jnp.int32, (c, c), 0)
    jj = lax.broadcasted_iota(jnp.int32, (c, c), 1)
    eye = (ii == jj).astype(F32)

    qk_ref[...] = jnp.zeros_like(qk_ref)
    kgt_ref[...] = jnp.zeros_like(kgt_ref)
    gl_ref[...] = jnp.zeros_like(gl_ref)

    heads = range(DN_HEADS)
    subs = range(n_sub)
    chains = [(u, d, h) for u in subs for h in heads for d in range(2)]
    rows = [slice(u * c, (u + 1) * c) for u in subs]
    sls = [slice(h * HEAD_DIM, (h + 1) * HEAD_DIM) for h in heads]
    gcol = [g_ref[0, rows[u], :] for u in subs]
    grow = [gt_ref[0, u] for u in subs]
    q16 = {(u, h): q_ref[0, rows[u], sls[h]] for u in subs for h in heads}
    k16 = {(u, h): k_ref[0, rows[u], sls[h]] for u in subs for h in heads}
    kk = {uh: _dot_nt(k16[uh], k16[uh]) for uh in k16}
    qk0 = {uh: _dot_nt(q16[uh], k16[uh]) for uh in k16}

    gcc, tot, beta, decay, a_list = {}, {}, {}, {}, []
    for u, d, h in chains:
        col = d * DN_HEADS + h
        key = (u, d, h)
        gcc[key] = gcol[u][:, col:col + 1]
        gcr = grow[u][col:col + 1, :]
        tot[key] = gcol[u][c - 1:c, col:col + 1] if d == 0 else gcol[u][0:1, col:col + 1]
        beta[key] = gcol[u][:, N_DN_GATES + col:N_DN_GATES + col + 1]
        incl = (ii >= jj) if d == 0 else (ii <= jj)
        strict = (ii > jj) if d == 0 else (ii < jj)
        decay[key] = jnp.where(incl, jnp.exp(jnp.where(incl, gcc[key] - gcr, 0.0)), 0.0)
        a_list.append(jnp.where(strict, kk[u, h] * beta[key] * decay[key], 0.0))
        gl_ref[0, d, u, h:h + 1, :] = jnp.broadcast_to(jnp.exp(tot[key]), (1, HEAD_DIM))

    t16 = [t.astype(BF16) for t in _neumann_inverse_many(a_list, eye)]

    for key, t in zip(chains, t16):
        u, d, h = key
        sl = sls[h]
        k = k16[u, h].astype(F32)
        e_col = jnp.exp(gcc[key])
        rhs = jnp.concatenate([(v_ref[0, rows[u], sl].astype(F32) * beta[key]).astype(BF16),
                               (k * (beta[key] * e_col)).astype(BF16)], axis=1)
        wk = _dot(t, rhs)
        w_ref[0, d, rows[u], sl] = wk[:, :HEAD_DIM]
        kc_ref[0, d, rows[u], sl] = wk[:, HEAD_DIM:].astype(BF16)
        qg_ref[0, d, rows[u], sl] = (q16[u, h].astype(F32) * e_col).astype(BF16)
        kg = k * jnp.exp(tot[key] - gcc[key])
        kgt_ref[0, d, u, :, pl.ds(h * HEAD_DIM, c)] = kg.T.astype(BF16)
        qk_ref[0, d, rows[u], pl.ds(h * HEAD_DIM, c)] = (qk0[u, h] * decay[key]).astype(BF16)


DN_STEP_CHUNKS = 2


def _dnchunk_call(qd, kd, vd, g, gt):
    b, l_all, _ = qd.shape
    c = DN_CHUNK
    n = l_all // c
    ns = DN_STEP_CHUNKS
    tok = lambda dt: jax.ShapeDtypeStruct((b, 2, l_all, DN_W), dt)
    tok_spec = pl.BlockSpec((1, 2, ns * c, DN_W), lambda bb, s: (bb, 0, s, 0))
    return pl.pallas_call(
        _dnchunk_kernel,
        grid=(b, n // ns),
        in_specs=[pl.BlockSpec((1, ns * c, DN_W), lambda bb, s: (bb, s, 0))] * 3
        + [pl.BlockSpec((1, ns * c, GATE_COLS), lambda bb, s: (bb, s, 0)),
           pl.BlockSpec((1, ns, 2 * N_DN_GATES, c), lambda bb, s: (bb, s, 0, 0))],
        out_specs=[tok_spec, tok_spec, tok_spec,
                   pl.BlockSpec((1, 2, ns, HEAD_DIM, DN_W), lambda bb, s: (bb, 0, s, 0, 0)),
                   tok_spec,
                   pl.BlockSpec((1, 2, ns, 8, HEAD_DIM), lambda bb, s: (bb, 0, s, 0, 0))],
        out_shape=[tok(F32), tok(BF16), tok(BF16),
                   jax.ShapeDtypeStruct((b, 2, n, HEAD_DIM, DN_W), BF16),
                   tok(BF16),
                   jax.ShapeDtypeStruct((b, 2, n, 8, HEAD_DIM), F32)],
        compiler_params=_params(("arbitrary", "arbitrary")),
        name="deltanet_chunk_factors",
    )(qd, kd, vd, g, gt)


def _dnscan_kernel(*refs):
    ins_f, ins_b = refs[0:6], refs[6:12]
    of_ref, ob_ref, s_ref = refs[12], refs[13], refs[14]
    c = DN_CHUNK
    n_sub = of_ref.shape[1] // c
    step = pl.program_id(1)

    @pl.when(step == 0)
    def _():
        s_ref[...] = jnp.zeros_like(s_ref)

    ins, outs = (ins_f, ins_b), (of_ref, ob_ref)
    chains = [(d, h) for d in range(2) for h in range(DN_HEADS)]
    sls = [slice(h * HEAD_DIM, (h + 1) * HEAD_DIM) for h in range(DN_HEADS)]
    state = [s_ref[d * DN_HEADS + h] for d, h in chains]
    for j in range(n_sub):
        sub = [j, n_sub - 1 - j]
        rows = [slice(u * c, (u + 1) * c) for u in sub]
        s16 = [s.astype(BF16) for s in state]
        v16 = [(ins[d][0][0, 0, rows[d], sls[h]] - _dot(ins[d][1][0, 0, rows[d], sls[h]], s)).astype(BF16)
               for (d, h), s in zip(chains, s16)]
        o_inter = [_dot(ins[d][2][0, 0, rows[d], sls[h]], s) for (d, h), s in zip(chains, s16)]
        new_state = []
        for (d, h), s, v, oi in zip(chains, state, v16, o_inter):
            qk = ins[d][4][0, 0, rows[d], pl.ds(h * HEAD_DIM, c)]
            outs[d][0, rows[d], sls[h]] = (oi + _dot(qk, v)).astype(outs[d].dtype)
            kgt = ins[d][3][0, 0, sub[d], :, pl.ds(h * HEAD_DIM, c)]
            new_state.append(s * ins[d][5][0, 0, sub[d], h:h + 1, :] + _dot(kgt, v))
        state = new_state
    for i, s in enumerate(state):
        s_ref[i] = s


def _dnscan_call(w, kc, qg, kgt, qk, gl, *, l_lat):
    b, _, l_all, _ = w.shape
    ns = DN_STEP_CHUNKS
    blk = ns * DN_CHUNK
    n_all, n_lat = l_all // blk, l_lat // blk
    n_ctx = n_all - n_lat

    def fwd(s):
        return jnp.where(s < n_ctx, n_lat + s, s - n_ctx)

    def bwd(s):
        return n_all - 1 - s

    in_specs = []
    for d, order in enumerate((fwd, bwd)):
        tok_spec = pl.BlockSpec((1, 1, blk, DN_W), lambda bb, s, d=d, o=order: (bb, d, o(s), 0))
        in_specs += [tok_spec, tok_spec, tok_spec,
                     pl.BlockSpec((1, 1, ns, HEAD_DIM, DN_W), lambda bb, s, d=d, o=order: (bb, d, o(s), 0, 0)),
                     tok_spec,
                     pl.BlockSpec((1, 1, ns, 8, HEAD_DIM), lambda bb, s, d=d, o=order: (bb, d, o(s), 0, 0))]
    return pl.pallas_call(
        _dnscan_kernel,
        grid=(b, n_all),
        in_specs=in_specs,
        out_specs=[pl.BlockSpec((1, blk, DN_W), lambda bb, s: (bb, fwd(s), 0)),
                   pl.BlockSpec((1, blk, DN_W), lambda bb, s: (bb, bwd(s), 0))],
        out_shape=[jax.ShapeDtypeStruct((b, l_all, DN_W), BF16)] * 2,
        scratch_shapes=[pltpu.VMEM((2 * DN_HEADS, HEAD_DIM, HEAD_DIM), F32)],
        compiler_params=_params(("arbitrary", "arbitrary")),
        name="deltanet_scan",
    )(w, kc, qg, kgt, qk, gl, w, kc, qg, kgt, qk, gl)


def _rope_tables(l_lat, l_ctx):
    rows = l_lat // GRID_W
    row = jnp.repeat(jnp.arange(rows, dtype=F32), GRID_W)
    col = jnp.tile(jnp.arange(GRID_W, dtype=F32), rows)
    n_freq = HEAD_DIM // 4
    inv = ROPE_THETA ** (-jnp.arange(n_freq, dtype=F32) / n_freq)
    ang = jnp.concatenate([row[:, None] * inv, col[:, None] * inv], -1)
    cos, sin = jnp.cos(ang), jnp.sin(ang)
    cos2 = jnp.concatenate([cos, cos], -1)
    sin2 = jnp.concatenate([-sin, sin], -1)
    cos2 = jnp.concatenate([cos2, jnp.ones((l_ctx, HEAD_DIM), F32)], 0)
    sin2 = jnp.concatenate([sin2, jnp.zeros((l_ctx, HEAD_DIM), F32)], 0)
    return cos2, sin2


def _pad_row(v, width):
    v = v.reshape(1, -1).astype(F32)
    return jnp.pad(v, ((0, 0), (0, width - v.shape[1])))


def _row_tile(l_lat, l_ctx):
    l_all = l_lat + l_ctx
    for tm in (768, 512, 256, 128):
        if l_all % tm == 0:
            return tm
    raise ValueError("token count must be a multiple of 128")


def kernel(x, c, ctx, c_ctx, w_ada, b_ada, w_in, ret_decay_logit, dn_conv_w, dn_a_log, dn_dt_bias, dn_norm_w,
           att_qn_w, att_kn_w, w_o, ln1_w, ln1_b, w_ffn_in, w_ffn_out, ln2_w, ln2_b):
    bsz, l_lat, d = x.shape
    l_ctx = ctx.shape[1]
    l_all = l_lat + l_ctx
    depth = w_ada.shape[0]
    assert l_lat % 256 == 0 and l_ctx % 256 == 0 and l_lat % GRID_W == 0
    tm = _row_tile(l_lat, l_ctx)
    tn = MAIN_W // 2

    cos2, sin2 = _rope_tables(l_lat, l_ctx)
    x_all = jnp.concatenate([x, ctx], axis=1)

    cond_rows = 8 * pl.cdiv(bsz + 1, 8)
    cond = jnp.concatenate([c, c_ctx[None, :]], 0)
    cond = jnp.pad(cond, ((0, cond_rows - bsz - 1), (0, 0)))
    mod = _ada_call(cond, w_ada, b_ada)

    gate_lo = COL_DN_Z + DN_W
    for i in range(depth):
        last = i == depth - 1
        ml = mod[i, :bsz].reshape(bsz, 6, d)
        mc = mod[i, bsz].reshape(1, 6, d)
        w_main = jnp.concatenate([w_in[i, :, :gate_lo], w_in[i, :, gate_lo + 2 * N_DN_GATES:]], 1).astype(BF16)
        w_ab = jnp.pad(w_in[i, :, gate_lo:gate_lo + 2 * N_DN_GATES],
                       ((0, 0), (0, GATE_COLS - 2 * N_DN_GATES))).astype(BF16)

        p_main, p_ab = _inproj_call(x_all, ml, mc, w_main, w_ab, l_lat=l_lat, tm=tm, tn=tn)

        log_gamma = jax.nn.log_sigmoid(ret_decay_logit[i].astype(F32))
        ret_f, ret_b = _ret_call(log_gamma, p_main, cos2, sin2, l_lat=l_lat)

        qn, kn, v1 = _attprep_call(p_main, cos2, sin2, att_qn_w[i].reshape(1, -1), att_kn_w[i].reshape(1, -1),
                                   tm=tm)
        y_att = _att_call(qn, kn, v1, l_lat=l_lat, tq=1024 if l_lat % 1024 == 0 else 256)

        qd, kd, vd, g = _dnprep_call(p_main, p_ab, dn_conv_w[i], _pad_row(dn_a_log[i], GATE_COLS),
                                     _pad_row(dn_dt_bias[i], GATE_COLS), l_lat=l_lat, tm=256)
        gt = g[:, :, :2 * N_DN_GATES].reshape(bsz, l_all // DN_CHUNK, DN_CHUNK, 2 * N_DN_GATES)
        gt = jnp.swapaxes(gt, 2, 3)
        dn_f, dn_b = _dnscan_call(*_dnchunk_call(qd, kd, vd, g, gt), l_lat=l_lat)

        x_all = _wo_call(ret_f, ret_b, dn_f, dn_b, p_main, dn_norm_w[i].reshape(1, -1), y_att, x_all, ml, mc,
                         w_o[i].astype(BF16), ln1_w[i].reshape(1, -1), ln1_b[i].reshape(1, -1), l_lat=l_lat, tm=tm)
        x_all = _ffn_call(x_all, ml, mc, w_ffn_in[i].astype(BF16), w_ffn_out[i].astype(BF16),
                          ln2_w[i].reshape(1, -1), ln2_b[i].reshape(1, -1), l_lat=l_lat,
                          out_rows=l_lat if last else l_all, tm=tm, tf=512)
    return x_all
```

```python
import functools

import jax
import jax.numpy as jnp
from jax import lax
from jax.experimental import pallas as pl
from jax.experimental.pallas import tpu as pltpu

F32 = jnp.float32
BF16 = jnp.bfloat16

HEAD_DIM = 128
RET_HEADS = 4
DN_HEADS = 4
ATT_HEADS = 8
ATT_KV_HEADS = 2
RET_W = RET_HEADS * HEAD_DIM
DN_W = DN_HEADS * HEAD_DIM
ATT_W = ATT_HEADS * HEAD_DIM
ATT_KV_W = ATT_KV_HEADS * HEAD_DIM
RET_CHUNK = 128
DN_CHUNK = 64
DN_CONV_K = 5
GRID_W = 64
ROPE_THETA = 10000.0
MODEL_DEPTH = 4
DEEPNORM_ALPHA = (2 * MODEL_DEPTH) ** 0.25
EPS = 1e-6
QK_SCALE = HEAD_DIM ** -0.5
LOG2_E = 1.4426950408889634

COL_RET_Q = 0
COL_RET_K = RET_W
COL_RET_V = 2 * RET_W
COL_RET_G = 3 * RET_W
COL_DN_QKV = 4 * RET_W
COL_DN_Z = COL_DN_QKV + 3 * DN_W
COL_ATT_Q = COL_DN_Z + DN_W
COL_ATT_K = COL_ATT_Q + ATT_W
COL_ATT_V = COL_ATT_K + ATT_KV_W
MAIN_W = COL_ATT_V + ATT_KV_W
GATE_COLS = 128
N_DN_GATES = 2 * DN_HEADS

VMEM_LIMIT_MB = 56


def _params(sem, vmem_mb=VMEM_LIMIT_MB):
    return pltpu.CompilerParams(dimension_semantics=sem, vmem_limit_bytes=vmem_mb * 1024 * 1024)


def _dot(a, b):
    return jnp.dot(a, b, preferred_element_type=F32)


def _dot_nt(a, b):
    return lax.dot_general(a, b, (((1,), (1,)), ((), ())), preferred_element_type=F32)


def _silu(x):
    return x * jax.nn.sigmoid(x)


def _is_ctx_rows(i, tm, l_lat):
    rows = i * tm + lax.broadcasted_iota(jnp.int32, (tm, 1), 0)
    return rows >= l_lat


def _mod_row(is_ctx, ml_ref, mc_ref, k):
    return jnp.where(is_ctx, mc_ref[0, k:k + 1, :], ml_ref[0, k:k + 1, :])


def _layer_norm(r, w, b):
    mu = jnp.mean(r, -1, keepdims=True)
    rc = r - mu
    var = jnp.mean(rc * rc, -1, keepdims=True)
    return rc * lax.rsqrt(var + EPS) * w + b


def _split_bf16(a):
    hi = a.astype(BF16)
    return hi, (a - hi.astype(F32)).astype(BF16)


def _ada_kernel(c_ref, w_ref, b_ref, o_ref):
    h_hi, h_lo = _split_bf16(_silu(c_ref[...]))
    w_hi, w_lo = _split_bf16(w_ref[0])
    o_ref[0] = _dot(h_hi, w_hi) + _dot(h_hi, w_lo) + _dot(h_lo, w_hi) + b_ref[0]


def _ada_call(cond, w_ada, b_ada):
    depth, d, n6 = w_ada.shape
    rows = cond.shape[0]
    tn = 1024
    return pl.pallas_call(
        _ada_kernel,
        grid=(depth, n6 // tn),
        in_specs=[pl.BlockSpec((rows, d), lambda l, j: (0, 0)),
                  pl.BlockSpec((1, d, tn), lambda l, j: (l, 0, j)),
                  pl.BlockSpec((1, 1, tn), lambda l, j: (l, 0, j))],
        out_specs=pl.BlockSpec((1, rows, tn), lambda l, j: (l, 0, j)),
        out_shape=jax.ShapeDtypeStruct((depth, rows, n6), F32),
        compiler_params=_params(("arbitrary", "arbitrary")),
        name="ada_mod",
    )(cond, w_ada, b_ada.reshape(depth, 1, n6))


def _inproj_kernel(*refs, tm, l_lat, split_input):
    if split_input:
        x_ref, c_ref, ml_ref, mc_ref, w_ref, wab_ref, pm_ref, pab_ref, xo_ref, h_ref = refs
    else:
        x_ref, ml_ref, mc_ref, w_ref, wab_ref, pm_ref, pab_ref, h_ref = refs
    i = pl.program_id(1)
    j = pl.program_id(2)

    @pl.when(j == 0)
    def _():
        if split_input:
            n_full, rem = divmod(l_lat, tm)

            @pl.when(i < n_full)
            def _():
                xo_ref[0] = x_ref[0]

            @pl.when(i >= n_full)
            def _():
                if rem:
                    xo_ref[0, :rem, :] = x_ref[0, :rem, :]
                xo_ref[0, rem:, :] = c_ref[0]

            x_tile = xo_ref[0]
        else:
            x_tile = x_ref[0]
        is_ctx = _is_ctx_rows(i, tm, l_lat)
        shift = _mod_row(is_ctx, ml_ref, mc_ref, 0)
        scale = _mod_row(is_ctx, ml_ref, mc_ref, 1)
        h = (x_tile * (1.0 + scale) + shift).astype(BF16)
        h_ref[...] = h
        pab_ref[0] = _dot(h, wab_ref[...])

    tn = w_ref.shape[1]
    step = 512
    for lo in range(0, tn, step):
        hi = min(lo + step, tn)
        pm_ref[0, :, lo:hi] = _dot(h_ref[...], w_ref[:, lo:hi]).astype(BF16)


def _inproj_call(xs, ml, mc, w_main, w_ab, *, l_lat, tm, tn):
    split_input = isinstance(xs, tuple)
    xs = list(xs) if split_input else [xs]
    b, _, d = xs[0].shape
    l_all = l_lat + xs[1].shape[1] if split_input else xs[0].shape[1]
    x_spec = pl.BlockSpec((1, tm, d), lambda bb, i, j: (bb, i, 0))
    in_specs = [x_spec]
    out_specs = [pl.BlockSpec((1, tm, tn), lambda bb, i, j: (bb, i, j)),
                 pl.BlockSpec((1, tm, GATE_COLS), lambda bb, i, j: (bb, i, 0))]
    out_shape = [jax.ShapeDtypeStruct((b, l_all, MAIN_W), BF16), jax.ShapeDtypeStruct((b, l_all, GATE_COLS), F32)]
    if split_input:
        l_ctx = l_all - l_lat
        assert l_lat % tm + l_ctx == tm, "the context must exactly fill the last row tile"
        in_specs.append(pl.BlockSpec((1, l_ctx, d), lambda bb, i, j: (bb, 0, 0)))
        out_specs.append(x_spec)
        out_shape.append(jax.ShapeDtypeStruct((b, l_all, d), F32))
    in_specs +=[pl.BlockSpec((1, 6, d), lambda bb, i, j: (bb, 0, 0)),
                 pl.BlockSpec((1, 6, d), lambda bb, i, j: (0, 0, 0)),
                 pl.BlockSpec((d, tn), lambda bb, i, j: (0, j)),
                 pl.BlockSpec((d, GATE_COLS), lambda bb, i, j: (0, 0))]
    return pl.pallas_call(
        functools.partial(_inproj_kernel, tm=tm, l_lat=l_lat, split_input=split_input),
        grid=(b, l_all // tm, MAIN_W // tn),
        in_specs=in_specs,
        out_specs=out_specs,
        out_shape=out_shape,
        scratch_shapes=[pltpu.VMEM((tm, d), BF16)],
        compiler_params=_params(("arbitrary", "arbitrary", "arbitrary")),
        name="in_proj",
    )(*xs, ml, mc, w_main, w_ab)


WO_SUB_ROWS = 256


def _wo_kernel(rf_ref, rb_ref, rg_ref, df_ref, db_ref, dz_ref, nw_ref, a_ref, x_ref, ml_ref, mc_ref, w_ref,
               lnw_ref, lnb_ref, o_ref, *, tm, l_lat):
    i = pl.program_id(1)
    sub = WO_SUB_ROWS
    n_rd = RET_W + DN_W

    def head_norm(o):
        return o * lax.rsqrt(jnp.mean(o * o, -1, keepdims=True) + EPS)

    for lo in range(0, tm, sub):
        rows = slice(lo, lo + sub)
        ys = []
        for h in range(RET_HEADS):
            sl = slice(h * HEAD_DIM, (h + 1) * HEAD_DIM)
            o = rf_ref[0, rows, sl].astype(F32) + rb_ref[0, rows, sl].astype(F32)
            ys.append((head_norm(o) * _silu(rg_ref[0, rows, sl].astype(F32))).astype(BF16))
        for h in range(DN_HEADS):
            sl = slice(h * HEAD_DIM, (h + 1) * HEAD_DIM)
            o = df_ref[0, rows, sl].astype(F32) + db_ref[0, rows, sl].astype(F32)
            ys.append((head_norm(o) * nw_ref[...] * _silu(dz_ref[0, rows, sl].astype(F32))).astype(BF16))
        y_rd = jnp.concatenate(ys, axis=1)
        acc = _dot(y_rd, w_ref[0:n_rd, :]) + _dot(a_ref[0, rows, :], w_ref[n_rd:, :])
        rid = i * tm + lo + lax.broadcasted_iota(jnp.int32, (sub, 1), 0)
        gate = _mod_row(rid >= l_lat, ml_ref, mc_ref, 2)
        r = DEEPNORM_ALPHA * x_ref[0, rows, :] + gate * acc
        o_ref[0, rows, :] = _layer_norm(r, lnw_ref[...], lnb_ref[...])


def _wo_call(ret_f, ret_b, dn_f, dn_b, p_main, dn_norm_w, y_att, x_all, ml, mc, w_o, ln_w, ln_b, *, l_lat, tm):
    b, l_all, d = x_all.shape
    kern = functools.partial(_wo_kernel, tm=tm, l_lat=l_lat)
    o_spec = pl.BlockSpec((1, tm, RET_W), lambda bb, i: (bb, i, 0))
    return pl.pallas_call(
        kern,
        grid=(b, l_all // tm),
        in_specs=[o_spec, o_spec,
                  pl.BlockSpec((1, tm, RET_W), lambda bb, i: (bb, i, COL_RET_G // RET_W)),
                  o_spec, o_spec,
                  pl.BlockSpec((1, tm, DN_W), lambda bb, i: (bb, i, COL_DN_Z // DN_W)),
                  pl.BlockSpec((1, HEAD_DIM), lambda bb, i: (0, 0)),
                  pl.BlockSpec((1, tm, ATT_W), lambda bb, i: (bb, i, 0)),
                  pl.BlockSpec((1, tm, d), lambda bb, i: (bb, i, 0)),
                  pl.BlockSpec((1, 6, d), lambda bb, i: (bb, 0, 0)),
                  pl.BlockSpec((1, 6, d), lambda bb, i: (0, 0, 0)),
                  pl.BlockSpec((RET_W + DN_W + ATT_W, d), lambda bb, i: (0, 0), pipeline_mode=pl.Buffered(1)),
                  pl.BlockSpec((1, d), lambda bb, i: (0, 0)),
                  pl.BlockSpec((1, d), lambda bb, i: (0, 0))],
        out_specs=pl.BlockSpec((1, tm, d), lambda bb, i: (bb, i, 0)),
        out_shape=jax.ShapeDtypeStruct((b, l_all, d), F32),
        compiler_params=_params(("arbitrary", "arbitrary")),
        name="w_o_postnorm",
    )(ret_f, ret_b, p_main, dn_f, dn_b, p_main, dn_norm_w, y_att, x_all, ml, mc, w_o, ln_w, ln_b)


def _ffn_kernel(x_ref, ml_ref, mc_ref, wg_ref, wu_ref, wo_ref, lnw_ref, lnb_ref, o_ref, h_ref, acc_ref,
                *, tm, l_lat):
    i = pl.program_id(1)
    f = pl.program_id(2)

    @pl.when(f == 0)
    def _():
        is_ctx = _is_ctx_rows(i, tm, l_lat)
        shift = _mod_row(is_ctx, ml_ref, mc_ref, 3)
        scale = _mod_row(is_ctx, ml_ref, mc_ref, 4)
        h_ref[...] = (x_ref[0] * (1.0 + scale) + shift).astype(BF16)
        acc_ref[...] = jnp.zeros_like(acc_ref)

    h = h_ref[...]
    g = _dot(h, wg_ref[...])
    u = _dot(h, wu_ref[...])
    acc_ref[...] += _dot((_silu(g) * u).astype(BF16), wo_ref[...])

    @pl.when(f == pl.num_programs(2) - 1)
    def _():
        is_ctx = _is_ctx_rows(i, tm, l_lat)
        gate = _mod_row(is_ctx, ml_ref, mc_ref, 5)
        r = DEEPNORM_ALPHA * x_ref[0] + gate * acc_ref[...]
        o_ref[0] = _layer_norm(r, lnw_ref[...], lnb_ref[...])


def _ffn_call(x_all, ml, mc, w_in, w_out, ln_w, ln_b, *, l_lat, out_rows, tm, tf):
    b, _, d = x_all.shape
    d_ff = w_out.shape[0]
    nf = d_ff // tf
    kern = functools.partial(_ffn_kernel, tm=tm, l_lat=l_lat)
    return pl.pallas_call(
        kern,
        grid=(b, pl.cdiv(out_rows, tm), nf),
        in_specs=[pl.BlockSpec((1, tm, d), lambda bb, i, f: (bb, i, 0)),
                  pl.BlockSpec((1, 6, d), lambda bb, i, f: (bb, 0, 0)),
                  pl.BlockSpec((1, 6, d), lambda bb, i, f: (0, 0, 0)),
                  pl.BlockSpec((d, tf), lambda bb, i, f: (0, f)),
                  pl.BlockSpec((d, tf), lambda bb, i, f: (0, f + nf)),
                  pl.BlockSpec((tf, d), lambda bb, i, f: (f, 0)),
                  pl.BlockSpec((1, d), lambda bb, i, f: (0, 0)),
                  pl.BlockSpec((1, d), lambda bb, i, f: (0, 0))],
        out_specs=pl.BlockSpec((1, tm, d), lambda bb, i, f: (bb, i, 0)),
        out_shape=jax.ShapeDtypeStruct((b, out_rows, d), F32),
        scratch_shapes=[pltpu.VMEM((tm, d), BF16), pltpu.VMEM((tm, d), F32)],
        compiler_params=_params(("arbitrary", "arbitrary", "arbitrary")),
        name="ffn_postnorm",
    )(x_all, ml, mc, w_in, w_in, w_out, ln_w, ln_b)


def _rope(x, cos2, sin2):
    return x * cos2 + pltpu.roll(x, HEAD_DIM // 2, 1) * sin2


def _ret_kernel(lg_ref, qf_ref, kf_ref, vf_ref, cf_ref, sf_ref, qb_ref, kb_ref, vb_ref, cb_ref, sb_ref,
                of_ref, ob_ref, s_ref, dm_ref, qd_ref, kd_ref):
    step = pl.program_id(1)
    c = RET_CHUNK

    @pl.when(step == 0)
    def _():
        s_ref[...] = jnp.zeros_like(s_ref)
        ii = lax.broadcasted_iota(jnp.int32, (c, c), 0).astype(F32)
        jj = lax.broadcasted_iota(jnp.int32, (c, c), 1).astype(F32)
        for d in range(2):
            for h in range(RET_HEADS):
                lg = lg_ref[d, h]
                if d == 0:
                    rel, qe, ke = ii - jj, ii + 1.0, (c - 1.0) - ii
                else:
                    rel, qe, ke = jj - ii, c - ii, ii
                idx = d * RET_HEADS + h
                dm_ref[idx] = jnp.where(rel >= 0, jnp.exp(jnp.maximum(rel, 0.0) * lg), 0.0)
                qd_ref[idx] = jnp.exp(qe * lg)
                kd_ref[idx] = jnp.exp(ke * lg)

    dirs = ((qf_ref, kf_ref, vf_ref, cf_ref, sf_ref, of_ref), (qb_ref, kb_ref, vb_ref, cb_ref, sb_ref, ob_ref))
    n_sub = of_ref.shape[1] // c
    chains = [(d, h) for d in range(2) for h in range(RET_HEADS)]
    sls = [slice(h * HEAD_DIM, (h + 1) * HEAD_DIM) for h in range(RET_HEADS)]
    order = [(j, n_sub - 1 - j) for j in range(n_sub)]
    rows = [[slice(u * c, (u + 1) * c) for u in sub] for sub in order]
    qs, ks, vs, qk, kv = {}, {}, {}, {}, {}
    for j in range(n_sub):
        for i, (d, h) in enumerate(chains):
            r = rows[j][d]
            tab = (dirs[d][3][r, :], dirs[d][4][r, :])
            qs[j, i] = _rope(dirs[d][0][0, r, sls[h]].astype(F32), *tab)
            ks[j, i] = _rope(dirs[d][1][0, r, sls[h]].astype(F32), *tab) * QK_SCALE
            vs[j, i] = dirs[d][2][0, r, sls[h]]
    for key in qs:
        qk[key] = (_dot_nt(qs[key].astype(BF16), ks[key].astype(BF16)) * dm_ref[key[1]]).astype(BF16)
        kv[key] = _dot((ks[key] * kd_ref[key[1]]).T.astype(BF16), vs[key])
    state = [s_ref[i] for i in range(len(chains))]
    for j in range(n_sub):
        o_inter = [_dot((qs[j, i] * qd_ref[i]).astype(BF16), s.astype(BF16)) for i, s in enumerate(state)]
        for i, (d, h) in enumerate(chains):
            dirs[d][5][0, rows[j][d], sls[h]] = (_dot(qk[j, i], vs[j, i]) + o_inter[i]).astype(dirs[d][5].dtype)
            chunk_decay = jnp.exp(jnp.full((1, HEAD_DIM), float(c), F32) * lg_ref[d, h])
            state[i] = state[i] * chunk_decay + kv[j, i]
    for i, st in enumerate(state):
        s_ref[i] = st


RET_STEP_CHUNKS = 2


def _ret_call(log_gamma, p_main, cos2, sin2, *, l_lat):
    b, l_all, _ = p_main.shape
    c = RET_CHUNK
    blk = RET_STEP_CHUNKS * c
    n_all, n_lat = l_all // blk, l_lat // blk
    n_ctx = n_all - n_lat

    def fwd(s):
        return jnp.where(s < n_ctx, n_lat + s, s - n_ctx)

    def bwd(s):
        return n_all - 1 - s

    def pspec(colblk, order):
        return pl.BlockSpec((1, blk, RET_W), lambda bb, s: (bb, order(s), colblk))

    def tspec(order):
        return pl.BlockSpec((blk, HEAD_DIM), lambda bb, s: (order(s), 0))

    in_specs = [pl.BlockSpec(memory_space=pltpu.SMEM)]
    for order in (fwd, bwd):
        in_specs += [pspec(COL_RET_Q // RET_W, order), pspec(COL_RET_K // RET_W, order),
                     pspec(COL_RET_V // RET_W, order), tspec(order), tspec(order)]
    nhd = 2 * RET_HEADS
    return pl.pallas_call(
        _ret_kernel,
        grid=(b, n_all),
        in_specs=in_specs,
        out_specs=[pl.BlockSpec((1, blk, RET_W), lambda bb, s: (bb, fwd(s), 0)),
                   pl.BlockSpec((1, blk, RET_W), lambda bb, s: (bb, bwd(s), 0))],
        out_shape=[jax.ShapeDtypeStruct((b, l_all, RET_W), BF16)] * 2,
        scratch_shapes=[pltpu.VMEM((nhd, HEAD_DIM, HEAD_DIM), F32), pltpu.VMEM((nhd, c, c), F32),
                        pltpu.VMEM((nhd, c, HEAD_DIM), F32), pltpu.VMEM((nhd, c, HEAD_DIM), F32)],
        compiler_params=_params(("arbitrary", "arbitrary")),
        name="retention_scan",
    )(log_gamma, p_main, p_main, p_main, cos2, sin2, p_main, p_main, p_main, cos2, sin2)


def _attprep_kernel(q_ref, k_ref, v_ref, c_ref, s_ref, qw_ref, kw_ref, qo_ref, ko_ref, vo_ref):
    cos2 = c_ref[...]
    sin2 = s_ref[...]

    def norm_rope(x, w):
        xf = x.astype(F32)
        y = xf * lax.rsqrt(jnp.mean(xf * xf, -1, keepdims=True) + EPS) * w
        return _rope(y, cos2, sin2)

    for h in range(ATT_HEADS):
        sl = slice(h * HEAD_DIM, (h + 1) * HEAD_DIM)
        qo_ref[0, :, sl] = (norm_rope(q_ref[0, :, sl], qw_ref[...]) * (QK_SCALE * LOG2_E)).astype(BF16)
    for h in range(ATT_KV_HEADS):
        sl = slice(h * HEAD_DIM, (h + 1) * HEAD_DIM)
        ko_ref[0, sl, :] = norm_rope(k_ref[0, :, sl], kw_ref[...]).T.astype(BF16)
        vo_ref[0, :, 2 * h * HEAD_DIM:(2 * h + 1) * HEAD_DIM] = v_ref[0, :, sl]
        vo_ref[0, :, (2 * h + 1) * HEAD_DIM:(2 * h + 2) * HEAD_DIM] = jnp.ones((v_ref.shape[1], HEAD_DIM), BF16)


def _attprep_call(p_main, cos2, sin2, qn_w, kn_w, *, tm):
    b, l_all, _ = p_main.shape
    return pl.pallas_call(
        _attprep_kernel,
        grid=(b, l_all // tm),
        in_specs=[pl.BlockSpec((1, tm, ATT_W), lambda bb, i: (bb, i, COL_ATT_Q // ATT_W)),
                  pl.BlockSpec((1, tm, ATT_KV_W), lambda bb, i: (bb, i, COL_ATT_K // ATT_KV_W)),
                  pl.BlockSpec((1, tm, ATT_KV_W), lambda bb, i: (bb, i, COL_ATT_V // ATT_KV_W)),
                  pl.BlockSpec((tm, HEAD_DIM), lambda bb, i: (i, 0)),
                  pl.BlockSpec((tm, HEAD_DIM), lambda bb, i: (i, 0)),
                  pl.BlockSpec((1, HEAD_DIM), lambda bb, i: (0, 0)),
                  pl.BlockSpec((1, HEAD_DIM), lambda bb, i: (0, 0))],
        out_specs=[pl.BlockSpec((1, tm, ATT_W), lambda bb, i: (bb, i, 0)),
                   pl.BlockSpec((1, ATT_KV_W, tm), lambda bb, i: (bb, 0, i)),
                   pl.BlockSpec((1, tm, 2 * ATT_KV_W), lambda bb, i: (bb, i, 0))],
        out_shape=[jax.ShapeDtypeStruct((b, l_all, ATT_W), BF16),
                   jax.ShapeDtypeStruct((b, ATT_KV_W, l_all), BF16),
                   jax.ShapeDtypeStruct((b, l_all, 2 * ATT_KV_W), BF16)],
        compiler_params=_params(("arbitrary", "arbitrary")),
        name="attn_prep",
    )(p_main, p_main, p_main, cos2, sin2, qn_w, kn_w)


def _att_kernel(*refs, tk):
    q_ref, kt_ref, v_ref = refs[0], refs[1], refs[2]
    o_ref, s_ref, mx_ref, m_ref, acc_ref = refs[-5], refs[-4], refs[-3], refs[-2], refs[-1]
    tq = q_ref.shape[1]
    n = kt_ref.shape[2] // tk
    lanes = tk // HEAD_DIM
    rb = 64

    def put_scores(c):
        col = pl.multiple_of(c * tk, tk)
        s_ref[c] = _dot(q_ref[0], kt_ref[0, :, pl.ds(col, tk)])

    def fold_max(c):
        for r in range(tq // rb):
            rows = slice(r * rb, (r + 1) * rb)
            acc = mx_ref[rows, :]
            for j in range(lanes):
                acc = jnp.maximum(acc, s_ref[c, rows, j * HEAD_DIM:(j + 1) * HEAD_DIM])
            mx_ref[rows, :] = acc

    mx_ref[...] = jnp.full(mx_ref.shape, -jnp.inf, F32)
    put_scores(0)

    @pl.loop(1, n)
    def _(c):
        put_scores(c)
        fold_max(c - 1)

    fold_max(n - 1)
    m_ref[...] = jnp.max(mx_ref[...], -1, keepdims=True)

    acc_ref[...] = jnp.zeros_like(acc_ref)

    @pl.loop(0, n)
    def _(c):
        row = pl.multiple_of(c * tk, tk)
        p = jnp.exp2((s_ref[c] - m_ref[...]).astype(BF16))
        acc_ref[...] += _dot(p, v_ref[0, pl.ds(row, tk), :])

    o_ref[0] = (acc_ref[:, :HEAD_DIM] / acc_ref[:, HEAD_DIM:]).astype(BF16)


def _key_chunk(n_keys):
    for tk in (4224, 768, 512, 256):
        if n_keys % tk == 0:
            return tk
    raise ValueError("key count must be a multiple of 256")


def _att_call(qn, kn, v1, *, l_lat, tq):
    b, l_all, _ = qn.shape
    l_ctx = l_all - l_lat
    group = ATT_HEADS // ATT_KV_HEADS
    tk = _key_chunk(l_all)
    y_lat = pl.pallas_call(
        functools.partial(_att_kernel, tk=tk),
        grid=(b, ATT_HEADS, l_lat // tq),
        in_specs=[pl.BlockSpec((1, tq, HEAD_DIM), lambda bb, h, i: (bb, i, h)),
                  pl.BlockSpec((1, HEAD_DIM, l_all), lambda bb, h, i: (bb, h // group, 0),
                               pipeline_mode=pl.Buffered(1)),
                  pl.BlockSpec((1, l_all, 2 * HEAD_DIM), lambda bb, h, i: (bb, 0, h // group),
                               pipeline_mode=pl.Buffered(1))],
        out_specs=pl.BlockSpec((1, tq, HEAD_DIM), lambda bb, h, i: (bb, i, h)),
        out_shape=jax.ShapeDtypeStruct((b, l_all, ATT_W), BF16),
        scratch_shapes=[pltpu.VMEM((l_all // tk, tq, tk), F32), pltpu.VMEM((tq, HEAD_DIM), F32), pltpu.VMEM((tq, 1), F32),
                        pltpu.VMEM((tq, 2 * HEAD_DIM), F32)],
        compiler_params=_params(("arbitrary", "arbitrary", "arbitrary")),
        name="attention",
    )(qn, kn, v1)
    ctx_blk = l_lat // l_ctx
    return pl.pallas_call(
        functools.partial(_att_kernel, tk=l_ctx),
        grid=(b, ATT_HEADS),
        in_specs=[pl.BlockSpec((1, l_ctx, HEAD_DIM), lambda bb, h: (bb, ctx_blk, h)),
                  pl.BlockSpec((1, HEAD_DIM, l_ctx), lambda bb, h: (bb, h // group, ctx_blk)),
                  pl.BlockSpec((1, l_ctx, 2 * HEAD_DIM), lambda bb, h: (bb, ctx_blk, h // group)),
                  pl.BlockSpec(memory_space=pl.ANY)],
        out_specs=pl.BlockSpec((1, l_ctx, HEAD_DIM), lambda bb, h: (bb, ctx_blk, h)),
        out_shape=jax.ShapeDtypeStruct((b, l_all, ATT_W), BF16),
        scratch_shapes=[pltpu.VMEM((1, l_ctx, l_ctx), F32), pltpu.VMEM((l_ctx, HEAD_DIM), F32), pltpu.VMEM((l_ctx, 1), F32),
                        pltpu.VMEM((l_ctx, 2 * HEAD_DIM), F32)],
        input_output_aliases={3: 0},
        compiler_params=_params(("arbitrary", "arbitrary")),
        name="attention_ctx",
    )(qn, kn, v1, y_lat)


def _dnprep_kernel(*refs, tm, l_lat, l_all):
    mains, prevs, nexts = refs[0:3], refs[3:6], refs[6:9]
    cw_ref, pab_ref, alog_ref, dtb_ref = refs[9:13]
    outs = refs[13:16]
    g_ref, ext_ref = refs[16], refs[17]
    i = pl.program_id(1)
    halo = 16
    pad = DN_CONV_K // 2
    first = jnp.logical_or(i == 0, i == l_lat // tm)
    last = jnp.logical_or(i == l_lat // tm - 1, i == l_all // tm - 1)

    keep_prev = jnp.where(first, 0.0, 1.0)
    keep_next = jnp.where(last, 0.0, 1.0)

    for part in range(3):
        ext_ref[0:halo, :] = prevs[part][0].astype(F32) * keep_prev
        ext_ref[halo:halo + tm, :] = mains[part][0].astype(F32)
        ext_ref[halo + tm:, :] = nexts[part][0].astype(F32) * keep_next
        acc = jnp.zeros((tm, DN_W), F32)
        for j in range(DN_CONV_K):
            w_j = cw_ref[j:j + 1, part * DN_W:(part + 1) * DN_W]
            acc = acc + w_j * ext_ref[pl.ds(halo - pad + j, tm), :]
        y = _silu(acc)
        for h in range(DN_HEADS):
            sl = slice(h * HEAD_DIM, (h + 1) * HEAD_DIM)
            yh = y[:, sl]
            if part < 2:
                yh = yh * lax.rsqrt(jnp.sum(yh * yh, -1, keepdims=True) + EPS)
            if part == 0:
                yh = yh * QK_SCALE
            outs[part][0, :, sl] = yh.astype(BF16)

    a = pab_ref[0]
    col = lax.broadcasted_iota(jnp.int32, a.shape, 1)
    z = a + dtb_ref[...]
    softplus = jnp.maximum(z, 0.0) + jnp.log(1.0 + jnp.exp(-jnp.abs(z)))
    g = -jnp.exp(alog_ref[...]) * softplus
    ii = lax.broadcasted_iota(jnp.int32, (tm, tm), 0)
    rr = lax.broadcasted_iota(jnp.int32, (tm, tm), 1)
    same_chunk = (ii // DN_CHUNK) == (rr // DN_CHUNK)
    prefix = jnp.logical_and(same_chunk, rr <= ii).astype(F32)
    suffix = jnp.logical_and(same_chunk, rr >= ii).astype(F32)
    hi = lax.Precision.HIGHEST
    cum_f = jnp.dot(prefix, g, preferred_element_type=F32, precision=hi)
    cum_b = jnp.dot(suffix, g, preferred_element_type=F32, precision=hi)
    g_ref[0] = jnp.where(col < DN_HEADS, cum_f, jnp.where(col < N_DN_GATES, cum_b, jax.nn.sigmoid(a)))


def _dnprep_call(p_main, p_ab, conv_w, alog_row, dtb_row, *, l_lat, tm):
    b, l_all, _ = p_main.shape
    halo = 16
    r = tm // halo
    cb = COL_DN_QKV // DN_W
    in_specs = []
    for part in range(3):
        in_specs.append(pl.BlockSpec((1, tm, DN_W), lambda bb, i, p=part: (bb, i, cb + p)))
    for part in range(3):
        in_specs.append(pl.BlockSpec((1, halo, DN_W), lambda bb, i, p=part: (bb, jnp.maximum(i * r - 1, 0), cb + p)))
    for part in range(3):
        in_specs.append(pl.BlockSpec(
            (1, halo, DN_W), lambda bb, i, p=part: (bb, jnp.minimum((i + 1) * r, l_all // halo - 1), cb + p)))
    in_specs += [pl.BlockSpec((DN_CONV_K, 3 * DN_W), lambda bb, i: (0, 0)),
                 pl.BlockSpec((1, tm, GATE_COLS), lambda bb, i: (bb, i, 0)),
                 pl.BlockSpec((1, GATE_COLS), lambda bb, i: (0, 0)),
                 pl.BlockSpec((1, GATE_COLS), lambda bb, i: (0, 0))]
    kern = functools.partial(_dnprep_kernel, tm=tm, l_lat=l_lat, l_all=l_all)
    return pl.pallas_call(
        kern,
        grid=(b, l_all // tm),
        in_specs=in_specs,
        out_specs=[pl.BlockSpec((1, tm, DN_W), lambda bb, i: (bb, i, 0))] * 3
        + [pl.BlockSpec((1, tm, GATE_COLS), lambda bb, i: (bb, i, 0))],
        out_shape=[jax.ShapeDtypeStruct((b, l_all, DN_W), BF16)] * 3
        + [jax.ShapeDtypeStruct((b, l_all, GATE_COLS), F32)],
        scratch_shapes=[pltpu.VMEM((tm + 2 * halo, DN_W), F32)],
        compiler_params=_params(("arbitrary", "arbitrary")),
        name="deltanet_conv_gates",
    )(*([p_main] * 9), conv_w, p_ab, alog_row, dtb_row)


def _neumann_inverse_many(a_list, eye):
    ps = [-a for a in a_list]
    ts = [eye + p for p in ps]
    for _ in range(5):
        pbs = [p.astype(BF16) for p in ps]
        ps = [_dot(pb, pb) for pb in pbs]
        ts = [t + _dot(t.astype(BF16), p.astype(BF16)) for t, p in zip(ts, ps)]
    return ts


def _dnchunk_kernel(q_ref, k_ref, v_ref, g_ref, gt_ref, w_ref, kc_ref, qg_ref, kgt_ref, qk_ref, gl_ref):
    c = DN_CHUNK
    n_sub = q_ref.shape[1] // c
    ii = lax.broadcasted_iota(jnp.int32, (c, c), 0)
    jj = lax.broadcasted_iota(jnp.int32, (c, c), 1)
    eye = (ii == jj).astype(F32)

    qk_ref[...] = jnp.zeros_like(qk_ref)
    kgt_ref[...] = jnp.zeros_like(kgt_ref)
    gl_ref[...] = jnp.zeros_like(gl_ref)

    heads = range(DN_HEADS)
    subs = range(n_sub)
    chains = [(u, d, h) for u in subs for h in heads for d in range(2)]
    rows = [slice(u * c, (u + 1) * c) for u in subs]
    sls = [slice(h * HEAD_DIM, (h + 1) * HEAD_DIM) for h in heads]
    gcol = [g_ref[0, rows[u], :] for u in subs]
    grow = [gt_ref[0, u] for u in subs]
    q16 = {(u, h): q_ref[0, rows[u], sls[h]] for u in subs for h in heads}
    k16 = {(u, h): k_ref[0, rows[u], sls[h]] for u in subs for h in heads}
    kk = {uh: _dot_nt(k16[uh], k16[uh]) for uh in k16}
    qk0 = {uh: _dot_nt(q16[uh], k16[uh]) for uh in k16}

    gcc, tot, beta, decay, a_list = {}, {}, {}, {}, []
    for u, d, h in chains:
        col = d * DN_HEADS + h
        key = (u, d, h)
        gcc[key] = gcol[u][:, col:col + 1]
        gcr = grow[u][col:col + 1, :]
        tot[key] = gcol[u][c - 1:c, col:col + 1] if d == 0 else gcol[u][0:1, col:col + 1]
        beta[key] = gcol[u][:, N_DN_GATES + col:N_DN_GATES + col + 1]
        incl = (ii >= jj) if d == 0 else (ii <= jj)
        strict = (ii > jj) if d == 0 else (ii < jj)
        decay[key] = jnp.where(incl, jnp.exp(jnp.where(incl, gcc[key] - gcr, 0.0)), 0.0)
        a_list.append(jnp.where(strict, kk[u, h] * beta[key] * decay[key], 0.0))
        gl_ref[0, d, u, h:h + 1, :] = jnp.broadcast_to(jnp.exp(tot[key]), (1, HEAD_DIM))

    t16 = [t.astype(BF16) for t in _neumann_inverse_many(a_list, eye)]

    for key, t in zip(chains, t16):
        u, d, h = key
        sl = sls[h]
        k = k16[u, h].astype(F32)
        e_col = jnp.exp(gcc[key])
        rhs = jnp.concatenate([(v_ref[0, rows[u], sl].astype(F32) * beta[key]).astype(BF16),
                               (k * (beta[key] * e_col)).astype(BF16)], axis=1)
        wk = _dot(t, rhs)
        w_ref[0, d, rows[u], sl] = wk[:, :HEAD_DIM]
        kc_ref[0, d, rows[u], sl] = wk[:, HEAD_DIM:].astype(BF16)
        qg_ref[0, d, rows[u], sl] = (q16[u, h].astype(F32) * e_col).astype(BF16)
        kg = k * jnp.exp(tot[key] - gcc[key])
        kgt_ref[0, d, u, :, pl.ds(h * HEAD_DIM, c)] = kg.T.astype(BF16)
        qk_ref[0, d, rows[u], pl.ds(h * HEAD_DIM, c)] = (qk0[u, h] * decay[key]).astype(BF16)


DN_STEP_CHUNKS = 4


def _dnchunk_call(qd, kd, vd, g, gt):
    b, l_all, _ = qd.shape
    c = DN_CHUNK
    n = l_all // c
    ns = DN_STEP_CHUNKS
    tok = lambda dt: jax.ShapeDtypeStruct((b, 2, l_all, DN_W), dt)
    tok_spec = pl.BlockSpec((1, 2, ns * c, DN_W), lambda bb, s: (bb, 0, s, 0))
    return pl.pallas_call(
        _dnchunk_kernel,
        grid=(b, n // ns),
        in_specs=[pl.BlockSpec((1, ns * c, DN_W), lambda bb, s: (bb, s, 0))] * 3
        + [pl.BlockSpec((1, ns * c, GATE_COLS), lambda bb, s: (bb, s, 0)),
           pl.BlockSpec((1, ns, 2 * N_DN_GATES, c), lambda bb, s: (bb, s, 0, 0))],
        out_specs=[tok_spec, tok_spec, tok_spec,
                   pl.BlockSpec((1, 2, ns, HEAD_DIM, DN_W), lambda bb, s: (bb, 0, s, 0, 0)),
                   tok_spec,
                   pl.BlockSpec((1, 2, ns, 8, HEAD_DIM), lambda bb, s: (bb, 0, s, 0, 0))],
        out_shape=[tok(F32), tok(BF16), tok(BF16),
                   jax.ShapeDtypeStruct((b, 2, n, HEAD_DIM, DN_W), BF16),
                   tok(BF16),
                   jax.ShapeDtypeStruct((b, 2, n, 8, HEAD_DIM), F32)],
        compiler_params=_params(("arbitrary", "arbitrary")),
        name="deltanet_chunk_factors",
    )(qd, kd, vd, g, gt)


def _dnscan_kernel(*refs):
    ins_f, ins_b = refs[0:6], refs[6:12]
    of_ref, ob_ref, s_ref = refs[12], refs[13], refs[14]
    c = DN_CHUNK
    n_sub = of_ref.shape[1] // c
    step = pl.program_id(1)

    @pl.when(step == 0)
    def _():
        s_ref[...] = jnp.zeros_like(s_ref)

    ins, outs = (ins_f, ins_b), (of_ref, ob_ref)
    chains = [(d, h) for d in range(2) for h in range(DN_HEADS)]
    sls = [slice(h * HEAD_DIM, (h + 1) * HEAD_DIM) for h in range(DN_HEADS)]
    state = [s_ref[d * DN_HEADS + h] for d, h in chains]
    for j in range(n_sub):
        sub = [j, n_sub - 1 - j]
        rows = [slice(u * c, (u + 1) * c) for u in sub]
        s16 = [s.astype(BF16) for s in state]
        v16 = [(ins[d][0][0, 0, rows[d], sls[h]] - _dot(ins[d][1][0, 0, rows[d], sls[h]], s)).astype(BF16)
               for (d, h), s in zip(chains, s16)]
        o_inter = [_dot(ins[d][2][0, 0, rows[d], sls[h]], s) for (d, h), s in zip(chains, s16)]
        new_state = []
        for (d, h), s, v, oi in zip(chains, state, v16, o_inter):
            qk = ins[d][4][0, 0, rows[d], pl.ds(h * HEAD_DIM, c)]
            outs[d][0, rows[d], sls[h]] = (oi + _dot(qk, v)).astype(outs[d].dtype)
            kgt = ins[d][3][0, 0, sub[d], :, pl.ds(h * HEAD_DIM, c)]
            new_state.append(s * ins[d][5][0, 0, sub[d], h:h + 1, :] + _dot(kgt, v))
        state = new_state
    for i, s in enumerate(state):
        s_ref[i] = s


def _dnscan_call(w, kc, qg, kgt, qk, gl, *, l_lat):
    b, _, l_all, _ = w.shape
    ns = DN_STEP_CHUNKS
    blk = ns * DN_CHUNK
    n_all, n_lat = l_all // blk, l_lat // blk
    n_ctx = n_all - n_lat

    def fwd(s):
        return jnp.where(s < n_ctx, n_lat + s, s - n_ctx)

    def bwd(s):
        return n_all - 1 - s

    in_specs = []
    for d, order in enumerate((fwd, bwd)):
        tok_spec = pl.BlockSpec((1, 1, blk, DN_W), lambda bb, s, d=d, o=order: (bb, d, o(s), 0))
        in_specs += [tok_spec, tok_spec, tok_spec,
                     pl.BlockSpec((1, 1, ns, HEAD_DIM, DN_W), lambda bb, s, d=d, o=order: (bb, d, o(s), 0, 0)),
                     tok_spec,
                     pl.BlockSpec((1, 1, ns, 8, HEAD_DIM), lambda bb, s, d=d, o=order: (bb, d, o(s), 0, 0))]
    return pl.pallas_call(
        _dnscan_kernel,
        grid=(b, n_all),
        in_specs=in_specs,
        out_specs=[pl.BlockSpec((1, blk, DN_W), lambda bb, s: (bb, fwd(s), 0)),
                   pl.BlockSpec((1, blk, DN_W), lambda bb, s: (bb, bwd(s), 0))],
        out_shape=[jax.ShapeDtypeStruct((b, l_all, DN_W), BF16)] * 2,
        scratch_shapes=[pltpu.VMEM((2 * DN_HEADS, HEAD_DIM, HEAD_DIM), F32)],
        compiler_params=_params(("arbitrary", "arbitrary")),
        name="deltanet_scan",
    )(w, kc, qg, kgt, qk, gl, w, kc, qg, kgt, qk, gl)


def _rope_tables(l_lat, l_ctx):
    rows = l_lat // GRID_W
    n_freq = HEAD_DIM // 4
    inv = ROPE_THETA ** (-jnp.arange(n_freq, dtype=F32) / n_freq)
    row_ang = jnp.arange(rows, dtype=F32)[:, None] * inv
    col_ang = jnp.arange(GRID_W, dtype=F32)[:, None] * inv

    def table(fn):
        return jnp.concatenate([jnp.repeat(fn(row_ang), GRID_W, axis=0), jnp.tile(fn(col_ang), (rows, 1))], -1)

    cos, sin = table(jnp.cos), table(jnp.sin)
    cos2 = jnp.concatenate([cos, cos], -1)
    sin2 = jnp.concatenate([-sin, sin], -1)
    cos2 = jnp.concatenate([cos2, jnp.ones((l_ctx, HEAD_DIM), F32)], 0)
    sin2 = jnp.concatenate([sin2, jnp.zeros((l_ctx, HEAD_DIM), F32)], 0)
    return cos2, sin2


def _pad_row(v, width):
    v = v.reshape(1, -1).astype(F32)
    return jnp.pad(v, ((0, 0), (0, width - v.shape[1])))


def _row_tile(l_lat, l_ctx):
    l_all = l_lat + l_ctx
    for tm in (768, 512, 256, 128):
        if l_all % tm == 0:
            return tm
    raise ValueError("token count must be a multiple of 128")


def kernel(x, c, ctx, c_ctx, w_ada, b_ada, w_in, ret_decay_logit, dn_conv_w, dn_a_log, dn_dt_bias, dn_norm_w,
           att_qn_w, att_kn_w, w_o, ln1_w, ln1_b, w_ffn_in, w_ffn_out, ln2_w, ln2_b):
    bsz, l_lat, d = x.shape
    l_ctx = ctx.shape[1]
    l_all = l_lat + l_ctx
    depth = w_ada.shape[0]
    assert l_lat % 256 == 0 and l_ctx % 256 == 0 and l_lat % GRID_W == 0
    tm = _row_tile(l_lat, l_ctx)
    tn = MAIN_W // 2

    cos2, sin2 = _rope_tables(l_lat, l_ctx)

    cond_rows = 8 * pl.cdiv(bsz + 1, 8)
    cond = jnp.concatenate([c, c_ctx[None, :]], 0)
    cond = jnp.pad(cond, ((0, cond_rows - bsz - 1), (0, 0)))
    mod = _ada_call(cond, w_ada, b_ada)

    gate_lo = COL_DN_Z + DN_W
    for i in range(depth):
        last = i == depth - 1
        ml = mod[i, :bsz].reshape(bsz, 6, d)
        mc = mod[i, bsz].reshape(1, 6, d)
        w_main = jnp.concatenate([w_in[i, :, :gate_lo], w_in[i, :, gate_lo + 2 * N_DN_GATES:]], 1).astype(BF16)
        w_ab = jnp.pad(w_in[i, :, gate_lo:gate_lo + 2 * N_DN_GATES],
                       ((0, 0), (0, GATE_COLS - 2 * N_DN_GATES))).astype(BF16)

        if i == 0:
            p_main, p_ab, x_all = _inproj_call((x, ctx), ml, mc, w_main, w_ab, l_lat=l_lat, tm=tm, tn=tn // 2)
        else:
            p_main, p_ab = _inproj_call(x_all, ml, mc, w_main, w_ab, l_lat=l_lat, tm=tm, tn=tn)

        log_gamma = jax.nn.log_sigmoid(ret_decay_logit[i].astype(F32))
        ret_f, ret_b = _ret_call(log_gamma, p_main, cos2, sin2, l_lat=l_lat)

        qn, kn, v1 = _attprep_call(p_main, cos2, sin2, att_qn_w[i].reshape(1, -1), att_kn_w[i].reshape(1, -1),
                                   tm=tm)
        y_att = _att_call(qn, kn, v1, l_lat=l_lat, tq=1024 if l_lat % 1024 == 0 else 256)

        qd, kd, vd, g = _dnprep_call(p_main, p_ab, dn_conv_w[i], _pad_row(dn_a_log[i], GATE_COLS),
                                     _pad_row(dn_dt_bias[i], GATE_COLS), l_lat=l_lat, tm=256)
        gt = g[:, :, :2 * N_DN_GATES].reshape(bsz, l_all // DN_CHUNK, DN_CHUNK, 2 * N_DN_GATES)
        gt = jnp.swapaxes(gt, 2, 3)
        dn_f, dn_b = _dnscan_call(*_dnchunk_call(qd, kd, vd, g, gt), l_lat=l_lat)

        x_all = _wo_call(ret_f, ret_b, dn_f, dn_b, p_main, dn_norm_w[i].reshape(1, -1), y_att, x_all, ml, mc,
                         w_o[i].astype(BF16), ln1_w[i].reshape(1, -1), ln1_b[i].reshape(1, -1), l_lat=l_lat, tm=tm)
        x_all = _ffn_call(x_all, ml, mc, w_ffn_in[i].astype(BF16), w_ffn_out[i].astype(BF16),
                          ln2_w[i].reshape(1, -1), ln2_b[i].reshape(1, -1), l_lat=l_lat,
                          out_rows=l_lat if last else l_all, tm=tm, tf=512)
    return x_all
```

```python
import functools

import jax
import jax.numpy as jnp
from jax import lax
from jax.experimental import pallas as pl
from jax.experimental.pallas import tpu as pltpu

F32 = jnp.float32
BF16 = jnp.bfloat16

HEAD_DIM = 128
RET_HEADS = 4
DN_HEADS = 4
ATT_HEADS = 8
ATT_KV_HEADS = 2
RET_W = RET_HEADS * HEAD_DIM
DN_W = DN_HEADS * HEAD_DIM
ATT_W = ATT_HEADS * HEAD_DIM
ATT_KV_W = ATT_KV_HEADS * HEAD_DIM
RET_CHUNK = 128
DN_CHUNK = 64
DN_CONV_K = 5
GRID_W = 64
ROPE_THETA = 10000.0
MODEL_DEPTH = 4
DEEPNORM_ALPHA = (2 * MODEL_DEPTH) ** 0.25
EPS = 1e-6
QK_SCALE = HEAD_DIM ** -0.5
LOG2_E = 1.4426950408889634

COL_RET_Q = 0
COL_RET_K = RET_W
COL_RET_V = 2 * RET_W
COL_RET_G = 3 * RET_W
COL_DN_QKV = 4 * RET_W
COL_DN_Z = COL_DN_QKV + 3 * DN_W
COL_ATT_Q = COL_DN_Z + DN_W
COL_ATT_K = COL_ATT_Q + ATT_W
COL_ATT_V = COL_ATT_K + ATT_KV_W
MAIN_W = COL_ATT_V + ATT_KV_W
GATE_COLS = 128
N_DN_GATES = 2 * DN_HEADS

VMEM_LIMIT_MB = 56


def _params(sem, vmem_mb=VMEM_LIMIT_MB):
    return pltpu.CompilerParams(dimension_semantics=sem, vmem_limit_bytes=vmem_mb * 1024 * 1024)


def _dot(a, b):
    return jnp.dot(a, b, preferred_element_type=F32)


def _dot_nt(a, b):
    return lax.dot_general(a, b, (((1,), (1,)), ((), ())), preferred_element_type=F32)


def _silu(x):
    return x * jax.nn.sigmoid(x)


def _is_ctx_rows(i, tm, l_lat):
    rows = i * tm + lax.broadcasted_iota(jnp.int32, (tm, 1), 0)
    return rows >= l_lat


def _mod_row(is_ctx, ml_ref, mc_ref, k):
    return jnp.where(is_ctx, mc_ref[0, k:k + 1, :], ml_ref[0, k:k + 1, :])


def _layer_norm(r, w, b):
    mu = jnp.mean(r, -1, keepdims=True)
    rc = r - mu
    var = jnp.mean(rc * rc, -1, keepdims=True)
    return rc * lax.rsqrt(var + EPS) * w + b


def _split_bf16(a):
    hi = a.astype(BF16)
    return hi, (a - hi.astype(F32)).astype(BF16)


def _ada_kernel(c_ref, w_ref, b_ref, o_ref):
    h_hi, h_lo = _split_bf16(_silu(c_ref[...]))
    w_hi, w_lo = _split_bf16(w_ref[0])
    o_ref[0] = _dot(h_hi, w_hi) + _dot(h_hi, w_lo) + _dot(h_lo, w_hi) + b_ref[0]


def _ada_call(cond, w_ada, b_ada):
    depth, d, n6 = w_ada.shape
    rows = cond.shape[0]
    tn = 1024
    return pl.pallas_call(
        _ada_kernel,
        grid=(depth, n6 // tn),
        in_specs=[pl.BlockSpec((rows, d), lambda l, j: (0, 0)),
                  pl.BlockSpec((1, d, tn), lambda l, j: (l, 0, j)),
                  pl.BlockSpec((1, 1, tn), lambda l, j: (l, 0, j))],
        out_specs=pl.BlockSpec((1, rows, tn), lambda l, j: (l, 0, j)),
        out_shape=jax.ShapeDtypeStruct((depth, rows, n6), F32),
        compiler_params=_params(("arbitrary", "arbitrary")),
        name="ada_mod",
    )(cond, w_ada, b_ada.reshape(depth, 1, n6))


GATE_LO = COL_DN_Z + DN_W


def _repack_kernel(w_ref, wm_ref, wab_ref):
    wm_ref[0, :, :GATE_LO] = w_ref[0, :, :GATE_LO].astype(BF16)
    wm_ref[0, :, GATE_LO:] = w_ref[0, :, GATE_LO + 2 * N_DN_GATES:].astype(BF16)
    gates = w_ref[0, :, GATE_LO:GATE_LO + GATE_COLS]
    lane = lax.broadcasted_iota(jnp.int32, gates.shape, 1)
    wab_ref[0] = jnp.where(lane < 2 * N_DN_GATES, gates, 0.0).astype(BF16)


def _repack_call(w_in):
    depth, d, proj_w = w_in.shape
    assert proj_w == MAIN_W + 2 * N_DN_GATES
    tr = 256
    return pl.pallas_call(
        _repack_kernel,
        grid=(depth, d // tr),
        in_specs=[pl.BlockSpec((1, tr, proj_w), lambda l, r: (l, r, 0))],
        out_specs=[pl.BlockSpec((1, tr, MAIN_W), lambda l, r: (l, r, 0)),
                   pl.BlockSpec((1, tr, GATE_COLS), lambda l, r: (l, r, 0))],
        out_shape=[jax.ShapeDtypeStruct((depth, d, MAIN_W), BF16), jax.ShapeDtypeStruct((depth, d, GATE_COLS), BF16)],
        compiler_params=_params(("arbitrary", "arbitrary")),
        name="repack_w_in",
    )(w_in)


def _inproj_kernel(*refs, tm, l_lat, split_input):
    if split_input:
        x_ref, c_ref, ml_ref, mc_ref, w_ref, wab_ref, pm_ref, pab_ref, xo_ref, h_ref = refs
    else:
        x_ref, ml_ref, mc_ref, w_ref, wab_ref, pm_ref, pab_ref, h_ref = refs
    i = pl.program_id(1)
    j = pl.program_id(2)

    @pl.when(j == 0)
    def _():
        if split_input:
            n_full, rem = divmod(l_lat, tm)

            @pl.when(i < n_full)
            def _():
                xo_ref[0] = x_ref[0]

            @pl.when(i >= n_full)
            def _():
                if rem:
                    xo_ref[0, :rem, :] = x_ref[0, :rem, :]
                xo_ref[0, rem:, :] = c_ref[0]

            x_tile = xo_ref[0]
        else:
            x_tile = x_ref[0]
        is_ctx = _is_ctx_rows(i, tm, l_lat)
        shift = _mod_row(is_ctx, ml_ref, mc_ref, 0)
        scale = _mod_row(is_ctx, ml_ref, mc_ref, 1)
        h = (x_tile * (1.0 + scale) + shift).astype(BF16)
        h_ref[...] = h
        pab_ref[0] = _dot(h, wab_ref[...])

    tn = w_ref.shape[1]
    step = 512
    for lo in range(0, tn, step):
        hi = min(lo + step, tn)
        pm_ref[0, :, lo:hi] = _dot(h_ref[...], w_ref[:, lo:hi]).astype(BF16)


def _inproj_call(xs, ml, mc, w_main, w_ab, *, layer, l_lat, tm, tn):
    split_input = isinstance(xs, tuple)
    xs = list(xs) if split_input else [xs]
    b, _, d = xs[0].shape
    l_all = l_lat + xs[1].shape[1] if split_input else xs[0].shape[1]
    x_spec = pl.BlockSpec((1, tm, d), lambda bb, i, j: (bb, i, 0))
    in_specs = [x_spec]
    out_specs = [pl.BlockSpec((1, tm, tn), lambda bb, i, j: (bb, i, j)),
                 pl.BlockSpec((1, tm, GATE_COLS), lambda bb, i, j: (bb, i, 0))]
    out_shape = [jax.ShapeDtypeStruct((b, l_all, MAIN_W), BF16), jax.ShapeDtypeStruct((b, l_all, GATE_COLS), F32)]
    if split_input:
        l_ctx = l_all - l_lat
        assert l_lat % tm + l_ctx == tm, "the context must exactly fill the last row tile"
        in_specs.append(pl.BlockSpec((1, l_ctx, d), lambda bb, i, j: (bb, 0, 0)))
        out_specs.append(x_spec)
        out_shape.append(jax.ShapeDtypeStruct((b, l_all, d), F32))
    in_specs +=[pl.BlockSpec((1, 6, d), lambda bb, i, j: (bb, 0, 0)),
                 pl.BlockSpec((1, 6, d), lambda bb, i, j: (0, 0, 0)),
                 pl.BlockSpec((None, d, tn), lambda bb, i, j: (layer, 0, j)),
                 pl.BlockSpec((None, d, GATE_COLS), lambda bb, i, j: (layer, 0, 0))]
    return pl.pallas_call(
        functools.partial(_inproj_kernel, tm=tm, l_lat=l_lat, split_input=split_input),
        grid=(b, l_all // tm, MAIN_W // tn),
        in_specs=in_specs,
        out_specs=out_specs,
        out_shape=out_shape,
        scratch_shapes=[pltpu.VMEM((tm, d), BF16)],
        compiler_params=_params(("arbitrary", "arbitrary", "arbitrary")),
        name="in_proj",
    )(*xs, ml, mc, w_main, w_ab)


WO_SUB_ROWS = 256


def _wo_kernel(rf_ref, rb_ref, rg_ref, df_ref, db_ref, dz_ref, nw_ref, a_ref, x_ref, ml_ref, mc_ref, w_ref,
               lnw_ref, lnb_ref, o_ref, *, tm, l_lat):
    i = pl.program_id(1)
    sub = WO_SUB_ROWS
    n_rd = RET_W + DN_W

    def head_norm(o):
        return o * lax.rsqrt(jnp.mean(o * o, -1, keepdims=True) + EPS)

    for lo in range(0, tm, sub):
        rows = slice(lo, lo + sub)
        ys = []
        for h in range(RET_HEADS):
            sl = slice(h * HEAD_DIM, (h + 1) * HEAD_DIM)
            o = rf_ref[0, rows, sl].astype(F32) + rb_ref[0, rows, sl].astype(F32)
            ys.append((head_norm(o) * _silu(rg_ref[0, rows, sl].astype(F32))).astype(BF16))
        for h in range(DN_HEADS):
            sl = slice(h * HEAD_DIM, (h + 1) * HEAD_DIM)
            o = df_ref[0, rows, sl].astype(F32) + db_ref[0, rows, sl].astype(F32)
            ys.append((head_norm(o) * nw_ref[...] * _silu(dz_ref[0, rows, sl].astype(F32))).astype(BF16))
        y_rd = jnp.concatenate(ys, axis=1)
        acc = _dot(y_rd, w_ref[0:n_rd, :]) + _dot(a_ref[0, rows, :], w_ref[n_rd:, :])
        rid = i * tm + lo + lax.broadcasted_iota(jnp.int32, (sub, 1), 0)
        gate = _mod_row(rid >= l_lat, ml_ref, mc_ref, 2)
        r = DEEPNORM_ALPHA * x_ref[0, rows, :] + gate * acc
        o_ref[0, rows, :] = _layer_norm(r, lnw_ref[...], lnb_ref[...])


def _wo_call(ret_f, ret_b, dn_f, dn_b, p_main, dn_norm_w, y_att, x_all, ml, mc, w_o, ln_w, ln_b, *, l_lat, tm):
    b, l_all, d = x_all.shape
    kern = functools.partial(_wo_kernel, tm=tm, l_lat=l_lat)
    o_spec = pl.BlockSpec((1, tm, RET_W), lambda bb, i: (bb, i, 0))
    return pl.pallas_call(
        kern,
        grid=(b, l_all // tm),
        in_specs=[o_spec, o_spec,
                  pl.BlockSpec((1, tm, RET_W), lambda bb, i: (bb, i, COL_RET_G // RET_W)),
                  o_spec, o_spec,
                  pl.BlockSpec((1, tm, DN_W), lambda bb, i: (bb, i, COL_DN_Z // DN_W)),
                  pl.BlockSpec((1, HEAD_DIM), lambda bb, i: (0, 0)),
                  pl.BlockSpec((1, tm, ATT_W), lambda bb, i: (bb, i, 0)),
                  pl.BlockSpec((1, tm, d), lambda bb, i: (bb, i, 0)),
                  pl.BlockSpec((1, 6, d), lambda bb, i: (bb, 0, 0)),
                  pl.BlockSpec((1, 6, d), lambda bb, i: (0, 0, 0)),
                  pl.BlockSpec((RET_W + DN_W + ATT_W, d), lambda bb, i: (0, 0), pipeline_mode=pl.Buffered(1)),
                  pl.BlockSpec((1, d), lambda bb, i: (0, 0)),
                  pl.BlockSpec((1, d), lambda bb, i: (0, 0))],
        out_specs=pl.BlockSpec((1, tm, d), lambda bb, i: (bb, i, 0)),
        out_shape=jax.ShapeDtypeStruct((b, l_all, d), F32),
        compiler_params=_params(("arbitrary", "arbitrary")),
        name="w_o_postnorm",
    )(ret_f, ret_b, p_main, dn_f, dn_b, p_main, dn_norm_w, y_att, x_all, ml, mc, w_o, ln_w, ln_b)


def _ffn_kernel(x_ref, ml_ref, mc_ref, wg_ref, wu_ref, wo_ref, lnw_ref, lnb_ref, o_ref, h_ref, acc_ref,
                *, tm, l_lat):
    i = pl.program_id(1)
    f = pl.program_id(2)

    @pl.when(f == 0)
    def _():
        is_ctx = _is_ctx_rows(i, tm, l_lat)
        shift = _mod_row(is_ctx, ml_ref, mc_ref, 3)
        scale = _mod_row(is_ctx, ml_ref, mc_ref, 4)
        h_ref[...] = (x_ref[0] * (1.0 + scale) + shift).astype(BF16)
        acc_ref[...] = jnp.zeros_like(acc_ref)

    h = h_ref[...]
    g = _dot(h, wg_ref[...])
    u = _dot(h, wu_ref[...])
    acc_ref[...] += _dot((_silu(g) * u).astype(BF16), wo_ref[...])

    @pl.when(f == pl.num_programs(2) - 1)
    def _():
        is_ctx = _is_ctx_rows(i, tm, l_lat)
        gate = _mod_row(is_ctx, ml_ref, mc_ref, 5)
        r = DEEPNORM_ALPHA * x_ref[0] + gate * acc_ref[...]
        o_ref[0] = _layer_norm(r, lnw_ref[...], lnb_ref[...])


def _ffn_call(x_all, ml, mc, w_in, w_out, ln_w, ln_b, *, l_lat, out_rows, tm, tf):
    b, _, d = x_all.shape
    d_ff = w_out.shape[0]
    nf = d_ff // tf
    kern = functools.partial(_ffn_kernel, tm=tm, l_lat=l_lat)
    return pl.pallas_call(
        kern,
        grid=(b, pl.cdiv(out_rows, tm), nf),
        in_specs=[pl.BlockSpec((1, tm, d), lambda bb, i, f: (bb, i, 0)),
                  pl.BlockSpec((1, 6, d), lambda bb, i, f: (bb, 0, 0)),
                  pl.BlockSpec((1, 6, d), lambda bb, i, f: (0, 0, 0)),
                  pl.BlockSpec((d, tf), lambda bb, i, f: (0, f)),
                  pl.BlockSpec((d, tf), lambda bb, i, f: (0, f + nf)),
                  pl.BlockSpec((tf, d), lambda bb, i, f: (f, 0)),
                  pl.BlockSpec((1, d), lambda bb, i, f: (0, 0)),
                  pl.BlockSpec((1, d), lambda bb, i, f: (0, 0))],
        out_specs=pl.BlockSpec((1, tm, d), lambda bb, i, f: (bb, i, 0)),
        out_shape=jax.ShapeDtypeStruct((b, out_rows, d), F32),
        scratch_shapes=[pltpu.VMEM((tm, d), BF16), pltpu.VMEM((tm, d), F32)],
        compiler_params=_params(("arbitrary", "arbitrary", "arbitrary")),
        name="ffn_postnorm",
    )(x_all, ml, mc, w_in, w_in, w_out, ln_w, ln_b)


def _rope(x, cos2, sin2):
    return x * cos2 + pltpu.roll(x, HEAD_DIM // 2, 1) * sin2


def _ret_kernel(lg_ref, qf_ref, kf_ref, vf_ref, cf_ref, sf_ref, qb_ref, kb_ref, vb_ref, cb_ref, sb_ref,
                of_ref, ob_ref, s_ref, dm_ref, qd_ref, kd_ref):
    step = pl.program_id(1)
    c = RET_CHUNK

    @pl.when(step == 0)
    def _():
        s_ref[...] = jnp.zeros_like(s_ref)
        ii = lax.broadcasted_iota(jnp.int32, (c, c), 0).astype(F32)
        jj = lax.broadcasted_iota(jnp.int32, (c, c), 1).astype(F32)
        for d in range(2):
            for h in range(RET_HEADS):
                lg = lg_ref[d, h]
                if d == 0:
                    rel, qe, ke = ii - jj, ii + 1.0, (c - 1.0) - ii
                else:
                    rel, qe, ke = jj - ii, c - ii, ii
                idx = d * RET_HEADS + h
                dm_ref[idx] = jnp.where(rel >= 0, jnp.exp(jnp.maximum(rel, 0.0) * lg), 0.0)
                qd_ref[idx] = jnp.exp(qe * lg)
                kd_ref[idx] = jnp.exp(ke * lg)

    dirs = ((qf_ref, kf_ref, vf_ref, cf_ref, sf_ref, of_ref), (qb_ref, kb_ref, vb_ref, cb_ref, sb_ref, ob_ref))
    n_sub = of_ref.shape[1] // c
    chains = [(d, h) for d in range(2) for h in range(RET_HEADS)]
    sls = [slice(h * HEAD_DIM, (h + 1) * HEAD_DIM) for h in range(RET_HEADS)]
    order = [(j, n_sub - 1 - j) for j in range(n_sub)]
    rows = [[slice(u * c, (u + 1) * c) for u in sub] for sub in order]
    qs, ks, vs, qk, kv = {}, {}, {}, {}, {}
    for j in range(n_sub):
        for i, (d, h) in enumerate(chains):
            r = rows[j][d]
            tab = (dirs[d][3][r, :], dirs[d][4][r, :])
            qs[j, i] = _rope(dirs[d][0][0, r, sls[h]].astype(F32), *tab)
            ks[j, i] = _rope(dirs[d][1][0, r, sls[h]].astype(F32), *tab) * QK_SCALE
            vs[j, i] = dirs[d][2][0, r, sls[h]]
    for key in qs:
        qk[key] = (_dot_nt(qs[key].astype(BF16), ks[key].astype(BF16)) * dm_ref[key[1]]).astype(BF16)
        kv[key] = _dot((ks[key] * kd_ref[key[1]]).T.astype(BF16), vs[key])
    state = [s_ref[i] for i in range(len(chains))]
    for j in range(n_sub):
        o_inter = [_dot((qs[j, i] * qd_ref[i]).astype(BF16), s.astype(BF16)) for i, s in enumerate(state)]
        for i, (d, h) in enumerate(chains):
            dirs[d][5][0, rows[j][d], sls[h]] = (_dot(qk[j, i], vs[j, i]) + o_inter[i]).astype(dirs[d][5].dtype)
            chunk_decay = jnp.exp(jnp.full((1, HEAD_DIM), float(c), F32) * lg_ref[d, h])
            state[i] = state[i] * chunk_decay + kv[j, i]
    for i, st in enumerate(state):
        s_ref[i] = st


RET_STEP_CHUNKS = 2


def _ret_call(log_gamma, p_main, cos2, sin2, *, l_lat):
    b, l_all, _ = p_main.shape
    c = RET_CHUNK
    blk = RET_STEP_CHUNKS * c
    n_all, n_lat = l_all // blk, l_lat // blk
    n_ctx = n_all - n_lat

    def fwd(s):
        return jnp.where(s < n_ctx, n_lat + s, s - n_ctx)

    def bwd(s):
        return n_all - 1 - s

    def pspec(colblk, order):
        return pl.BlockSpec((1, blk, RET_W), lambda bb, s: (bb, order(s), colblk))

    def tspec(order):
        return pl.BlockSpec((blk, HEAD_DIM), lambda bb, s: (order(s), 0))

    in_specs = [pl.BlockSpec(memory_space=pltpu.SMEM)]
    for order in (fwd, bwd):
        in_specs += [pspec(COL_RET_Q // RET_W, order), pspec(COL_RET_K // RET_W, order),
                     pspec(COL_RET_V // RET_W, order), tspec(order), tspec(order)]
    nhd = 2 * RET_HEADS
    return pl.pallas_call(
        _ret_kernel,
        grid=(b, n_all),
        in_specs=in_specs,
        out_specs=[pl.BlockSpec((1, blk, RET_W), lambda bb, s: (bb, fwd(s), 0)),
                   pl.BlockSpec((1, blk, RET_W), lambda bb, s: (bb, bwd(s), 0))],
        out_shape=[jax.ShapeDtypeStruct((b, l_all, RET_W), BF16)] * 2,
        scratch_shapes=[pltpu.VMEM((nhd, HEAD_DIM, HEAD_DIM), F32), pltpu.VMEM((nhd, c, c), F32),
                        pltpu.VMEM((nhd, c, HEAD_DIM), F32), pltpu.VMEM((nhd, c, HEAD_DIM), F32)],
        compiler_params=_params(("arbitrary", "arbitrary")),
        name="retention_scan",
    )(log_gamma, p_main, p_main, p_main, cos2, sin2, p_main, p_main, p_main, cos2, sin2)


def _attprep_kernel(q_ref, k_ref, v_ref, c_ref, s_ref, qw_ref, kw_ref, qo_ref, ko_ref, vo_ref):
    cos2 = c_ref[...]
    sin2 = s_ref[...]

    def norm_rope(x, w):
        xf = x.astype(F32)
        y = xf * lax.rsqrt(jnp.mean(xf * xf, -1, keepdims=True) + EPS) * w
        return _rope(y, cos2, sin2)

    for h in range(ATT_HEADS):
        sl = slice(h * HEAD_DIM, (h + 1) * HEAD_DIM)
        qo_ref[0, :, sl] = (norm_rope(q_ref[0, :, sl], qw_ref[...]) * (QK_SCALE * LOG2_E)).astype(BF16)
    for h in range(ATT_KV_HEADS):
        sl = slice(h * HEAD_DIM, (h + 1) * HEAD_DIM)
        ko_ref[0, sl, :] = norm_rope(k_ref[0, :, sl], kw_ref[...]).T.astype(BF16)
        vo_ref[0, :, 2 * h * HEAD_DIM:(2 * h + 1) * HEAD_DIM] = v_ref[0, :, sl]
        vo_ref[0, :, (2 * h + 1) * HEAD_DIM:(2 * h + 2) * HEAD_DIM] = jnp.ones((v_ref.shape[1], HEAD_DIM), BF16)


def _attprep_call(p_main, cos2, sin2, qn_w, kn_w, *, tm):
    b, l_all, _ = p_main.shape
    return pl.pallas_call(
        _attprep_kernel,
        grid=(b, l_all // tm),
        in_specs=[pl.BlockSpec((1, tm, ATT_W), lambda bb, i: (bb, i, COL_ATT_Q // ATT_W)),
                  pl.BlockSpec((1, tm, ATT_KV_W), lambda bb, i: (bb, i, COL_ATT_K // ATT_KV_W)),
                  pl.BlockSpec((1, tm, ATT_KV_W), lambda bb, i: (bb, i, COL_ATT_V // ATT_KV_W)),
                  pl.BlockSpec((tm, HEAD_DIM), lambda bb, i: (i, 0)),
                  pl.BlockSpec((tm, HEAD_DIM), lambda bb, i: (i, 0)),
                  pl.BlockSpec((1, HEAD_DIM), lambda bb, i: (0, 0)),
                  pl.BlockSpec((1, HEAD_DIM), lambda bb, i: (0, 0))],
        out_specs=[pl.BlockSpec((1, tm, ATT_W), lambda bb, i: (bb, i, 0)),
                   pl.BlockSpec((1, ATT_KV_W, tm), lambda bb, i: (bb, 0, i)),
                   pl.BlockSpec((1, tm, 2 * ATT_KV_W), lambda bb, i: (bb, i, 0))],
        out_shape=[jax.ShapeDtypeStruct((b, l_all, ATT_W), BF16),
                   jax.ShapeDtypeStruct((b, ATT_KV_W, l_all), BF16),
                   jax.ShapeDtypeStruct((b, l_all, 2 * ATT_KV_W), BF16)],
        compiler_params=_params(("arbitrary", "arbitrary")),
        name="attn_prep",
    )(p_main, p_main, p_main, cos2, sin2, qn_w, kn_w)


def _att_kernel(*refs, tk):
    q_ref, kt_ref, v_ref = refs[0], refs[1], refs[2]
    o_ref, s_ref, mx_ref, m_ref, acc_ref = refs[-5], refs[-4], refs[-3], refs[-2], refs[-1]
    tq = q_ref.shape[1]
    n = kt_ref.shape[2] // tk
    lanes = tk // HEAD_DIM
    rb = 64

    def put_scores(c):
        col = pl.multiple_of(c * tk, tk)
        s_ref[c] = _dot(q_ref[0], kt_ref[0, :, pl.ds(col, tk)])

    def fold_max(c):
        for r in range(tq // rb):
            rows = slice(r * rb, (r + 1) * rb)
            acc = mx_ref[rows, :]
            for j in range(lanes):
                acc = jnp.maximum(acc, s_ref[c, rows, j * HEAD_DIM:(j + 1) * HEAD_DIM])
            mx_ref[rows, :] = acc

    mx_ref[...] = jnp.full(mx_ref.shape, -jnp.inf, F32)
    put_scores(0)

    @pl.loop(1, n)
    def _(c):
        put_scores(c)
        fold_max(c - 1)

    fold_max(n - 1)
    m_ref[...] = jnp.max(mx_ref[...], -1, keepdims=True)

    acc_ref[...] = jnp.zeros_like(acc_ref)

    @pl.loop(0, n)
    def _(c):
        row = pl.multiple_of(c * tk, tk)
        p = jnp.exp2((s_ref[c] - m_ref[...]).astype(BF16))
        acc_ref[...] += _dot(p, v_ref[0, pl.ds(row, tk), :])

    o_ref[0] = (acc_ref[:, :HEAD_DIM] / acc_ref[:, HEAD_DIM:]).astype(BF16)


def _key_chunk(n_keys):
    for tk in (4224, 768, 512, 256):
        if n_keys % tk == 0:
            return tk
    raise ValueError("key count must be a multiple of 256")


def _att_call(qn, kn, v1, *, l_lat, tq):
    b, l_all, _ = qn.shape
    l_ctx = l_all - l_lat
    group = ATT_HEADS // ATT_KV_HEADS
    tk = _key_chunk(l_all)
    y_lat = pl.pallas_call(
        functools.partial(_att_kernel, tk=tk),
        grid=(b, ATT_HEADS, l_lat // tq),
        in_specs=[pl.BlockSpec((1, tq, HEAD_DIM), lambda bb, h, i: (bb, i, h)),
                  pl.BlockSpec((1, HEAD_DIM, l_all), lambda bb, h, i: (bb, h // group, 0),
                               pipeline_mode=pl.Buffered(1)),
                  pl.BlockSpec((1, l_all, 2 * HEAD_DIM), lambda bb, h, i: (bb, 0, h // group),
                               pipeline_mode=pl.Buffered(1))],
        out_specs=pl.BlockSpec((1, tq, HEAD_DIM), lambda bb, h, i: (bb, i, h)),
        out_shape=jax.ShapeDtypeStruct((b, l_all, ATT_W), BF16),
        scratch_shapes=[pltpu.VMEM((l_all // tk, tq, tk), F32), pltpu.VMEM((tq, HEAD_DIM), F32), pltpu.VMEM((tq, 1), F32),
                        pltpu.VMEM((tq, 2 * HEAD_DIM), F32)],
        compiler_params=_params(("arbitrary", "arbitrary", "arbitrary")),
        name="attention",
    )(qn, kn, v1)
    ctx_blk = l_lat // l_ctx
    return pl.pallas_call(
        functools.partial(_att_kernel, tk=l_ctx),
        grid=(b, ATT_HEADS),
        in_specs=[pl.BlockSpec((1, l_ctx, HEAD_DIM), lambda bb, h: (bb, ctx_blk, h)),
                  pl.BlockSpec((1, HEAD_DIM, l_ctx), lambda bb, h: (bb, h // group, ctx_blk)),
                  pl.BlockSpec((1, l_ctx, 2 * HEAD_DIM), lambda bb, h: (bb, ctx_blk, h // group)),
                  pl.BlockSpec(memory_space=pl.ANY)],
        out_specs=pl.BlockSpec((1, l_ctx, HEAD_DIM), lambda bb, h: (bb, ctx_blk, h)),
        out_shape=jax.ShapeDtypeStruct((b, l_all, ATT_W), BF16),
        scratch_shapes=[pltpu.VMEM((1, l_ctx, l_ctx), F32), pltpu.VMEM((l_ctx, HEAD_DIM), F32), pltpu.VMEM((l_ctx, 1), F32),
                        pltpu.VMEM((l_ctx, 2 * HEAD_DIM), F32)],
        input_output_aliases={3: 0},
        compiler_params=_params(("arbitrary", "arbitrary")),
        name="attention_ctx",
    )(qn, kn, v1, y_lat)


def _dnprep_kernel(*refs, tm, l_lat, l_all):
    mains, prevs, nexts = refs[0:3], refs[3:6], refs[6:9]
    cw_ref, pab_ref, alog_ref, dtb_ref = refs[9:13]
    outs = refs[13:16]
    g_ref, ext_ref = refs[16], refs[17]
    i = pl.program_id(1)
    halo = 16
    pad = DN_CONV_K // 2
    first = jnp.logical_or(i == 0, i == l_lat // tm)
    last = jnp.logical_or(i == l_lat // tm - 1, i == l_all // tm - 1)

    keep_prev = jnp.where(first, 0.0, 1.0)
    keep_next = jnp.where(last, 0.0, 1.0)

    for part in range(3):
        ext_ref[0:halo, :] = prevs[part][0].astype(F32) * keep_prev
        ext_ref[halo:halo + tm, :] = mains[part][0].astype(F32)
        ext_ref[halo + tm:, :] = nexts[part][0].astype(F32) * keep_next
        acc = jnp.zeros((tm, DN_W), F32)
        for j in range(DN_CONV_K):
            w_j = cw_ref[j:j + 1, part * DN_W:(part + 1) * DN_W]
            acc = acc + w_j * ext_ref[pl.ds(halo - pad + j, tm), :]
        y = _silu(acc)
        for h in range(DN_HEADS):
            sl = slice(h * HEAD_DIM, (h + 1) * HEAD_DIM)
            yh = y[:, sl]
            if part < 2:
                yh = yh * lax.rsqrt(jnp.sum(yh * yh, -1, keepdims=True) + EPS)
            if part == 0:
                yh = yh * QK_SCALE
            outs[part][0, :, sl] = yh.astype(BF16)

    a = pab_ref[0]
    col = lax.broadcasted_iota(jnp.int32, a.shape, 1)
    z = a + dtb_ref[...]
    softplus = jnp.maximum(z, 0.0) + jnp.log(1.0 + jnp.exp(-jnp.abs(z)))
    g = -jnp.exp(alog_ref[...]) * softplus
    ii = lax.broadcasted_iota(jnp.int32, (tm, tm), 0)
    rr = lax.broadcasted_iota(jnp.int32, (tm, tm), 1)
    same_chunk = (ii // DN_CHUNK) == (rr // DN_CHUNK)
    prefix = jnp.logical_and(same_chunk, rr <= ii).astype(F32)
    suffix = jnp.logical_and(same_chunk, rr >= ii).astype(F32)
    hi = lax.Precision.HIGHEST
    cum_f = jnp.dot(prefix, g, preferred_element_type=F32, precision=hi)
    cum_b = jnp.dot(suffix, g, preferred_element_type=F32, precision=hi)
    g_ref[0] = jnp.where(col < DN_HEADS, cum_f, jnp.where(col < N_DN_GATES, cum_b, jax.nn.sigmoid(a)))


def _dnprep_call(p_main, p_ab, conv_w, alog_row, dtb_row, *, l_lat, tm):
    b, l_all, _ = p_main.shape
    halo = 16
    r = tm // halo
    cb = COL_DN_QKV // DN_W
    in_specs = []
    for part in range(3):
        in_specs.append(pl.BlockSpec((1, tm, DN_W), lambda bb, i, p=part: (bb, i, cb + p)))
    for part in range(3):
        in_specs.append(pl.BlockSpec((1, halo, DN_W), lambda bb, i, p=part: (bb, jnp.maximum(i * r - 1, 0), cb + p)))
    for part in range(3):
        in_specs.append(pl.BlockSpec(
            (1, halo, DN_W), lambda bb, i, p=part: (bb, jnp.minimum((i + 1) * r, l_all // halo - 1), cb + p)))
    in_specs += [pl.BlockSpec((DN_CONV_K, 3 * DN_W), lambda bb, i: (0, 0)),
                 pl.BlockSpec((1, tm, GATE_COLS), lambda bb, i: (bb, i, 0)),
                 pl.BlockSpec((1, GATE_COLS), lambda bb, i: (0, 0)),
                 pl.BlockSpec((1, GATE_COLS), lambda bb, i: (0, 0))]
    kern = functools.partial(_dnprep_kernel, tm=tm, l_lat=l_lat, l_all=l_all)
    return pl.pallas_call(
        kern,
        grid=(b, l_all // tm),
        in_specs=in_specs,
        out_specs=[pl.BlockSpec((1, tm, DN_W), lambda bb, i: (bb, i, 0))] * 3
        + [pl.BlockSpec((1, tm, GATE_COLS), lambda bb, i: (bb, i, 0))],
        out_shape=[jax.ShapeDtypeStruct((b, l_all, DN_W), BF16)] * 3
        + [jax.ShapeDtypeStruct((b, l_all, GATE_COLS), F32)],
        scratch_shapes=[pltpu.VMEM((tm + 2 * halo, DN_W), F32)],
        compiler_params=_params(("arbitrary", "arbitrary")),
        name="deltanet_conv_gates",
    )(*([p_main] * 9), conv_w, p_ab, alog_row, dtb_row)


def _neumann_inverse_many(a_list, eye):
    ps = [-a for a in a_list]
    ts = [eye + p for p in ps]
    for _ in range(5):
        pbs = [p.astype(BF16) for p in ps]
        ps = [_dot(pb, pb) for pb in pbs]
        ts = [t + _dot(t.astype(BF16), p.astype(BF16)) for t, p in zip(ts, ps)]
    return ts


def _dnchunk_kernel(q_ref, k_ref, v_ref, g_ref, gt_ref, w_ref, kc_ref, qg_ref, kgt_ref, qk_ref, gl_ref):
    c = DN_CHUNK
    n_sub = q_ref.shape[1] // c
    ii = lax.broadcasted_iota(jnp.int32, (c, c), 0)
    jj = lax.broadcasted_iota(jnp.int32, (c, c), 1)
    eye = (ii == jj).astype(F32)

    qk_ref[...] = jnp.zeros_like(qk_ref)
    kgt_ref[...] = jnp.zeros_like(kgt_ref)
    gl_ref[...] = jnp.zeros_like(gl_ref)

    heads = range(DN_HEADS)
    subs = range(n_sub)
    chains = [(u, d, h) for u in subs for h in heads for d in range(2)]
    rows = [slice(u * c, (u + 1) * c) for u in subs]
    sls = [slice(h * HEAD_DIM, (h + 1) * HEAD_DIM) for h in heads]
    gcol = [g_ref[0, rows[u], :] for u in subs]
    grow = [gt_ref[0, u] for u in subs]
    q16 = {(u, h): q_ref[0, rows[u], sls[h]] for u in subs for h in heads}
    k16 = {(u, h): k_ref[0, rows[u], sls[h]] for u in subs for h in heads}
    kk = {uh: _dot_nt(k16[uh], k16[uh]) for uh in k16}
    qk0 = {uh: _dot_nt(q16[uh], k16[uh]) for uh in k16}

    gcc, tot, beta, decay, a_list = {}, {}, {}, {}, []
    for u, d, h in chains:
        col = d * DN_HEADS + h
        key = (u, d, h)
        gcc[key] = gcol[u][:, col:col + 1]
        gcr = grow[u][col:col + 1, :]
        tot[key] = gcol[u][c - 1:c, col:col + 1] if d == 0 else gcol[u][0:1, col:col + 1]
        beta[key] = gcol[u][:, N_DN_GATES + col:N_DN_GATES + col + 1]
        incl = (ii >= jj) if d == 0 else (ii <= jj)
        strict = (ii > jj) if d == 0 else (ii < jj)
        decay[key] = jnp.where(incl, jnp.exp(jnp.where(incl, gcc[key] - gcr, 0.0)), 0.0)
        a_list.append(jnp.where(strict, kk[u, h] * beta[key] * decay[key], 0.0))
        gl_ref[0, d, u, h:h + 1, :] = jnp.broadcast_to(jnp.exp(tot[key]), (1, HEAD_DIM))

    t16 = [t.astype(BF16) for t in _neumann_inverse_many(a_list, eye)]

    for key, t in zip(chains, t16):
        u, d, h = key
        sl = sls[h]
        k = k16[u, h].astype(F32)
        e_col = jnp.exp(gcc[key])
        rhs = jnp.concatenate([(v_ref[0, rows[u], sl].astype(F32) * beta[key]).astype(BF16),
                               (k * (beta[key] * e_col)).astype(BF16)], axis=1)
        wk = _dot(t, rhs)
        w_ref[0, d, rows[u], sl] = wk[:, :HEAD_DIM]
        kc_ref[0, d, rows[u], sl] = wk[:, HEAD_DIM:].astype(BF16)
        qg_ref[0, d, rows[u], sl] = (q16[u, h].astype(F32) * e_col).astype(BF16)
        kg = k * jnp.exp(tot[key] - gcc[key])
        kgt_ref[0, d, u, :, pl.ds(h * HEAD_DIM, c)] = kg.T.astype(BF16)
        qk_ref[0, d, rows[u], pl.ds(h * HEAD_DIM, c)] = (qk0[u, h] * decay[key]).astype(BF16)


DN_STEP_CHUNKS = 4


def _dnchunk_call(qd, kd, vd, g, gt):
    b, l_all, _ = qd.shape
    c = DN_CHUNK
    n = l_all // c
    ns = DN_STEP_CHUNKS
    tok = lambda dt: jax.ShapeDtypeStruct((b, 2, l_all, DN_W), dt)
    tok_spec = pl.BlockSpec((1, 2, ns * c, DN_W), lambda bb, s: (bb, 0, s, 0))
    return pl.pallas_call(
        _dnchunk_kernel,
        grid=(b, n // ns),
        in_specs=[pl.BlockSpec((1, ns * c, DN_W), lambda bb, s: (bb, s, 0))] * 3
        + [pl.BlockSpec((1, ns * c, GATE_COLS), lambda bb, s: (bb, s, 0)),
           pl.BlockSpec((1, ns, 2 * N_DN_GATES, c), lambda bb, s: (bb, s, 0, 0))],
        out_specs=[tok_spec, tok_spec, tok_spec,
                   pl.BlockSpec((1, 2, ns, HEAD_DIM, DN_W), lambda bb, s: (bb, 0, s, 0, 0)),
                   tok_spec,
                   pl.BlockSpec((1, 2, ns, 8, HEAD_DIM), lambda bb, s: (bb, 0, s, 0, 0))],
        out_shape=[tok(F32), tok(BF16), tok(BF16),
                   jax.ShapeDtypeStruct((b, 2, n, HEAD_DIM, DN_W), BF16),
                   tok(BF16),
                   jax.ShapeDtypeStruct((b, 2, n, 8, HEAD_DIM), F32)],
        compiler_params=_params(("arbitrary", "arbitrary")),
        name="deltanet_chunk_factors",
    )(qd, kd, vd, g, gt)


def _dnscan_kernel(*refs):
    ins_f, ins_b = refs[0:6], refs[6:12]
    of_ref, ob_ref, s_ref = refs[12], refs[13], refs[14]
    c = DN_CHUNK
    n_sub = of_ref.shape[1] // c
    step = pl.program_id(1)

    @pl.when(step == 0)
    def _():
        s_ref[...] = jnp.zeros_like(s_ref)

    ins, outs = (ins_f, ins_b), (of_ref, ob_ref)
    chains = [(d, h) for d in range(2) for h in range(DN_HEADS)]
    sls = [slice(h * HEAD_DIM, (h + 1) * HEAD_DIM) for h in range(DN_HEADS)]
    state = [s_ref[d * DN_HEADS + h] for d, h in chains]
    for j in range(n_sub):
        sub = [j, n_sub - 1 - j]
        rows = [slice(u * c, (u + 1) * c) for u in sub]
        s16 = [s.astype(BF16) for s in state]
        v16 = [(ins[d][0][0, 0, rows[d], sls[h]] - _dot(ins[d][1][0, 0, rows[d], sls[h]], s)).astype(BF16)
               for (d, h), s in zip(chains, s16)]
        o_inter = [_dot(ins[d][2][0, 0, rows[d], sls[h]], s) for (d, h), s in zip(chains, s16)]
        new_state = []
        for (d, h), s, v, oi in zip(chains, state, v16, o_inter):
            qk = ins[d][4][0, 0, rows[d], pl.ds(h * HEAD_DIM, c)]
            outs[d][0, rows[d], sls[h]] = (oi + _dot(qk, v)).astype(outs[d].dtype)
            kgt = ins[d][3][0, 0, sub[d], :, pl.ds(h * HEAD_DIM, c)]
            new_state.append(s * ins[d][5][0, 0, sub[d], h:h + 1, :] + _dot(kgt, v))
        state = new_state
    for i, s in enumerate(state):
        s_ref[i] = s


def _dnscan_call(w, kc, qg, kgt, qk, gl, *, l_lat):
    b, _, l_all, _ = w.shape
    ns = DN_STEP_CHUNKS
    blk = ns * DN_CHUNK
    n_all, n_lat = l_all // blk, l_lat // blk
    n_ctx = n_all - n_lat

    def fwd(s):
        return jnp.where(s < n_ctx, n_lat + s, s - n_ctx)

    def bwd(s):
        return n_all - 1 - s

    in_specs = []
    for d, order in enumerate((fwd, bwd)):
        tok_spec = pl.BlockSpec((1, 1, blk, DN_W), lambda bb, s, d=d, o=order: (bb, d, o(s), 0))
        in_specs += [tok_spec, tok_spec, tok_spec,
                     pl.BlockSpec((1, 1, ns, HEAD_DIM, DN_W), lambda bb, s, d=d, o=order: (bb, d, o(s), 0, 0)),
                     tok_spec,
                     pl.BlockSpec((1, 1, ns, 8, HEAD_DIM), lambda bb, s, d=d, o=order: (bb, d, o(s), 0, 0))]
    return pl.pallas_call(
        _dnscan_kernel,
        grid=(b, n_all),
        in_specs=in_specs,
        out_specs=[pl.BlockSpec((1, blk, DN_W), lambda bb, s: (bb, fwd(s), 0)),
                   pl.BlockSpec((1, blk, DN_W), lambda bb, s: (bb, bwd(s), 0))],
        out_shape=[jax.ShapeDtypeStruct((b, l_all, DN_W), BF16)] * 2,
        scratch_shapes=[pltpu.VMEM((2 * DN_HEADS, HEAD_DIM, HEAD_DIM), F32)],
        compiler_params=_params(("arbitrary", "arbitrary")),
        name="deltanet_scan",
    )(w, kc, qg, kgt, qk, gl, w, kc, qg, kgt, qk, gl)


def _rope_tables(l_lat, l_ctx):
    rows = l_lat // GRID_W
    n_freq = HEAD_DIM // 4
    inv = ROPE_THETA ** (-jnp.arange(n_freq, dtype=F32) / n_freq)
    row_ang = jnp.arange(rows, dtype=F32)[:, None] * inv
    col_ang = jnp.arange(GRID_W, dtype=F32)[:, None] * inv

    def table(fn):
        return jnp.concatenate([jnp.repeat(fn(row_ang), GRID_W, axis=0), jnp.tile(fn(col_ang), (rows, 1))], -1)

    cos, sin = table(jnp.cos), table(jnp.sin)
    cos2 = jnp.concatenate([cos, cos], -1)
    sin2 = jnp.concatenate([-sin, sin], -1)
    cos2 = jnp.concatenate([cos2, jnp.ones((l_ctx, HEAD_DIM), F32)], 0)
    sin2 = jnp.concatenate([sin2, jnp.zeros((l_ctx, HEAD_DIM), F32)], 0)
    return cos2, sin2


def _pad_row(v, width):
    v = v.reshape(1, -1).astype(F32)
    return jnp.pad(v, ((0, 0), (0, width - v.shape[1])))


def _row_tile(l_lat, l_ctx):
    l_all = l_lat + l_ctx
    for tm in (768, 512, 256, 128):
        if l_all % tm == 0:
            return tm
    raise ValueError("token count must be a multiple of 128")


def kernel(x, c, ctx, c_ctx, w_ada, b_ada, w_in, ret_decay_logit, dn_conv_w, dn_a_log, dn_dt_bias, dn_norm_w,
           att_qn_w, att_kn_w, w_o, ln1_w, ln1_b, w_ffn_in, w_ffn_out, ln2_w, ln2_b):
    bsz, l_lat, d = x.shape
    l_ctx = ctx.shape[1]
    l_all = l_lat + l_ctx
    depth = w_ada.shape[0]
    assert l_lat % 256 == 0 and l_ctx % 256 == 0 and l_lat % GRID_W == 0
    tm = _row_tile(l_lat, l_ctx)
    tn = MAIN_W // 2

    cos2, sin2 = _rope_tables(l_lat, l_ctx)

    cond_rows = 8 * pl.cdiv(bsz + 1, 8)
    cond = jnp.concatenate([c, c_ctx[None, :]], 0)
    cond = jnp.pad(cond, ((0, cond_rows - bsz - 1), (0, 0)))
    mod = _ada_call(cond, w_ada, b_ada)

    w_main, w_ab = _repack_call(w_in)
    for i in range(depth):
        last = i == depth - 1
        ml = mod[i, :bsz].reshape(bsz, 6, d)
        mc = mod[i, bsz].reshape(1, 6, d)
        if i == 0:
            p_main, p_ab, x_all = _inproj_call((x, ctx), ml, mc, w_main, w_ab, layer=i, l_lat=l_lat, tm=tm, tn=tn // 2)
        else:
            p_main, p_ab = _inproj_call(x_all, ml, mc, w_main, w_ab, layer=i, l_lat=l_lat, tm=tm, tn=tn)

        log_gamma = jax.nn.log_sigmoid(ret_decay_logit[i].astype(F32))
        ret_f, ret_b = _ret_call(log_gamma, p_main, cos2, sin2, l_lat=l_lat)

        qn, kn, v1 = _attprep_call(p_main, cos2, sin2, att_qn_w[i].reshape(1, -1), att_kn_w[i].reshape(1, -1),
                                   tm=tm)
        y_att = _att_call(qn, kn, v1, l_lat=l_lat, tq=1024 if l_lat % 1024 == 0 else 256)

        qd, kd, vd, g = _dnprep_call(p_main, p_ab, dn_conv_w[i], _pad_row(dn_a_log[i], GATE_COLS),
                                     _pad_row(dn_dt_bias[i], GATE_COLS), l_lat=l_lat, tm=256)
        gt = g[:, :, :2 * N_DN_GATES].reshape(bsz, l_all // DN_CHUNK, DN_CHUNK, 2 * N_DN_GATES)
        gt = jnp.swapaxes(gt, 2, 3)
        dn_f, dn_b = _dnscan_call(*_dnchunk_call(qd, kd, vd, g, gt), l_lat=l_lat)

        x_all = _wo_call(ret_f, ret_b, dn_f, dn_b, p_main, dn_norm_w[i].reshape(1, -1), y_att, x_all, ml, mc,
                         w_o[i].astype(BF16), ln1_w[i].reshape(1, -1), ln1_b[i].reshape(1, -1), l_lat=l_lat, tm=tm)
        x_all = _ffn_call(x_all, ml, mc, w_ffn_in[i].astype(BF16), w_ffn_out[i].astype(BF16),
                          ln2_w[i].reshape(1, -1), ln2_b[i].reshape(1, -1), l_lat=l_lat,
                          out_rows=l_lat if last else l_all, tm=tm, tf=512)
    return x_all
```

```python
import functools

import jax
import jax.numpy as jnp
from jax import lax
from jax.experimental import pallas as pl
from jax.experimental.pallas import tpu as pltpu

F32 = jnp.float32
BF16 = jnp.bfloat16

HEAD_DIM = 128
RET_HEADS = 4
DN_HEADS = 4
ATT_HEADS = 8
ATT_KV_HEADS = 2
RET_W = RET_HEADS * HEAD_DIM
DN_W = DN_HEADS * HEAD_DIM
ATT_W = ATT_HEADS * HEAD_DIM
ATT_KV_W = ATT_KV_HEADS * HEAD_DIM
RET_CHUNK = 128
DN_CHUNK = 64
DN_CONV_K = 5
GRID_W = 64
ROPE_THETA = 10000.0
MODEL_DEPTH = 4
DEEPNORM_ALPHA = (2 * MODEL_DEPTH) ** 0.25
EPS = 1e-6
QK_SCALE = HEAD_DIM ** -0.5
LOG2_E = 1.4426950408889634

COL_RET_Q = 0
COL_RET_K = RET_W
COL_RET_V = 2 * RET_W
COL_RET_G = 3 * RET_W
COL_DN_QKV = 4 * RET_W
COL_DN_Z = COL_DN_QKV + 3 * DN_W
COL_ATT_Q = COL_DN_Z + DN_W
COL_ATT_K = COL_ATT_Q + ATT_W
COL_ATT_V = COL_ATT_K + ATT_KV_W
MAIN_W = COL_ATT_V + ATT_KV_W
GATE_COLS = 128
N_DN_GATES = 2 * DN_HEADS

VMEM_LIMIT_MB = 56


def _params(sem, vmem_mb=VMEM_LIMIT_MB):
    return pltpu.CompilerParams(dimension_semantics=sem, vmem_limit_bytes=vmem_mb * 1024 * 1024)


def _dot(a, b):
    return jnp.dot(a, b, preferred_element_type=F32)


def _dot_nt(a, b):
    return lax.dot_general(a, b, (((1,), (1,)), ((), ())), preferred_element_type=F32)


def _silu(x):
    return x * jax.nn.sigmoid(x)


def _is_ctx_rows(i, tm, l_lat):
    rows = i * tm + lax.broadcasted_iota(jnp.int32, (tm, 1), 0)
    return rows >= l_lat


def _mod_row(is_ctx, ml_ref, mc_ref, k):
    return jnp.where(is_ctx, mc_ref[0, k:k + 1, :], ml_ref[0, k:k + 1, :])


def _layer_norm(r, w, b):
    mu = jnp.mean(r, -1, keepdims=True)
    rc = r - mu
    var = jnp.mean(rc * rc, -1, keepdims=True)
    return rc * lax.rsqrt(var + EPS) * w + b


def _split_bf16(a):
    hi = a.astype(BF16)
    return hi, (a - hi.astype(F32)).astype(BF16)


def _ada_kernel(c_ref, w_ref, b_ref, o_ref):
    h_hi, h_lo = _split_bf16(_silu(c_ref[...]))
    w_hi, w_lo = _split_bf16(w_ref[0])
    o_ref[0] = _dot(h_hi, w_hi) + _dot(h_hi, w_lo) + _dot(h_lo, w_hi) + b_ref[0]


def _ada_call(cond, w_ada, b_ada):
    depth, d, n6 = w_ada.shape
    rows = cond.shape[0]
    tn = 1024
    return pl.pallas_call(
        _ada_kernel,
        grid=(depth, n6 // tn),
        in_specs=[pl.BlockSpec((rows, d), lambda l, j: (0, 0)),
                  pl.BlockSpec((1, d, tn), lambda l, j: (l, 0, j)),
                  pl.BlockSpec((1, 1, tn), lambda l, j: (l, 0, j))],
        out_specs=pl.BlockSpec((1, rows, tn), lambda l, j: (l, 0, j)),
        out_shape=jax.ShapeDtypeStruct((depth, rows, n6), F32),
        compiler_params=_params(("arbitrary", "arbitrary")),
        name="ada_mod",
    )(cond, w_ada, b_ada.reshape(depth, 1, n6))


GATE_LO = COL_DN_Z + DN_W


def _repack_kernel(w_ref, wm_ref, wab_ref):
    wm_ref[0, :, :GATE_LO] = w_ref[0, :, :GATE_LO].astype(BF16)
    wm_ref[0, :, GATE_LO:] = w_ref[0, :, GATE_LO + 2 * N_DN_GATES:].astype(BF16)
    gates = w_ref[0, :, GATE_LO:GATE_LO + GATE_COLS]
    lane = lax.broadcasted_iota(jnp.int32, gates.shape, 1)
    wab_ref[0] = jnp.where(lane < 2 * N_DN_GATES, gates, jnp.zeros_like(gates)).astype(BF16)


def _repack_call(w_in):
    depth, d, proj_w = w_in.shape
    assert proj_w == MAIN_W + 2 * N_DN_GATES
    tr = 256
    return pl.pallas_call(
        _repack_kernel,
        grid=(depth, d // tr),
        in_specs=[pl.BlockSpec((1, tr, proj_w), lambda l, r: (l, r, 0))],
        out_specs=[pl.BlockSpec((1, tr, MAIN_W), lambda l, r: (l, r, 0)),
                   pl.BlockSpec((1, tr, GATE_COLS), lambda l, r: (l, r, 0))],
        out_shape=[jax.ShapeDtypeStruct((depth, d, MAIN_W), BF16), jax.ShapeDtypeStruct((depth, d, GATE_COLS), BF16)],
        compiler_params=_params(("arbitrary", "arbitrary")),
        name="repack_w_in",
    )(w_in)


def _inproj_kernel(*refs, tm, l_lat, split_input):
    if split_input:
        x_ref, c_ref, ml_ref, mc_ref, w_ref, wab_ref, pm_ref, pab_ref, xo_ref, h_ref = refs
    else:
        x_ref, ml_ref, mc_ref, w_ref, wab_ref, pm_ref, pab_ref, h_ref = refs
    i = pl.program_id(1)
    j = pl.program_id(2)

    @pl.when(j == 0)
    def _():
        if split_input:
            n_full, rem = divmod(l_lat, tm)

            @pl.when(i < n_full)
            def _():
                xo_ref[0] = x_ref[0]

            @pl.when(i >= n_full)
            def _():
                if rem:
                    xo_ref[0, :rem, :] = x_ref[0, :rem, :]
                xo_ref[0, rem:, :] = c_ref[0]

            x_tile = xo_ref[0]
        else:
            x_tile = x_ref[0]
        is_ctx = _is_ctx_rows(i, tm, l_lat)
        shift = _mod_row(is_ctx, ml_ref, mc_ref, 0)
        scale = _mod_row(is_ctx, ml_ref, mc_ref, 1)
        h = (x_tile * (1.0 + scale) + shift).astype(BF16)
        h_ref[...] = h
        pab_ref[0] = _dot(h, wab_ref[...])

    tn = w_ref.shape[1]
    step = 512
    for lo in range(0, tn, step):
        hi = min(lo + step, tn)
        pm_ref[0, :, lo:hi] = _dot(h_ref[...], w_ref[:, lo:hi]).astype(BF16)


def _inproj_call(xs, ml, mc, w_main, w_ab, *, layer, l_lat, tm, tn):
    split_input = isinstance(xs, tuple)
    xs = list(xs) if split_input else [xs]
    b, _, d = xs[0].shape
    l_all = l_lat + xs[1].shape[1] if split_input else xs[0].shape[1]
    x_spec = pl.BlockSpec((1, tm, d), lambda bb, i, j: (bb, i, 0))
    in_specs = [x_spec]
    out_specs = [pl.BlockSpec((1, tm, tn), lambda bb, i, j: (bb, i, j)),
                 pl.BlockSpec((1, tm, GATE_COLS), lambda bb, i, j: (bb, i, 0))]
    out_shape = [jax.ShapeDtypeStruct((b, l_all, MAIN_W), BF16), jax.ShapeDtypeStruct((b, l_all, GATE_COLS), F32)]
    if split_input:
        l_ctx = l_all - l_lat
        assert l_lat % tm + l_ctx == tm, "the context must exactly fill the last row tile"
        in_specs.append(pl.BlockSpec((1, l_ctx, d), lambda bb, i, j: (bb, 0, 0)))
        out_specs.append(x_spec)
        out_shape.append(jax.ShapeDtypeStruct((b, l_all, d), F32))
    in_specs +=[pl.BlockSpec((1, 6, d), lambda bb, i, j: (bb, 0, 0)),
                 pl.BlockSpec((1, 6, d), lambda bb, i, j: (0, 0, 0)),
                 pl.BlockSpec((None, d, tn), lambda bb, i, j: (layer, 0, j)),
                 pl.BlockSpec((None, d, GATE_COLS), lambda bb, i, j: (layer, 0, 0))]
    return pl.pallas_call(
        functools.partial(_inproj_kernel, tm=tm, l_lat=l_lat, split_input=split_input),
        grid=(b, l_all // tm, MAIN_W // tn),
        in_specs=in_specs,
        out_specs=out_specs,
        out_shape=out_shape,
        scratch_shapes=[pltpu.VMEM((tm, d), BF16)],
        compiler_params=_params(("arbitrary", "arbitrary", "arbitrary")),
        name="in_proj",
    )(*xs, ml, mc, w_main, w_ab)


WO_SUB_ROWS = 256


def _wo_kernel(rf_ref, rb_ref, rg_ref, df_ref, db_ref, dz_ref, nw_ref, a_ref, x_ref, ml_ref, mc_ref, w_ref,
               lnw_ref, lnb_ref, o_ref, *, tm, l_lat):
    i = pl.program_id(1)
    sub = WO_SUB_ROWS
    n_rd = RET_W + DN_W

    def head_norm(o):
        return o * lax.rsqrt(jnp.mean(o * o, -1, keepdims=True) + EPS)

    for lo in range(0, tm, sub):
        rows = slice(lo, lo + sub)
        ys = []
        for h in range(RET_HEADS):
            sl = slice(h * HEAD_DIM, (h + 1) * HEAD_DIM)
            o = rf_ref[0, rows, sl].astype(F32) + rb_ref[0, rows, sl].astype(F32)
            ys.append((head_norm(o) * _silu(rg_ref[0, rows, sl].astype(F32))).astype(BF16))
        for h in range(DN_HEADS):
            sl = slice(h * HEAD_DIM, (h + 1) * HEAD_DIM)
            o = df_ref[0, rows, sl].astype(F32) + db_ref[0, rows, sl].astype(F32)
            ys.append((head_norm(o) * nw_ref[...] * _silu(dz_ref[0, rows, sl].astype(F32))).astype(BF16))
        y_rd = jnp.concatenate(ys, axis=1)
        acc = _dot(y_rd, w_ref[0:n_rd, :]) + _dot(a_ref[0, rows, :], w_ref[n_rd:, :])
        rid = i * tm + lo + lax.broadcasted_iota(jnp.int32, (sub, 1), 0)
        gate = _mod_row(rid >= l_lat, ml_ref, mc_ref, 2)
        r = DEEPNORM_ALPHA * x_ref[0, rows, :] + gate * acc
        o_ref[0, rows, :] = _layer_norm(r, lnw_ref[...], lnb_ref[...])


def _wo_call(ret_f, ret_b, dn_f, dn_b, p_main, dn_norm_w, y_att, x_all, ml, mc, w_o, ln_w, ln_b, *, layer, l_lat,
             tm):
    b, l_all, d = x_all.shape
    kern = functools.partial(_wo_kernel, tm=tm, l_lat=l_lat)
    o_spec = pl.BlockSpec((1, tm, RET_W), lambda bb, i: (bb, i, 0))
    return pl.pallas_call(
        kern,
        grid=(b, l_all // tm),
        in_specs=[o_spec, o_spec,
                  pl.BlockSpec((1, tm, RET_W), lambda bb, i: (bb, i, COL_RET_G // RET_W)),
                  o_spec, o_spec,
                  pl.BlockSpec((1, tm, DN_W), lambda bb, i: (bb, i, COL_DN_Z // DN_W)),
                  pl.BlockSpec((1, HEAD_DIM), lambda bb, i: (0, 0)),
                  pl.BlockSpec((1, tm, ATT_W), lambda bb, i: (bb, i, 0)),
                  pl.BlockSpec((1, tm, d), lambda bb, i: (bb, i, 0)),
                  pl.BlockSpec((1, 6, d), lambda bb, i: (bb, 0, 0)),
                  pl.BlockSpec((1, 6, d), lambda bb, i: (0, 0, 0)),
                  pl.BlockSpec((None, RET_W + DN_W + ATT_W, d), lambda bb, i: (layer, 0, 0),
                               pipeline_mode=pl.Buffered(1)),
                  pl.BlockSpec((1, d), lambda bb, i: (0, 0)),
                  pl.BlockSpec((1, d), lambda bb, i: (0, 0))],
        out_specs=pl.BlockSpec((1, tm, d), lambda bb, i: (bb, i, 0)),
        out_shape=jax.ShapeDtypeStruct((b, l_all, d), F32),
        compiler_params=_params(("arbitrary", "arbitrary")),
        name="w_o_postnorm",
    )(ret_f, ret_b, p_main, dn_f, dn_b, p_main, dn_norm_w, y_att, x_all, ml, mc, w_o, ln_w, ln_b)


def _ffn_kernel(x_ref, ml_ref, mc_ref, wg_ref, wu_ref, wo_ref, lnw_ref, lnb_ref, o_ref, h_ref, acc_ref,
                *, tm, l_lat):
    i = pl.program_id(1)
    f = pl.program_id(2)

    @pl.when(f == 0)
    def _():
        is_ctx = _is_ctx_rows(i, tm, l_lat)
        shift = _mod_row(is_ctx, ml_ref, mc_ref, 3)
        scale = _mod_row(is_ctx, ml_ref, mc_ref, 4)
        h_ref[...] = (x_ref[0] * (1.0 + scale) + shift).astype(BF16)
        acc_ref[...] = jnp.zeros_like(acc_ref)

    h = h_ref[...]
    g = _dot(h, wg_ref[...])
    u = _dot(h, wu_ref[...])
    acc_ref[...] += _dot((_silu(g) * u).astype(BF16), wo_ref[...])

    @pl.when(f == pl.num_programs(2) - 1)
    def _():
        is_ctx = _is_ctx_rows(i, tm, l_lat)
        gate = _mod_row(is_ctx, ml_ref, mc_ref, 5)
        r = DEEPNORM_ALPHA * x_ref[0] + gate * acc_ref[...]
        o_ref[0] = _layer_norm(r, lnw_ref[...], lnb_ref[...])


def _ffn_call(x_all, ml, mc, w_in, w_out, ln_w, ln_b, *, layer, l_lat, out_rows, tm, tf):
    b, _, d = x_all.shape
    d_ff = w_out.shape[1]
    nf = d_ff // tf
    kern = functools.partial(_ffn_kernel, tm=tm, l_lat=l_lat)
    return pl.pallas_call(
        kern,
        grid=(b, pl.cdiv(out_rows, tm), nf),
        in_specs=[pl.BlockSpec((1, tm, d), lambda bb, i, f: (bb, i, 0)),
                  pl.BlockSpec((1, 6, d), lambda bb, i, f: (bb, 0, 0)),
                  pl.BlockSpec((1, 6, d), lambda bb, i, f: (0, 0, 0)),
                  pl.BlockSpec((None, d, tf), lambda bb, i, f: (layer, 0, f)),
                  pl.BlockSpec((None, d, tf), lambda bb, i, f: (layer, 0, f + nf)),
                  pl.BlockSpec((None, tf, d), lambda bb, i, f: (layer, f, 0)),
                  pl.BlockSpec((1, d), lambda bb, i, f: (0, 0)),
                  pl.BlockSpec((1, d), lambda bb, i, f: (0, 0))],
        out_specs=pl.BlockSpec((1, tm, d), lambda bb, i, f: (bb, i, 0)),
        out_shape=jax.ShapeDtypeStruct((b, out_rows, d), F32),
        scratch_shapes=[pltpu.VMEM((tm, d), BF16), pltpu.VMEM((tm, d), F32)],
        compiler_params=_params(("arbitrary", "arbitrary", "arbitrary")),
        name="ffn_postnorm",
    )(x_all, ml, mc, w_in, w_in, w_out, ln_w, ln_b)


def _rope(x, cos2, sin2):
    return x * cos2 + pltpu.roll(x, HEAD_DIM // 2, 1) * sin2


def _ret_kernel(lg_ref, qf_ref, kf_ref, vf_ref, cf_ref, sf_ref, qb_ref, kb_ref, vb_ref, cb_ref, sb_ref,
                of_ref, ob_ref, s_ref, dm_ref, qd_ref, kd_ref):
    step = pl.program_id(1)
    c = RET_CHUNK

    @pl.when(step == 0)
    def _():
        s_ref[...] = jnp.zeros_like(s_ref)
        ii = lax.broadcasted_iota(jnp.int32, (c, c), 0).astype(F32)
        jj = lax.broadcasted_iota(jnp.int32, (c, c), 1).astype(F32)
        for d in range(2):
            for h in range(RET_HEADS):
                lg = lg_ref[d, h]
                if d == 0:
                    rel, qe, ke = ii - jj, ii + 1.0, (c - 1.0) - ii
                else:
                    rel, qe, ke = jj - ii, c - ii, ii
                idx = d * RET_HEADS + h
                dm_ref[idx] = jnp.where(rel >= 0, jnp.exp(jnp.maximum(rel, 0.0) * lg), 0.0)
                qd_ref[idx] = jnp.exp(qe * lg)
                kd_ref[idx] = jnp.exp(ke * lg)

    dirs = ((qf_ref, kf_ref, vf_ref, cf_ref, sf_ref, of_ref), (qb_ref, kb_ref, vb_ref, cb_ref, sb_ref, ob_ref))
    n_sub = of_ref.shape[1] // c
    chains = [(d, h) for d in range(2) for h in range(RET_HEADS)]
    sls = [slice(h * HEAD_DIM, (h + 1) * HEAD_DIM) for h in range(RET_HEADS)]
    order = [(j, n_sub - 1 - j) for j in range(n_sub)]
    rows = [[slice(u * c, (u + 1) * c) for u in sub] for sub in order]
    qs, ks, vs, qk, kv = {}, {}, {}, {}, {}
    for j in range(n_sub):
        for i, (d, h) in enumerate(chains):
            r = rows[j][d]
            tab = (dirs[d][3][r, :], dirs[d][4][r, :])
            qs[j, i] = _rope(dirs[d][0][0, r, sls[h]].astype(F32), *tab)
            ks[j, i] = _rope(dirs[d][1][0, r, sls[h]].astype(F32), *tab) * QK_SCALE
            vs[j, i] = dirs[d][2][0, r, sls[h]]
    for key in qs:
        qk[key] = (_dot_nt(qs[key].astype(BF16), ks[key].astype(BF16)) * dm_ref[key[1]]).astype(BF16)
        kv[key] = _dot((ks[key] * kd_ref[key[1]]).T.astype(BF16), vs[key])
    state = [s_ref[i] for i in range(len(chains))]
    for j in range(n_sub):
        o_inter = [_dot((qs[j, i] * qd_ref[i]).astype(BF16), s.astype(BF16)) for i, s in enumerate(state)]
        for i, (d, h) in enumerate(chains):
            dirs[d][5][0, rows[j][d], sls[h]] = (_dot(qk[j, i], vs[j, i]) + o_inter[i]).astype(dirs[d][5].dtype)
            chunk_decay = jnp.exp(jnp.full((1, HEAD_DIM), float(c), F32) * lg_ref[d, h])
            state[i] = state[i] * chunk_decay + kv[j, i]
    for i, st in enumerate(state):
        s_ref[i] = st


RET_STEP_CHUNKS = 2


def _ret_call(log_gamma, p_main, cos2, sin2, *, l_lat):
    b, l_all, _ = p_main.shape
    c = RET_CHUNK
    blk = RET_STEP_CHUNKS * c
    n_all, n_lat = l_all // blk, l_lat // blk
    n_ctx = n_all - n_lat

    def fwd(s):
        return jnp.where(s < n_ctx, n_lat + s, s - n_ctx)

    def bwd(s):
        return n_all - 1 - s

    def pspec(colblk, order):
        return pl.BlockSpec((1, blk, RET_W), lambda bb, s: (bb, order(s), colblk))

    def tspec(order):
        return pl.BlockSpec((blk, HEAD_DIM), lambda bb, s: (order(s), 0))

    in_specs = [pl.BlockSpec(memory_space=pltpu.SMEM)]
    for order in (fwd, bwd):
        in_specs += [pspec(COL_RET_Q // RET_W, order), pspec(COL_RET_K // RET_W, order),
                     pspec(COL_RET_V // RET_W, order), tspec(order), tspec(order)]
    nhd = 2 * RET_HEADS
    return pl.pallas_call(
        _ret_kernel,
        grid=(b, n_all),
        in_specs=in_specs,
        out_specs=[pl.BlockSpec((1, blk, RET_W), lambda bb, s: (bb, fwd(s), 0)),
                   pl.BlockSpec((1, blk, RET_W), lambda bb, s: (bb, bwd(s), 0))],
        out_shape=[jax.ShapeDtypeStruct((b, l_all, RET_W), BF16)] * 2,
        scratch_shapes=[pltpu.VMEM((nhd, HEAD_DIM, HEAD_DIM), F32), pltpu.VMEM((nhd, c, c), F32),
                        pltpu.VMEM((nhd, c, HEAD_DIM), F32), pltpu.VMEM((nhd, c, HEAD_DIM), F32)],
        compiler_params=_params(("arbitrary", "arbitrary")),
        name="retention_scan",
    )(log_gamma, p_main, p_main, p_main, cos2, sin2, p_main, p_main, p_main, cos2, sin2)


def _attprep_kernel(q_ref, k_ref, v_ref, c_ref, s_ref, qw_ref, kw_ref, qo_ref, ko_ref, vo_ref):
    cos2 = c_ref[...]
    sin2 = s_ref[...]

    def norm_rope(x, w):
        xf = x.astype(F32)
        y = xf * lax.rsqrt(jnp.mean(xf * xf, -1, keepdims=True) + EPS) * w
        return _rope(y, cos2, sin2)

    for h in range(ATT_HEADS):
        sl = slice(h * HEAD_DIM, (h + 1) * HEAD_DIM)
        qo_ref[0, :, sl] = (norm_rope(q_ref[0, :, sl], qw_ref[...]) * (QK_SCALE * LOG2_E)).astype(BF16)
    for h in range(ATT_KV_HEADS):
        sl = slice(h * HEAD_DIM, (h + 1) * HEAD_DIM)
        ko_ref[0, sl, :] = norm_rope(k_ref[0, :, sl], kw_ref[...]).T.astype(BF16)
        vo_ref[0, :, 2 * h * HEAD_DIM:(2 * h + 1) * HEAD_DIM] = v_ref[0, :, sl]
        vo_ref[0, :, (2 * h + 1) * HEAD_DIM:(2 * h + 2) * HEAD_DIM] = jnp.ones((v_ref.shape[1], HEAD_DIM), BF16)


def _attprep_call(p_main, cos2, sin2, qn_w, kn_w, *, tm):
    b, l_all, _ = p_main.shape
    return pl.pallas_call(
        _attprep_kernel,
        grid=(b, l_all // tm),
        in_specs=[pl.BlockSpec((1, tm, ATT_W), lambda bb, i: (bb, i, COL_ATT_Q // ATT_W)),
                  pl.BlockSpec((1, tm, ATT_KV_W), lambda bb, i: (bb, i, COL_ATT_K // ATT_KV_W)),
                  pl.BlockSpec((1, tm, ATT_KV_W), lambda bb, i: (bb, i, COL_ATT_V // ATT_KV_W)),
                  pl.BlockSpec((tm, HEAD_DIM), lambda bb, i: (i, 0)),
                  pl.BlockSpec((tm, HEAD_DIM), lambda bb, i: (i, 0)),
                  pl.BlockSpec((1, HEAD_DIM), lambda bb, i: (0, 0)),
                  pl.BlockSpec((1, HEAD_DIM), lambda bb, i: (0, 0))],
        out_specs=[pl.BlockSpec((1, tm, ATT_W), lambda bb, i: (bb, i, 0)),
                   pl.BlockSpec((1, ATT_KV_W, tm), lambda bb, i: (bb, 0, i)),
                   pl.BlockSpec((1, tm, 2 * ATT_KV_W), lambda bb, i: (bb, i, 0))],
        out_shape=[jax.ShapeDtypeStruct((b, l_all, ATT_W), BF16),
                   jax.ShapeDtypeStruct((b, ATT_KV_W, l_all), BF16),
                   jax.ShapeDtypeStruct((b, l_all, 2 * ATT_KV_W), BF16)],
        compiler_params=_params(("arbitrary", "arbitrary")),
        name="attn_prep",
    )(p_main, p_main, p_main, cos2, sin2, qn_w, kn_w)


def _att_kernel(*refs, tk):
    q_ref, kt_ref, v_ref = refs[0], refs[1], refs[2]
    o_ref, s_ref, mx_ref, m_ref, acc_ref = refs[-5], refs[-4], refs[-3], refs[-2], refs[-1]
    tq = q_ref.shape[1]
    n = kt_ref.shape[2] // tk
    lanes = tk // HEAD_DIM
    rb = 64

    def put_scores(c):
        col = pl.multiple_of(c * tk, tk)
        s_ref[c] = _dot(q_ref[0], kt_ref[0, :, pl.ds(col, tk)])

    def fold_max(c):
        for r in range(tq // rb):
            rows = slice(r * rb, (r + 1) * rb)
            acc = mx_ref[rows, :]
            for j in range(lanes):
                acc = jnp.maximum(acc, s_ref[c, rows, j * HEAD_DIM:(j + 1) * HEAD_DIM])
            mx_ref[rows, :] = acc

    mx_ref[...] = jnp.full(mx_ref.shape, -jnp.inf, F32)
    put_scores(0)

    @pl.loop(1, n)
    def _(c):
        put_scores(c)
        fold_max(c - 1)

    fold_max(n - 1)
    m_ref[...] = jnp.max(mx_ref[...], -1, keepdims=True)

    acc_ref[...] = jnp.zeros_like(acc_ref)

    @pl.loop(0, n)
    def _(c):
        row = pl.multiple_of(c * tk, tk)
        p = jnp.exp2((s_ref[c] - m_ref[...]).astype(BF16))
        acc_ref[...] += _dot(p, v_ref[0, pl.ds(row, tk), :])

    o_ref[0] = (acc_ref[:, :HEAD_DIM] / acc_ref[:, HEAD_DIM:]).astype(BF16)


def _key_chunk(n_keys):
    for tk in (4224, 768, 512, 256):
        if n_keys % tk == 0:
            return tk
    raise ValueError("key count must be a multiple of 256")


def _att_call(qn, kn, v1, *, l_lat, tq):
    b, l_all, _ = qn.shape
    l_ctx = l_all - l_lat
    group = ATT_HEADS // ATT_KV_HEADS
    tk = _key_chunk(l_all)
    y_lat = pl.pallas_call(
        functools.partial(_att_kernel, tk=tk),
        grid=(b, ATT_HEADS, l_lat // tq),
        in_specs=[pl.BlockSpec((1, tq, HEAD_DIM), lambda bb, h, i: (bb, i, h)),
                  pl.BlockSpec((1, HEAD_DIM, l_all), lambda bb, h, i: (bb, h // group, 0),
                               pipeline_mode=pl.Buffered(1)),
                  pl.BlockSpec((1, l_all, 2 * HEAD_DIM), lambda bb, h, i: (bb, 0, h // group),
                               pipeline_mode=pl.Buffered(1))],
        out_specs=pl.BlockSpec((1, tq, HEAD_DIM), lambda bb, h, i: (bb, i, h)),
        out_shape=jax.ShapeDtypeStruct((b, l_all, ATT_W), BF16),
        scratch_shapes=[pltpu.VMEM((l_all // tk, tq, tk), F32), pltpu.VMEM((tq, HEAD_DIM), F32), pltpu.VMEM((tq, 1), F32),
                        pltpu.VMEM((tq, 2 * HEAD_DIM), F32)],
        compiler_params=_params(("arbitrary", "arbitrary", "arbitrary")),
        name="attention",
    )(qn, kn, v1)
    ctx_blk = l_lat // l_ctx
    return pl.pallas_call(
        functools.partial(_att_kernel, tk=l_ctx),
        grid=(b, ATT_HEADS),
        in_specs=[pl.BlockSpec((1, l_ctx, HEAD_DIM), lambda bb, h: (bb, ctx_blk, h)),
                  pl.BlockSpec((1, HEAD_DIM, l_ctx), lambda bb, h: (bb, h // group, ctx_blk)),
                  pl.BlockSpec((1, l_ctx, 2 * HEAD_DIM), lambda bb, h: (bb, ctx_blk, h // group)),
                  pl.BlockSpec(memory_space=pl.ANY)],
        out_specs=pl.BlockSpec((1, l_ctx, HEAD_DIM), lambda bb, h: (bb, ctx_blk, h)),
        out_shape=jax.ShapeDtypeStruct((b, l_all, ATT_W), BF16),
        scratch_shapes=[pltpu.VMEM((1, l_ctx, l_ctx), F32), pltpu.VMEM((l_ctx, HEAD_DIM), F32), pltpu.VMEM((l_ctx, 1), F32),
                        pltpu.VMEM((l_ctx, 2 * HEAD_DIM), F32)],
        input_output_aliases={3: 0},
        compiler_params=_params(("arbitrary", "arbitrary")),
        name="attention_ctx",
    )(qn, kn, v1, y_lat)


def _dnprep_kernel(*refs, tm, l_lat, l_all):
    mains, prevs, nexts = refs[0:3], refs[3:6], refs[6:9]
    cw_ref, pab_ref, alog_ref, dtb_ref = refs[9:13]
    outs = refs[13:16]
    g_ref, ext_ref = refs[16], refs[17]
    i = pl.program_id(1)
    halo = 16
    pad = DN_CONV_K // 2
    first = jnp.logical_or(i == 0, i == l_lat // tm)
    last = jnp.logical_or(i == l_lat // tm - 1, i == l_all // tm - 1)

    keep_prev = jnp.where(first, 0.0, 1.0)
    keep_next = jnp.where(last, 0.0, 1.0)

    for part in range(3):
        ext_ref[0:halo, :] = prevs[part][0].astype(F32) * keep_prev
        ext_ref[halo:halo + tm, :] = mains[part][0].astype(F32)
        ext_ref[halo + tm:, :] = nexts[part][0].astype(F32) * keep_next
        acc = jnp.zeros((tm, DN_W), F32)
        for j in range(DN_CONV_K):
            w_j = cw_ref[j:j + 1, part * DN_W:(part + 1) * DN_W]
            acc = acc + w_j * ext_ref[pl.ds(halo - pad + j, tm), :]
        y = _silu(acc)
        for h in range(DN_HEADS):
            sl = slice(h * HEAD_DIM, (h + 1) * HEAD_DIM)
            yh = y[:, sl]
            if part < 2:
                yh = yh * lax.rsqrt(jnp.sum(yh * yh, -1, keepdims=True) + EPS)
            if part == 0:
                yh = yh * QK_SCALE
            outs[part][0, :, sl] = yh.astype(BF16)

    a = pab_ref[0]
    col = lax.broadcasted_iota(jnp.int32, a.shape, 1)
    z = a + dtb_ref[...]
    softplus = jnp.maximum(z, 0.0) + jnp.log(1.0 + jnp.exp(-jnp.abs(z)))
    g = -jnp.exp(alog_ref[...]) * softplus
    ii = lax.broadcasted_iota(jnp.int32, (tm, tm), 0)
    rr = lax.broadcasted_iota(jnp.int32, (tm, tm), 1)
    same_chunk = (ii // DN_CHUNK) == (rr // DN_CHUNK)
    prefix = jnp.logical_and(same_chunk, rr <= ii).astype(F32)
    suffix = jnp.logical_and(same_chunk, rr >= ii).astype(F32)
    hi = lax.Precision.HIGHEST
    cum_f = jnp.dot(prefix, g, preferred_element_type=F32, precision=hi)
    cum_b = jnp.dot(suffix, g, preferred_element_type=F32, precision=hi)
    g_ref[0] = jnp.where(col < DN_HEADS, cum_f, jnp.where(col < N_DN_GATES, cum_b, jax.nn.sigmoid(a)))


def _dnprep_call(p_main, p_ab, conv_w, alog_row, dtb_row, *, l_lat, tm):
    b, l_all, _ = p_main.shape
    halo = 16
    r = tm // halo
    cb = COL_DN_QKV // DN_W
    in_specs = []
    for part in range(3):
        in_specs.append(pl.BlockSpec((1, tm, DN_W), lambda bb, i, p=part: (bb, i, cb + p)))
    for part in range(3):
        in_specs.append(pl.BlockSpec((1, halo, DN_W), lambda bb, i, p=part: (bb, jnp.maximum(i * r - 1, 0), cb + p)))
    for part in range(3):
        in_specs.append(pl.BlockSpec(
            (1, halo, DN_W), lambda bb, i, p=part: (bb, jnp.minimum((i + 1) * r, l_all // halo - 1), cb + p)))
    in_specs += [pl.BlockSpec((DN_CONV_K, 3 * DN_W), lambda bb, i: (0, 0)),
                 pl.BlockSpec((1, tm, GATE_COLS), lambda bb, i: (bb, i, 0)),
                 pl.BlockSpec((1, GATE_COLS), lambda bb, i: (0, 0)),
                 pl.BlockSpec((1, GATE_COLS), lambda bb, i: (0, 0))]
    kern = functools.partial(_dnprep_kernel, tm=tm, l_lat=l_lat, l_all=l_all)
    return pl.pallas_call(
        kern,
        grid=(b, l_all // tm),
        in_specs=in_specs,
        out_specs=[pl.BlockSpec((1, tm, DN_W), lambda bb, i: (bb, i, 0))] * 3
        + [pl.BlockSpec((1, tm, GATE_COLS), lambda bb, i: (bb, i, 0))],
        out_shape=[jax.ShapeDtypeStruct((b, l_all, DN_W), BF16)] * 3
        + [jax.ShapeDtypeStruct((b, l_all, GATE_COLS), F32)],
        scratch_shapes=[pltpu.VMEM((tm + 2 * halo, DN_W), F32)],
        compiler_params=_params(("arbitrary", "arbitrary")),
        name="deltanet_conv_gates",
    )(*([p_main] * 9), conv_w, p_ab, alog_row, dtb_row)


def _neumann_inverse_many(a_list, eye):
    ps = [-a for a in a_list]
    ts = [eye + p for p in ps]
    for _ in range(5):
        pbs = [p.astype(BF16) for p in ps]
        ps = [_dot(pb, pb) for pb in pbs]
        ts = [t + _dot(t.astype(BF16), p.astype(BF16)) for t, p in zip(ts, ps)]
    return ts


def _dnchunk_kernel(q_ref, k_ref, v_ref, g_ref, gt_ref, w_ref, kc_ref, qg_ref, kgt_ref, qk_ref, gl_ref):
    c = DN_CHUNK
    n_sub = q_ref.shape[1] // c
    ii = lax.broadcasted_iota(jnp.int32, (c, c), 0)
    jj = lax.broadcasted_iota(jnp.int32, (c, c), 1)
    eye = (ii == jj).astype(F32)

    qk_ref[...] = jnp.zeros_like(qk_ref)
    kgt_ref[...] = jnp.zeros_like(kgt_ref)
    gl_ref[...] = jnp.zeros_like(gl_ref)

    heads = range(DN_HEADS)
    subs = range(n_sub)
    chains = [(u, d, h) for u in subs for h in heads for d in range(2)]
    rows = [slice(u * c, (u + 1) * c) for u in subs]
    sls = [slice(h * HEAD_DIM, (h + 1) * HEAD_DIM) for h in heads]
    gcol = [g_ref[0, rows[u], :] for u in subs]
    grow = [gt_ref[0, u] for u in subs]
    q16 = {(u, h): q_ref[0, rows[u], sls[h]] for u in subs for h in heads}
    k16 = {(u, h): k_ref[0, rows[u], sls[h]] for u in subs for h in heads}
    kk = {uh: _dot_nt(k16[uh], k16[uh]) for uh in k16}
    qk0 = {uh: _dot_nt(q16[uh], k16[uh]) for uh in k16}

    gcc, tot, beta, decay, a_list = {}, {}, {}, {}, []
    for u, d, h in chains:
        col = d * DN_HEADS + h
        key = (u, d, h)
        gcc[key] = gcol[u][:, col:col + 1]
        gcr = grow[u][col:col + 1, :]
        tot[key] = gcol[u][c - 1:c, col:col + 1] if d == 0 else gcol[u][0:1, col:col + 1]
        beta[key] = gcol[u][:, N_DN_GATES + col:N_DN_GATES + col + 1]
        incl = (ii >= jj) if d == 0 else (ii <= jj)
        strict = (ii > jj) if d == 0 else (ii < jj)
        decay[key] = jnp.where(incl, jnp.exp(jnp.where(incl, gcc[key] - gcr, 0.0)), 0.0)
        a_list.append(jnp.where(strict, kk[u, h] * beta[key] * decay[key], 0.0))
        gl_ref[0, d, u, h:h + 1, :] = jnp.broadcast_to(jnp.exp(tot[key]), (1, HEAD_DIM))

    t16 = [t.astype(BF16) for t in _neumann_inverse_many(a_list, eye)]

    for key, t in zip(chains, t16):
        u, d, h = key
        sl = sls[h]
        k = k16[u, h].astype(F32)
        e_col = jnp.exp(gcc[key])
        rhs = jnp.concatenate([(v_ref[0, rows[u], sl].astype(F32) * beta[key]).astype(BF16),
                               (k * (beta[key] * e_col)).astype(BF16)], axis=1)
        wk = _dot(t, rhs)
        w_ref[0, d, rows[u], sl] = wk[:, :HEAD_DIM]
        kc_ref[0, d, rows[u], sl] = wk[:, HEAD_DIM:].astype(BF16)
        qg_ref[0, d, rows[u], sl] = (q16[u, h].astype(F32) * e_col).astype(BF16)
        kg = k * jnp.exp(tot[key] - gcc[key])
        kgt_ref[0, d, u, :, pl.ds(h * HEAD_DIM, c)] = kg.T.astype(BF16)
        qk_ref[0, d, rows[u], pl.ds(h * HEAD_DIM, c)] = (qk0[u, h] * decay[key]).astype(BF16)


DN_STEP_CHUNKS = 4


def _dnchunk_call(qd, kd, vd, g, gt):
    b, l_all, _ = qd.shape
    c = DN_CHUNK
    n = l_all // c
    ns = DN_STEP_CHUNKS
    tok = lambda dt: jax.ShapeDtypeStruct((b, 2, l_all, DN_W), dt)
    tok_spec = pl.BlockSpec((1, 2, ns * c, DN_W), lambda bb, s: (bb, 0, s, 0))
    return pl.pallas_call(
        _dnchunk_kernel,
        grid=(b, n // ns),
        in_specs=[pl.BlockSpec((1, ns * c, DN_W), lambda bb, s: (bb, s, 0))] * 3
        + [pl.BlockSpec((1, ns * c, GATE_COLS), lambda bb, s: (bb, s, 0)),
           pl.BlockSpec((1, ns, 2 * N_DN_GATES, c), lambda bb, s: (bb, s, 0, 0))],
        out_specs=[tok_spec, tok_spec, tok_spec,
                   pl.BlockSpec((1, 2, ns, HEAD_DIM, DN_W), lambda bb, s: (bb, 0, s, 0, 0)),
                   tok_spec,
                   pl.BlockSpec((1, 2, ns, 8, HEAD_DIM), lambda bb, s: (bb, 0, s, 0, 0))],
        out_shape=[tok(F32), tok(BF16), tok(BF16),
                   jax.ShapeDtypeStruct((b, 2, n, HEAD_DIM, DN_W), BF16),
                   tok(BF16),
                   jax.ShapeDtypeStruct((b, 2, n, 8, HEAD_DIM), F32)],
        compiler_params=_params(("arbitrary", "arbitrary")),
        name="deltanet_chunk_factors",
    )(qd, kd, vd, g, gt)


def _dnscan_kernel(*refs):
    ins_f, ins_b = refs[0:6], refs[6:12]
    of_ref, ob_ref, s_ref = refs[12], refs[13], refs[14]
    c = DN_CHUNK
    n_sub = of_ref.shape[1] // c
    step = pl.program_id(1)

    @pl.when(step == 0)
    def _():
        s_ref[...] = jnp.zeros_like(s_ref)

    ins, outs = (ins_f, ins_b), (of_ref, ob_ref)
    chains = [(d, h) for d in range(2) for h in range(DN_HEADS)]
    sls = [slice(h * HEAD_DIM, (h + 1) * HEAD_DIM) for h in range(DN_HEADS)]
    state = [s_ref[d * DN_HEADS + h] for d, h in chains]
    for j in range(n_sub):
        sub = [j, n_sub - 1 - j]
        rows = [slice(u * c, (u + 1) * c) for u in sub]
        s16 = [s.astype(BF16) for s in state]
        v16 = [(ins[d][0][0, 0, rows[d], sls[h]] - _dot(ins[d][1][0, 0, rows[d], sls[h]], s)).astype(BF16)
               for (d, h), s in zip(chains, s16)]
        o_inter = [_dot(ins[d][2][0, 0, rows[d], sls[h]], s) for (d, h), s in zip(chains, s16)]
        new_state = []
        for (d, h), s, v, oi in zip(chains, state, v16, o_inter):
            qk = ins[d][4][0, 0, rows[d], pl.ds(h * HEAD_DIM, c)]
            outs[d][0, rows[d], sls[h]] = (oi + _dot(qk, v)).astype(outs[d].dtype)
            kgt = ins[d][3][0, 0, sub[d], :, pl.ds(h * HEAD_DIM, c)]
            new_state.append(s * ins[d][5][0, 0, sub[d], h:h + 1, :] + _dot(kgt, v))
        state = new_state
    for i, s in enumerate(state):
        s_ref[i] = s


def _dnscan_call(w, kc, qg, kgt, qk, gl, *, l_lat):
    b, _, l_all, _ = w.shape
    ns = DN_STEP_CHUNKS
    blk = ns * DN_CHUNK
    n_all, n_lat = l_all // blk, l_lat // blk
    n_ctx = n_all - n_lat

    def fwd(s):
        return jnp.where(s < n_ctx, n_lat + s, s - n_ctx)

    def bwd(s):
        return n_all - 1 - s

    in_specs = []
    for d, order in enumerate((fwd, bwd)):
        tok_spec = pl.BlockSpec((1, 1, blk, DN_W), lambda bb, s, d=d, o=order: (bb, d, o(s), 0))
        in_specs += [tok_spec, tok_spec, tok_spec,
                     pl.BlockSpec((1, 1, ns, HEAD_DIM, DN_W), lambda bb, s, d=d, o=order: (bb, d, o(s), 0, 0)),
                     tok_spec,
                     pl.BlockSpec((1, 1, ns, 8, HEAD_DIM), lambda bb, s, d=d, o=order: (bb, d, o(s), 0, 0))]
    return pl.pallas_call(
        _dnscan_kernel,
        grid=(b, n_all),
        in_specs=in_specs,
        out_specs=[pl.BlockSpec((1, blk, DN_W), lambda bb, s: (bb, fwd(s), 0)),
                   pl.BlockSpec((1, blk, DN_W), lambda bb, s: (bb, bwd(s), 0))],
        out_shape=[jax.ShapeDtypeStruct((b, l_all, DN_W), BF16)] * 2,
        scratch_shapes=[pltpu.VMEM((2 * DN_HEADS, HEAD_DIM, HEAD_DIM), F32)],
        compiler_params=_params(("arbitrary", "arbitrary")),
        name="deltanet_scan",
    )(w, kc, qg, kgt, qk, gl, w, kc, qg, kgt, qk, gl)


def _rope_tables(l_lat, l_ctx):
    rows = l_lat // GRID_W
    n_freq = HEAD_DIM // 4
    inv = ROPE_THETA ** (-jnp.arange(n_freq, dtype=F32) / n_freq)
    row_ang = jnp.arange(rows, dtype=F32)[:, None] * inv
    col_ang = jnp.arange(GRID_W, dtype=F32)[:, None] * inv

    def table(fn):
        return jnp.concatenate([jnp.repeat(fn(row_ang), GRID_W, axis=0), jnp.tile(fn(col_ang), (rows, 1))], -1)

    cos, sin = table(jnp.cos), table(jnp.sin)
    cos2 = jnp.concatenate([cos, cos], -1)
    sin2 = jnp.concatenate([-sin, sin], -1)
    cos2 = jnp.concatenate([cos2, jnp.ones((l_ctx, HEAD_DIM), F32)], 0)
    sin2 = jnp.concatenate([sin2, jnp.zeros((l_ctx, HEAD_DIM), F32)], 0)
    return cos2, sin2


def _pad_row(v, width):
    v = v.reshape(1, -1).astype(F32)
    return jnp.pad(v, ((0, 0), (0, width - v.shape[1])))


def _row_tile(l_lat, l_ctx):
    l_all = l_lat + l_ctx
    for tm in (768, 512, 256, 128):
        if l_all % tm == 0:
            return tm
    raise ValueError("token count must be a multiple of 128")


def kernel(x, c, ctx, c_ctx, w_ada, b_ada, w_in, ret_decay_logit, dn_conv_w, dn_a_log, dn_dt_bias, dn_norm_w,
           att_qn_w, att_kn_w, w_o, ln1_w, ln1_b, w_ffn_in, w_ffn_out, ln2_w, ln2_b):
    bsz, l_lat, d = x.shape
    l_ctx = ctx.shape[1]
    l_all = l_lat + l_ctx
    depth = w_ada.shape[0]
    assert l_lat % 256 == 0 and l_ctx % 256 == 0 and l_lat % GRID_W == 0
    tm = _row_tile(l_lat, l_ctx)
    tn = MAIN_W // 2

    cos2, sin2 = _rope_tables(l_lat, l_ctx)

    cond_rows = 8 * pl.cdiv(bsz + 1, 8)
    cond = jnp.concatenate([c, c_ctx[None, :]], 0)
    cond = jnp.pad(cond, ((0, cond_rows - bsz - 1), (0, 0)))
    mod = _ada_call(cond, w_ada, b_ada)

    w_main, w_ab = _repack_call(w_in.astype(BF16))
    w_o16, w_ffn_in16, w_ffn_out16 = w_o.astype(BF16), w_ffn_in.astype(BF16), w_ffn_out.astype(BF16)
    for i in range(depth):
        last = i == depth - 1
        ml = mod[i, :bsz].reshape(bsz, 6, d)
        mc = mod[i, bsz].reshape(1, 6, d)
        if i == 0:
            p_main, p_ab, x_all = _inproj_call((x, ctx), ml, mc, w_main, w_ab, layer=i, l_lat=l_lat, tm=tm, tn=tn // 2)
        else:
            p_main, p_ab = _inproj_call(x_all, ml, mc, w_main, w_ab, layer=i, l_lat=l_lat, tm=tm, tn=tn)

        log_gamma = jax.nn.log_sigmoid(ret_decay_logit[i].astype(F32))
        ret_f, ret_b = _ret_call(log_gamma, p_main, cos2, sin2, l_lat=l_lat)

        qn, kn, v1 = _attprep_call(p_main, cos2, sin2, att_qn_w[i].reshape(1, -1), att_kn_w[i].reshape(1, -1),
                                   tm=tm)
        y_att = _att_call(qn, kn, v1, l_lat=l_lat, tq=1024 if l_lat % 1024 == 0 else 256)

        qd, kd, vd, g = _dnprep_call(p_main, p_ab, dn_conv_w[i], _pad_row(dn_a_log[i], GATE_COLS),
                                     _pad_row(dn_dt_bias[i], GATE_COLS), l_lat=l_lat, tm=256)
        gt = g[:, :, :2 * N_DN_GATES].reshape(bsz, l_all // DN_CHUNK, DN_CHUNK, 2 * N_DN_GATES)
        gt = jnp.swapaxes(gt, 2, 3)
        dn_f, dn_b = _dnscan_call(*_dnchunk_call(qd, kd, vd, g, gt), l_lat=l_lat)

        x_all = _wo_call(ret_f, ret_b, dn_f, dn_b, p_main, dn_norm_w[i].reshape(1, -1), y_att, x_all, ml, mc,
                         w_o16, ln1_w[i].reshape(1, -1), ln1_b[i].reshape(1, -1), layer=i, l_lat=l_lat, tm=tm)
        x_all = _ffn_call(x_all, ml, mc, w_ffn_in16, w_ffn_out16, ln2_w[i].reshape(1, -1), ln2_b[i].reshape(1, -1),
                          layer=i, l_lat=l_lat, out_rows=l_lat if last else l_all, tm=tm, tf=512)
    return x_all
```

```python
import functools

import jax
import jax.numpy as jnp
from jax import lax
from jax.experimental import pallas as pl
from jax.experimental.pallas import tpu as pltpu

F32 = jnp.float32
BF16 = jnp.bfloat16

HEAD_DIM = 128
RET_HEADS = 4
DN_HEADS = 4
ATT_HEADS = 8
ATT_KV_HEADS = 2
RET_W = RET_HEADS * HEAD_DIM
DN_W = DN_HEADS * HEAD_DIM
ATT_W = ATT_HEADS * HEAD_DIM
ATT_KV_W = ATT_KV_HEADS * HEAD_DIM
RET_CHUNK = 128
DN_CHUNK = 64
DN_CONV_K = 5
GRID_W = 64
ROPE_THETA = 10000.0
MODEL_DEPTH = 4
DEEPNORM_ALPHA = (2 * MODEL_DEPTH) ** 0.25
EPS = 1e-6
QK_SCALE = HEAD_DIM ** -0.5
LOG2_E = 1.4426950408889634

COL_RET_Q = 0
COL_RET_K = RET_W
COL_RET_V = 2 * RET_W
COL_RET_G = 3 * RET_W
COL_DN_QKV = 4 * RET_W
COL_DN_Z = COL_DN_QKV + 3 * DN_W
COL_ATT_Q = COL_DN_Z + DN_W
COL_ATT_K = COL_ATT_Q + ATT_W
COL_ATT_V = COL_ATT_K + ATT_KV_W
MAIN_W = COL_ATT_V + ATT_KV_W
GATE_COLS = 128
N_DN_GATES = 2 * DN_HEADS

VMEM_LIMIT_MB = 56


def _params(sem, vmem_mb=VMEM_LIMIT_MB):
    return pltpu.CompilerParams(dimension_semantics=sem, vmem_limit_bytes=vmem_mb * 1024 * 1024)


def _dot(a, b):
    return jnp.dot(a, b, preferred_element_type=F32)


def _dot_nt(a, b):
    return lax.dot_general(a, b, (((1,), (1,)), ((), ())), preferred_element_type=F32)


def _silu(x):
    return x * jax.nn.sigmoid(x)


def _is_ctx_rows(i, tm, l_lat):
    rows = i * tm + lax.broadcasted_iota(jnp.int32, (tm, 1), 0)
    return rows >= l_lat


def _mod_row(is_ctx, ml_ref, mc_ref, k):
    return jnp.where(is_ctx, mc_ref[0, k:k + 1, :], ml_ref[0, k:k + 1, :])


def _layer_norm(r, w, b):
    mu = jnp.mean(r, -1, keepdims=True)
    rc = r - mu
    var = jnp.mean(rc * rc, -1, keepdims=True)
    return rc * lax.rsqrt(var + EPS) * w + b


def _split_bf16(a):
    hi = a.astype(BF16)
    return hi, (a - hi.astype(F32)).astype(BF16)


def _ada_kernel(c_ref, w_ref, b_ref, o_ref):
    h_hi, h_lo = _split_bf16(_silu(c_ref[...]))
    w_hi, w_lo = _split_bf16(w_ref[0])
    o_ref[0] = _dot(h_hi, w_hi) + _dot(h_hi, w_lo) + _dot(h_lo, w_hi) + b_ref[0]


def _ada_call(cond, w_ada, b_ada):
    depth, d, n6 = w_ada.shape
    rows = cond.shape[0]
    tn = 1024
    return pl.pallas_call(
        _ada_kernel,
        grid=(depth, n6 // tn),
        in_specs=[pl.BlockSpec((rows, d), lambda l, j: (0, 0)),
                  pl.BlockSpec((1, d, tn), lambda l, j: (l, 0, j)),
                  pl.BlockSpec((1, 1, tn), lambda l, j: (l, 0, j))],
        out_specs=pl.BlockSpec((1, rows, tn), lambda l, j: (l, 0, j)),
        out_shape=jax.ShapeDtypeStruct((depth, rows, n6), F32),
        compiler_params=_params(("arbitrary", "arbitrary")),
        name="ada_mod",
    )(cond, w_ada, b_ada.reshape(depth, 1, n6))


GATE_LO = COL_DN_Z + DN_W


def _repack_kernel(w_ref, wm_ref, wab_ref):
    wm_ref[0, :, :GATE_LO] = w_ref[0, :, :GATE_LO].astype(BF16)
    wm_ref[0, :, GATE_LO:] = w_ref[0, :, GATE_LO + 2 * N_DN_GATES:MAIN_W + 2 * N_DN_GATES].astype(BF16)
    gates = w_ref[0, :, GATE_LO:GATE_LO + GATE_COLS]
    lane = lax.broadcasted_iota(jnp.int32, gates.shape, 1)
    wab_ref[0] = jnp.where(lane < 2 * N_DN_GATES, gates, jnp.zeros_like(gates)).astype(BF16)


def _repack_call(w_in):
    depth, d, proj_w = w_in.shape
    assert proj_w >= MAIN_W + 2 * N_DN_GATES
    tr = 256
    return pl.pallas_call(
        _repack_kernel,
        grid=(depth, d // tr),
        in_specs=[pl.BlockSpec((1, tr, proj_w), lambda l, r: (l, r, 0))],
        out_specs=[pl.BlockSpec((1, tr, MAIN_W), lambda l, r: (l, r, 0)),
                   pl.BlockSpec((1, tr, GATE_COLS), lambda l, r: (l, r, 0))],
        out_shape=[jax.ShapeDtypeStruct((depth, d, MAIN_W), BF16), jax.ShapeDtypeStruct((depth, d, GATE_COLS), BF16)],
        compiler_params=_params(("arbitrary", "arbitrary")),
        name="repack_w_in",
    )(w_in)


def _inproj_kernel(*refs, tm, l_lat, split_input):
    if split_input:
        x_ref, c_ref, ml_ref, mc_ref, w_ref, wab_ref, pm_ref, pab_ref, xo_ref, h_ref = refs
    else:
        x_ref, ml_ref, mc_ref, w_ref, wab_ref, pm_ref, pab_ref, h_ref = refs
    i = pl.program_id(1)
    j = pl.program_id(2)

    @pl.when(j == 0)
    def _():
        if split_input:
            n_full, rem = divmod(l_lat, tm)

            @pl.when(i < n_full)
            def _():
                xo_ref[0] = x_ref[0]

            @pl.when(i >= n_full)
            def _():
                if rem:
                    xo_ref[0, :rem, :] = x_ref[0, :rem, :]
                xo_ref[0, rem:, :] = c_ref[0]

            x_tile = xo_ref[0]
        else:
            x_tile = x_ref[0]
        is_ctx = _is_ctx_rows(i, tm, l_lat)
        shift = _mod_row(is_ctx, ml_ref, mc_ref, 0)
        scale = _mod_row(is_ctx, ml_ref, mc_ref, 1)
        h = (x_tile * (1.0 + scale) + shift).astype(BF16)
        h_ref[...] = h
        pab_ref[0] = _dot(h, wab_ref[...])

    tn = w_ref.shape[1]
    step = 512
    for lo in range(0, tn, step):
        hi = min(lo + step, tn)
        pm_ref[0, :, lo:hi] = _dot(h_ref[...], w_ref[:, lo:hi]).astype(BF16)


def _inproj_call(xs, ml, mc, w_main, w_ab, *, layer, l_lat, tm, tn):
    split_input = isinstance(xs, tuple)
    xs = list(xs) if split_input else [xs]
    b, _, d = xs[0].shape
    l_all = l_lat + xs[1].shape[1] if split_input else xs[0].shape[1]
    x_spec = pl.BlockSpec((1, tm, d), lambda bb, i, j: (bb, i, 0))
    in_specs = [x_spec]
    out_specs = [pl.BlockSpec((1, tm, tn), lambda bb, i, j: (bb, i, j)),
                 pl.BlockSpec((1, tm, GATE_COLS), lambda bb, i, j: (bb, i, 0))]
    out_shape = [jax.ShapeDtypeStruct((b, l_all, MAIN_W), BF16), jax.ShapeDtypeStruct((b, l_all, GATE_COLS), F32)]
    if split_input:
        l_ctx = l_all - l_lat
        assert l_lat % tm + l_ctx == tm, "the context must exactly fill the last row tile"
        in_specs.append(pl.BlockSpec((1, l_ctx, d), lambda bb, i, j: (bb, 0, 0)))
        out_specs.append(x_spec)
        out_shape.append(jax.ShapeDtypeStruct((b, l_all, d), F32))
    in_specs +=[pl.BlockSpec((1, 6, d), lambda bb, i, j: (bb, 0, 0)),
                 pl.BlockSpec((1, 6, d), lambda bb, i, j: (0, 0, 0)),
                 pl.BlockSpec((None, d, tn), lambda bb, i, j: (layer, 0, j)),
                 pl.BlockSpec((None, d, GATE_COLS), lambda bb, i, j: (layer, 0, 0))]
    return pl.pallas_call(
        functools.partial(_inproj_kernel, tm=tm, l_lat=l_lat, split_input=split_input),
        grid=(b, l_all // tm, MAIN_W // tn),
        in_specs=in_specs,
        out_specs=out_specs,
        out_shape=out_shape,
        scratch_shapes=[pltpu.VMEM((tm, d), BF16)],
        compiler_params=_params(("arbitrary", "arbitrary", "arbitrary")),
        name="in_proj",
    )(*xs, ml, mc, w_main, w_ab)


WO_SUB_ROWS = 256


def _wo_kernel(rf_ref, rb_ref, rg_ref, df_ref, db_ref, dz_ref, nw_ref, a_ref, x_ref, ml_ref, mc_ref, w_ref,
               lnw_ref, lnb_ref, o_ref, *, tm, l_lat):
    i = pl.program_id(1)
    sub = WO_SUB_ROWS
    n_rd = RET_W + DN_W

    def head_norm(o):
        return o * lax.rsqrt(jnp.mean(o * o, -1, keepdims=True) + EPS)

    for lo in range(0, tm, sub):
        rows = slice(lo, lo + sub)
        ys = []
        for h in range(RET_HEADS):
            sl = slice(h * HEAD_DIM, (h + 1) * HEAD_DIM)
            o = rf_ref[0, rows, sl].astype(F32) + rb_ref[0, rows, sl].astype(F32)
            ys.append((head_norm(o) * _silu(rg_ref[0, rows, sl].astype(F32))).astype(BF16))
        for h in range(DN_HEADS):
            sl = slice(h * HEAD_DIM, (h + 1) * HEAD_DIM)
            o = df_ref[0, rows, sl].astype(F32) + db_ref[0, rows, sl].astype(F32)
            ys.append((head_norm(o) * nw_ref[...] * _silu(dz_ref[0, rows, sl].astype(F32))).astype(BF16))
        y_rd = jnp.concatenate(ys, axis=1)
        acc = _dot(y_rd, w_ref[0:n_rd, :]) + _dot(a_ref[0, rows, :], w_ref[n_rd:, :])
        rid = i * tm + lo + lax.broadcasted_iota(jnp.int32, (sub, 1), 0)
        gate = _mod_row(rid >= l_lat, ml_ref, mc_ref, 2)
        r = DEEPNORM_ALPHA * x_ref[0, rows, :] + gate * acc
        o_ref[0, rows, :] = _layer_norm(r, lnw_ref[...], lnb_ref[...])


def _wo_call(ret_f, ret_b, dn_f, dn_b, p_main, dn_norm_w, y_att, x_all, ml, mc, w_o, ln_w, ln_b, *, layer, l_lat,
             tm):
    b, l_all, d = x_all.shape
    kern = functools.partial(_wo_kernel, tm=tm, l_lat=l_lat)
    o_spec = pl.BlockSpec((1, tm, RET_W), lambda bb, i: (bb, i, 0))
    return pl.pallas_call(
        kern,
        grid=(b, l_all // tm),
        in_specs=[o_spec, o_spec,
                  pl.BlockSpec((1, tm, RET_W), lambda bb, i: (bb, i, COL_RET_G // RET_W)),
                  o_spec, o_spec,
                  pl.BlockSpec((1, tm, DN_W), lambda bb, i: (bb, i, COL_DN_Z // DN_W)),
                  pl.BlockSpec((1, HEAD_DIM), lambda bb, i: (0, 0)),
                  pl.BlockSpec((1, tm, ATT_W), lambda bb, i: (bb, i, 0)),
                  pl.BlockSpec((1, tm, d), lambda bb, i: (bb, i, 0)),
                  pl.BlockSpec((1, 6, d), lambda bb, i: (bb, 0, 0)),
                  pl.BlockSpec((1, 6, d), lambda bb, i: (0, 0, 0)),
                  pl.BlockSpec((None, RET_W + DN_W + ATT_W, d), lambda bb, i: (layer, 0, 0),
                               pipeline_mode=pl.Buffered(1)),
                  pl.BlockSpec((1, d), lambda bb, i: (0, 0)),
                  pl.BlockSpec((1, d), lambda bb, i: (0, 0))],
        out_specs=pl.BlockSpec((1, tm, d), lambda bb, i: (bb, i, 0)),
        out_shape=jax.ShapeDtypeStruct((b, l_all, d), F32),
        compiler_params=_params(("arbitrary", "arbitrary")),
        name="w_o_postnorm",
    )(ret_f, ret_b, p_main, dn_f, dn_b, p_main, dn_norm_w, y_att, x_all, ml, mc, w_o, ln_w, ln_b)


def _ffn_kernel(x_ref, ml_ref, mc_ref, wg_ref, wu_ref, wo_ref, lnw_ref, lnb_ref, o_ref, h_ref, acc_ref,
                *, tm, l_lat):
    i = pl.program_id(1)
    f = pl.program_id(2)

    @pl.when(f == 0)
    def _():
        is_ctx = _is_ctx_rows(i, tm, l_lat)
        shift = _mod_row(is_ctx, ml_ref, mc_ref, 3)
        scale = _mod_row(is_ctx, ml_ref, mc_ref, 4)
        h_ref[...] = (x_ref[0] * (1.0 + scale) + shift).astype(BF16)
        acc_ref[...] = jnp.zeros_like(acc_ref)

    h = h_ref[...]
    g = _dot(h, wg_ref[...])
    u = _dot(h, wu_ref[...])
    acc_ref[...] += _dot((_silu(g) * u).astype(BF16), wo_ref[...])

    @pl.when(f == pl.num_programs(2) - 1)
    def _():
        is_ctx = _is_ctx_rows(i, tm, l_lat)
        gate = _mod_row(is_ctx, ml_ref, mc_ref, 5)
        r = DEEPNORM_ALPHA * x_ref[0] + gate * acc_ref[...]
        o_ref[0] = _layer_norm(r, lnw_ref[...], lnb_ref[...])


def _ffn_call(x_all, ml, mc, w_in, w_out, ln_w, ln_b, *, layer, l_lat, out_rows, tm, tf):
    b, _, d = x_all.shape
    d_ff = w_out.shape[1]
    nf = d_ff // tf
    kern = functools.partial(_ffn_kernel, tm=tm, l_lat=l_lat)
    return pl.pallas_call(
        kern,
        grid=(b, pl.cdiv(out_rows, tm), nf),
        in_specs=[pl.BlockSpec((1, tm, d), lambda bb, i, f: (bb, i, 0)),
                  pl.BlockSpec((1, 6, d), lambda bb, i, f: (bb, 0, 0)),
                  pl.BlockSpec((1, 6, d), lambda bb, i, f: (0, 0, 0)),
                  pl.BlockSpec((None, d, tf), lambda bb, i, f: (layer, 0, f)),
                  pl.BlockSpec((None, d, tf), lambda bb, i, f: (layer, 0, f + nf)),
                  pl.BlockSpec((None, tf, d), lambda bb, i, f: (layer, f, 0)),
                  pl.BlockSpec((1, d), lambda bb, i, f: (0, 0)),
                  pl.BlockSpec((1, d), lambda bb, i, f: (0, 0))],
        out_specs=pl.BlockSpec((1, tm, d), lambda bb, i, f: (bb, i, 0)),
        out_shape=jax.ShapeDtypeStruct((b, out_rows, d), F32),
        scratch_shapes=[pltpu.VMEM((tm, d), BF16), pltpu.VMEM((tm, d), F32)],
        compiler_params=_params(("arbitrary", "arbitrary", "arbitrary")),
        name="ffn_postnorm",
    )(x_all, ml, mc, w_in, w_in, w_out, ln_w, ln_b)


def _rope(x, cos2, sin2):
    return x * cos2 + pltpu.roll(x, HEAD_DIM // 2, 1) * sin2


def _ret_kernel(lg_ref, qf_ref, kf_ref, vf_ref, cf_ref, sf_ref, qb_ref, kb_ref, vb_ref, cb_ref, sb_ref,
                of_ref, ob_ref, s_ref, dm_ref, qd_ref, kd_ref):
    step = pl.program_id(1)
    c = RET_CHUNK

    @pl.when(step == 0)
    def _():
        s_ref[...] = jnp.zeros_like(s_ref)
        ii = lax.broadcasted_iota(jnp.int32, (c, c), 0).astype(F32)
        jj = lax.broadcasted_iota(jnp.int32, (c, c), 1).astype(F32)
        for d in range(2):
            for h in range(RET_HEADS):
                lg = lg_ref[d, h]
                if d == 0:
                    rel, qe, ke = ii - jj, ii + 1.0, (c - 1.0) - ii
                else:
                    rel, qe, ke = jj - ii, c - ii, ii
                idx = d * RET_HEADS + h
                dm_ref[idx] = jnp.where(rel >= 0, jnp.exp(jnp.maximum(rel, 0.0) * lg), 0.0)
                qd_ref[idx] = jnp.exp(qe * lg)
                kd_ref[idx] = jnp.exp(ke * lg)

    dirs = ((qf_ref, kf_ref, vf_ref, cf_ref, sf_ref, of_ref), (qb_ref, kb_ref, vb_ref, cb_ref, sb_ref, ob_ref))
    n_sub = of_ref.shape[1] // c
    chains = [(d, h) for d in range(2) for h in range(RET_HEADS)]
    sls = [slice(h * HEAD_DIM, (h + 1) * HEAD_DIM) for h in range(RET_HEADS)]
    order = [(j, n_sub - 1 - j) for j in range(n_sub)]
    rows = [[slice(u * c, (u + 1) * c) for u in sub] for sub in order]
    qs, ks, vs, qk, kv = {}, {}, {}, {}, {}
    for j in range(n_sub):
        for i, (d, h) in enumerate(chains):
            r = rows[j][d]
            tab = (dirs[d][3][r, :], dirs[d][4][r, :])
            qs[j, i] = _rope(dirs[d][0][0, r, sls[h]].astype(F32), *tab)
            ks[j, i] = _rope(dirs[d][1][0, r, sls[h]].astype(F32), *tab) * QK_SCALE
            vs[j, i] = dirs[d][2][0, r, sls[h]]
    for key in qs:
        qk[key] = (_dot_nt(qs[key].astype(BF16), ks[key].astype(BF16)) * dm_ref[key[1]]).astype(BF16)
        kv[key] = _dot((ks[key] * kd_ref[key[1]]).T.astype(BF16), vs[key])
    state = [s_ref[i] for i in range(len(chains))]
    for j in range(n_sub):
        o_inter = [_dot((qs[j, i] * qd_ref[i]).astype(BF16), s.astype(BF16)) for i, s in enumerate(state)]
        for i, (d, h) in enumerate(chains):
            dirs[d][5][0, rows[j][d], sls[h]] = (_dot(qk[j, i], vs[j, i]) + o_inter[i]).astype(dirs[d][5].dtype)
            chunk_decay = jnp.exp(jnp.full((1, HEAD_DIM), float(c), F32) * lg_ref[d, h])
            state[i] = state[i] * chunk_decay + kv[j, i]
    for i, st in enumerate(state):
        s_ref[i] = st


RET_STEP_CHUNKS = 2


def _ret_call(log_gamma, p_main, cos2, sin2, *, l_lat):
    b, l_all, _ = p_main.shape
    c = RET_CHUNK
    blk = RET_STEP_CHUNKS * c
    n_all, n_lat = l_all // blk, l_lat // blk
    n_ctx = n_all - n_lat

    def fwd(s):
        return jnp.where(s < n_ctx, n_lat + s, s - n_ctx)

    def bwd(s):
        return n_all - 1 - s

    def pspec(colblk, order):
        return pl.BlockSpec((1, blk, RET_W), lambda bb, s: (bb, order(s), colblk))

    def tspec(order):
        return pl.BlockSpec((blk, HEAD_DIM), lambda bb, s: (order(s), 0))

    in_specs = [pl.BlockSpec(memory_space=pltpu.SMEM)]
    for order in (fwd, bwd):
        in_specs += [pspec(COL_RET_Q // RET_W, order), pspec(COL_RET_K // RET_W, order),
                     pspec(COL_RET_V // RET_W, order), tspec(order), tspec(order)]
    nhd = 2 * RET_HEADS
    return pl.pallas_call(
        _ret_kernel,
        grid=(b, n_all),
        in_specs=in_specs,
        out_specs=[pl.BlockSpec((1, blk, RET_W), lambda bb, s: (bb, fwd(s), 0)),
                   pl.BlockSpec((1, blk, RET_W), lambda bb, s: (bb, bwd(s), 0))],
        out_shape=[jax.ShapeDtypeStruct((b, l_all, RET_W), BF16)] * 2,
        scratch_shapes=[pltpu.VMEM((nhd, HEAD_DIM, HEAD_DIM), F32), pltpu.VMEM((nhd, c, c), F32),
                        pltpu.VMEM((nhd, c, HEAD_DIM), F32), pltpu.VMEM((nhd, c, HEAD_DIM), F32)],
        compiler_params=_params(("arbitrary", "arbitrary")),
        name="retention_scan",
    )(log_gamma, p_main, p_main, p_main, cos2, sin2, p_main, p_main, p_main, cos2, sin2)


def _attprep_kernel(q_ref, k_ref, v_ref, c_ref, s_ref, qw_ref, kw_ref, qo_ref, ko_ref, vo_ref):
    cos2 = c_ref[...]
    sin2 = s_ref[...]

    def norm_rope(x, w):
        xf = x.astype(F32)
        y = xf * lax.rsqrt(jnp.mean(xf * xf, -1, keepdims=True) + EPS) * w
        return _rope(y, cos2, sin2)

    for h in range(ATT_HEADS):
        sl = slice(h * HEAD_DIM, (h + 1) * HEAD_DIM)
        qo_ref[0, :, sl] = (norm_rope(q_ref[0, :, sl], qw_ref[...]) * (QK_SCALE * LOG2_E)).astype(BF16)
    for h in range(ATT_KV_HEADS):
        sl = slice(h * HEAD_DIM, (h + 1) * HEAD_DIM)
        ko_ref[0, sl, :] = norm_rope(k_ref[0, :, sl], kw_ref[...]).T.astype(BF16)
        vo_ref[0, :, 2 * h * HEAD_DIM:(2 * h + 1) * HEAD_DIM] = v_ref[0, :, sl]
        vo_ref[0, :, (2 * h + 1) * HEAD_DIM:(2 * h + 2) * HEAD_DIM] = jnp.ones((v_ref.shape[1], HEAD_DIM), BF16)


def _attprep_call(p_main, cos2, sin2, qn_w, kn_w, *, tm):
    b, l_all, _ = p_main.shape
    return pl.pallas_call(
        _attprep_kernel,
        grid=(b, l_all // tm),
        in_specs=[pl.BlockSpec((1, tm, ATT_W), lambda bb, i: (bb, i, COL_ATT_Q // ATT_W)),
                  pl.BlockSpec((1, tm, ATT_KV_W), lambda bb, i: (bb, i, COL_ATT_K // ATT_KV_W)),
                  pl.BlockSpec((1, tm, ATT_KV_W), lambda bb, i: (bb, i, COL_ATT_V // ATT_KV_W)),
                  pl.BlockSpec((tm, HEAD_DIM), lambda bb, i: (i, 0)),
                  pl.BlockSpec((tm, HEAD_DIM), lambda bb, i: (i, 0)),
                  pl.BlockSpec((1, HEAD_DIM), lambda bb, i: (0, 0)),
                  pl.BlockSpec((1, HEAD_DIM), lambda bb, i: (0, 0))],
        out_specs=[pl.BlockSpec((1, tm, ATT_W), lambda bb, i: (bb, i, 0)),
                   pl.BlockSpec((1, ATT_KV_W, tm), lambda bb, i: (bb, 0, i)),
                   pl.BlockSpec((1, tm, 2 * ATT_KV_W), lambda bb, i: (bb, i, 0))],
        out_shape=[jax.ShapeDtypeStruct((b, l_all, ATT_W), BF16),
                   jax.ShapeDtypeStruct((b, ATT_KV_W, l_all), BF16),
                   jax.ShapeDtypeStruct((b, l_all, 2 * ATT_KV_W), BF16)],
        compiler_params=_params(("arbitrary", "arbitrary")),
        name="attn_prep",
    )(p_main, p_main, p_main, cos2, sin2, qn_w, kn_w)


def _att_kernel(*refs, tk):
    q_ref, kt_ref, v_ref = refs[0], refs[1], refs[2]
    o_ref, s_ref, mx_ref, m_ref, acc_ref = refs[-5], refs[-4], refs[-3], refs[-2], refs[-1]
    tq = q_ref.shape[1]
    n = kt_ref.shape[2] // tk
    lanes = tk // HEAD_DIM
    rb = 64

    def put_scores(c):
        col = pl.multiple_of(c * tk, tk)
        s_ref[c] = _dot(q_ref[0], kt_ref[0, :, pl.ds(col, tk)])

    def fold_max(c):
        for r in range(tq // rb):
            rows = slice(r * rb, (r + 1) * rb)
            acc = mx_ref[rows, :]
            for j in range(lanes):
                acc = jnp.maximum(acc, s_ref[c, rows, j * HEAD_DIM:(j + 1) * HEAD_DIM])
            mx_ref[rows, :] = acc

    mx_ref[...] = jnp.full(mx_ref.shape, -jnp.inf, F32)
    put_scores(0)

    @pl.loop(1, n)
    def _(c):
        put_scores(c)
        fold_max(c - 1)

    fold_max(n - 1)
    m_ref[...] = jnp.max(mx_ref[...], -1, keepdims=True)

    acc_ref[...] = jnp.zeros_like(acc_ref)

    @pl.loop(0, n)
    def _(c):
        row = pl.multiple_of(c * tk, tk)
        p = jnp.exp2((s_ref[c] - m_ref[...]).astype(BF16))
        acc_ref[...] += _dot(p, v_ref[0, pl.ds(row, tk), :])

    o_ref[0] = (acc_ref[:, :HEAD_DIM] / acc_ref[:, HEAD_DIM:]).astype(BF16)


def _key_chunk(n_keys):
    for tk in (4224, 768, 512, 256):
        if n_keys % tk == 0:
            return tk
    raise ValueError("key count must be a multiple of 256")


def _att_call(qn, kn, v1, *, l_lat, tq):
    b, l_all, _ = qn.shape
    l_ctx = l_all - l_lat
    group = ATT_HEADS // ATT_KV_HEADS
    tk = _key_chunk(l_all)
    y_lat = pl.pallas_call(
        functools.partial(_att_kernel, tk=tk),
        grid=(b, ATT_HEADS, l_lat // tq),
        in_specs=[pl.BlockSpec((1, tq, HEAD_DIM), lambda bb, h, i: (bb, i, h)),
                  pl.BlockSpec((1, HEAD_DIM, l_all), lambda bb, h, i: (bb, h // group, 0),
                               pipeline_mode=pl.Buffered(1)),
                  pl.BlockSpec((1, l_all, 2 * HEAD_DIM), lambda bb, h, i: (bb, 0, h // group),
                               pipeline_mode=pl.Buffered(1))],
        out_specs=pl.BlockSpec((1, tq, HEAD_DIM), lambda bb, h, i: (bb, i, h)),
        out_shape=jax.ShapeDtypeStruct((b, l_all, ATT_W), BF16),
        scratch_shapes=[pltpu.VMEM((l_all // tk, tq, tk), F32), pltpu.VMEM((tq, HEAD_DIM), F32), pltpu.VMEM((tq, 1), F32),
                        pltpu.VMEM((tq, 2 * HEAD_DIM), F32)],
        compiler_params=_params(("arbitrary", "arbitrary", "arbitrary")),
        name="attention",
    )(qn, kn, v1)
    ctx_blk = l_lat // l_ctx
    return pl.pallas_call(
        functools.partial(_att_kernel, tk=l_ctx),
        grid=(b, ATT_HEADS),
        in_specs=[pl.BlockSpec((1, l_ctx, HEAD_DIM), lambda bb, h: (bb, ctx_blk, h)),
                  pl.BlockSpec((1, HEAD_DIM, l_ctx), lambda bb, h: (bb, h // group, ctx_blk)),
                  pl.BlockSpec((1, l_ctx, 2 * HEAD_DIM), lambda bb, h: (bb, ctx_blk, h // group)),
                  pl.BlockSpec(memory_space=pl.ANY)],
        out_specs=pl.BlockSpec((1, l_ctx, HEAD_DIM), lambda bb, h: (bb, ctx_blk, h)),
        out_shape=jax.ShapeDtypeStruct((b, l_all, ATT_W), BF16),
        scratch_shapes=[pltpu.VMEM((1, l_ctx, l_ctx), F32), pltpu.VMEM((l_ctx, HEAD_DIM), F32), pltpu.VMEM((l_ctx, 1), F32),
                        pltpu.VMEM((l_ctx, 2 * HEAD_DIM), F32)],
        input_output_aliases={3: 0},
        compiler_params=_params(("arbitrary", "arbitrary")),
        name="attention_ctx",
    )(qn, kn, v1, y_lat)


def _dnprep_kernel(*refs, tm, l_lat, l_all):
    mains, prevs, nexts = refs[0:3], refs[3:6], refs[6:9]
    cw_ref, pab_ref, alog_ref, dtb_ref = refs[9:13]
    outs = refs[13:16]
    g_ref, ext_ref = refs[16], refs[17]
    i = pl.program_id(1)
    halo = 16
    pad = DN_CONV_K // 2
    first = jnp.logical_or(i == 0, i == l_lat // tm)
    last = jnp.logical_or(i == l_lat // tm - 1, i == l_all // tm - 1)

    keep_prev = jnp.where(first, 0.0, 1.0)
    keep_next = jnp.where(last, 0.0, 1.0)

    for part in range(3):
        ext_ref[0:halo, :] = prevs[part][0].astype(F32) * keep_prev
        ext_ref[halo:halo + tm, :] = mains[part][0].astype(F32)
        ext_ref[halo + tm:, :] = nexts[part][0].astype(F32) * keep_next
        acc = jnp.zeros((tm, DN_W), F32)
        for j in range(DN_CONV_K):
            w_j = cw_ref[j:j + 1, part * DN_W:(part + 1) * DN_W]
            acc = acc + w_j * ext_ref[pl.ds(halo - pad + j, tm), :]
        y = _silu(acc)
        for h in range(DN_HEADS):
            sl = slice(h * HEAD_DIM, (h + 1) * HEAD_DIM)
            yh = y[:, sl]
            if part < 2:
                yh = yh * lax.rsqrt(jnp.sum(yh * yh, -1, keepdims=True) + EPS)
            if part == 0:
                yh = yh * QK_SCALE
            outs[part][0, :, sl] = yh.astype(BF16)

    a = pab_ref[0]
    col = lax.broadcasted_iota(jnp.int32, a.shape, 1)
    z = a + dtb_ref[...]
    softplus = jnp.maximum(z, 0.0) + jnp.log(1.0 + jnp.exp(-jnp.abs(z)))
    g = -jnp.exp(alog_ref[...]) * softplus
    ii = lax.broadcasted_iota(jnp.int32, (tm, tm), 0)
    rr = lax.broadcasted_iota(jnp.int32, (tm, tm), 1)
    same_chunk = (ii // DN_CHUNK) == (rr // DN_CHUNK)
    prefix = jnp.logical_and(same_chunk, rr <= ii).astype(F32)
    suffix = jnp.logical_and(same_chunk, rr >= ii).astype(F32)
    hi = lax.Precision.HIGHEST
    cum_f = jnp.dot(prefix, g, preferred_element_type=F32, precision=hi)
    cum_b = jnp.dot(suffix, g, preferred_element_type=F32, precision=hi)
    g_ref[0] = jnp.where(col < DN_HEADS, cum_f, jnp.where(col < N_DN_GATES, cum_b, jax.nn.sigmoid(a)))


def _dnprep_call(p_main, p_ab, conv_w, alog_row, dtb_row, *, l_lat, tm):
    b, l_all, _ = p_main.shape
    halo = 16
    r = tm // halo
    cb = COL_DN_QKV // DN_W
    in_specs = []
    for part in range(3):
        in_specs.append(pl.BlockSpec((1, tm, DN_W), lambda bb, i, p=part: (bb, i, cb + p)))
    for part in range(3):
        in_specs.append(pl.BlockSpec((1, halo, DN_W), lambda bb, i, p=part: (bb, jnp.maximum(i * r - 1, 0), cb + p)))
    for part in range(3):
        in_specs.append(pl.BlockSpec(
            (1, halo, DN_W), lambda bb, i, p=part: (bb, jnp.minimum((i + 1) * r, l_all // halo - 1), cb + p)))
    in_specs += [pl.BlockSpec((DN_CONV_K, 3 * DN_W), lambda bb, i: (0, 0)),
                 pl.BlockSpec((1, tm, GATE_COLS), lambda bb, i: (bb, i, 0)),
                 pl.BlockSpec((1, GATE_COLS), lambda bb, i: (0, 0)),
                 pl.BlockSpec((1, GATE_COLS), lambda bb, i: (0, 0))]
    kern = functools.partial(_dnprep_kernel, tm=tm, l_lat=l_lat, l_all=l_all)
    return pl.pallas_call(
        kern,
        grid=(b, l_all // tm),
        in_specs=in_specs,
        out_specs=[pl.BlockSpec((1, tm, DN_W), lambda bb, i: (bb, i, 0))] * 3
        + [pl.BlockSpec((1, tm, GATE_COLS), lambda bb, i: (bb, i, 0))],
        out_shape=[jax.ShapeDtypeStruct((b, l_all, DN_W), BF16)] * 3
        + [jax.ShapeDtypeStruct((b, l_all, GATE_COLS), F32)],
        scratch_shapes=[pltpu.VMEM((tm + 2 * halo, DN_W), F32)],
        compiler_params=_params(("arbitrary", "arbitrary")),
        name="deltanet_conv_gates",
    )(*([p_main] * 9), conv_w, p_ab, alog_row, dtb_row)


def _neumann_inverse_many(a_list, eye):
    ps = [-a for a in a_list]
    ts = [eye + p for p in ps]
    for _ in range(5):
        pbs = [p.astype(BF16) for p in ps]
        ps = [_dot(pb, pb) for pb in pbs]
        ts = [t + _dot(t.astype(BF16), p.astype(BF16)) for t, p in zip(ts, ps)]
    return ts


def _dnchunk_kernel(q_ref, k_ref, v_ref, g_ref, gt_ref, w_ref, kc_ref, qg_ref, kgt_ref, qk_ref, gl_ref):
    c = DN_CHUNK
    n_sub = q_ref.shape[1] // c
    ii = lax.broadcasted_iota(jnp.int32, (c, c), 0)
    jj = lax.broadcasted_iota(jnp.int32, (c, c), 1)
    eye = (ii == jj).astype(F32)

    qk_ref[...] = jnp.zeros_like(qk_ref)
    kgt_ref[...] = jnp.zeros_like(kgt_ref)
    gl_ref[...] = jnp.zeros_like(gl_ref)

    heads = range(DN_HEADS)
    subs = range(n_sub)
    chains = [(u, d, h) for u in subs for h in heads for d in range(2)]
    rows = [slice(u * c, (u + 1) * c) for u in subs]
    sls = [slice(h * HEAD_DIM, (h + 1) * HEAD_DIM) for h in heads]
    gcol = [g_ref[0, rows[u], :] for u in subs]
    grow = [gt_ref[0, u] for u in subs]
    q16 = {(u, h): q_ref[0, rows[u], sls[h]] for u in subs for h in heads}
    k16 = {(u, h): k_ref[0, rows[u], sls[h]] for u in subs for h in heads}
    kk = {uh: _dot_nt(k16[uh], k16[uh]) for uh in k16}
    qk0 = {uh: _dot_nt(q16[uh], k16[uh]) for uh in k16}

    gcc, tot, beta, decay, a_list = {}, {}, {}, {}, []
    for u, d, h in chains:
        col = d * DN_HEADS + h
        key = (u, d, h)
        gcc[key] = gcol[u][:, col:col + 1]
        gcr = grow[u][col:col + 1, :]
        tot[key] = gcol[u][c - 1:c, col:col + 1] if d == 0 else gcol[u][0:1, col:col + 1]
        beta[key] = gcol[u][:, N_DN_GATES + col:N_DN_GATES + col + 1]
        incl = (ii >= jj) if d == 0 else (ii <= jj)
        strict = (ii > jj) if d == 0 else (ii < jj)
        decay[key] = jnp.where(incl, jnp.exp(jnp.where(incl, gcc[key] - gcr, 0.0)), 0.0)
        a_list.append(jnp.where(strict, kk[u, h] * beta[key] * decay[key], 0.0))
        gl_ref[0, d, u, h:h + 1, :] = jnp.broadcast_to(jnp.exp(tot[key]), (1, HEAD_DIM))

    t16 = [t.astype(BF16) for t in _neumann_inverse_many(a_list, eye)]

    for key, t in zip(chains, t16):
        u, d, h = key
        sl = sls[h]
        k = k16[u, h].astype(F32)
        e_col = jnp.exp(gcc[key])
        rhs = jnp.concatenate([(v_ref[0, rows[u], sl].astype(F32) * beta[key]).astype(BF16),
                               (k * (beta[key] * e_col)).astype(BF16)], axis=1)
        wk = _dot(t, rhs)
        w_ref[0, d, rows[u], sl] = wk[:, :HEAD_DIM]
        kc_ref[0, d, rows[u], sl] = wk[:, HEAD_DIM:].astype(BF16)
        qg_ref[0, d, rows[u], sl] = (q16[u, h].astype(F32) * e_col).astype(BF16)
        kg = k * jnp.exp(tot[key] - gcc[key])
        kgt_ref[0, d, u, :, pl.ds(h * HEAD_DIM, c)] = kg.T.astype(BF16)
        qk_ref[0, d, rows[u], pl.ds(h * HEAD_DIM, c)] = (qk0[u, h] * decay[key]).astype(BF16)


DN_STEP_CHUNKS = 4


def _dnchunk_call(qd, kd, vd, g, gt):
    b, l_all, _ = qd.shape
    c = DN_CHUNK
    n = l_all // c
    ns = DN_STEP_CHUNKS
    tok = lambda dt: jax.ShapeDtypeStruct((b, 2, l_all, DN_W), dt)
    tok_spec = pl.BlockSpec((1, 2, ns * c, DN_W), lambda bb, s: (bb, 0, s, 0))
    return pl.pallas_call(
        _dnchunk_kernel,
        grid=(b, n // ns),
        in_specs=[pl.BlockSpec((1, ns * c, DN_W), lambda bb, s: (bb, s, 0))] * 3
        + [pl.BlockSpec((1, ns * c, GATE_COLS), lambda bb, s: (bb, s, 0)),
           pl.BlockSpec((1, ns, 2 * N_DN_GATES, c), lambda bb, s: (bb, s, 0, 0))],
        out_specs=[tok_spec, tok_spec, tok_spec,
                   pl.BlockSpec((1, 2, ns, HEAD_DIM, DN_W), lambda bb, s: (bb, 0, s, 0, 0)),
                   tok_spec,
                   pl.BlockSpec((1, 2, ns, 8, HEAD_DIM), lambda bb, s: (bb, 0, s, 0, 0))],
        out_shape=[tok(F32), tok(BF16), tok(BF16),
                   jax.ShapeDtypeStruct((b, 2, n, HEAD_DIM, DN_W), BF16),
                   tok(BF16),
                   jax.ShapeDtypeStruct((b, 2, n, 8, HEAD_DIM), F32)],
        compiler_params=_params(("arbitrary", "arbitrary")),
        name="deltanet_chunk_factors",
    )(qd, kd, vd, g, gt)


def _dnscan_kernel(*refs):
    ins_f, ins_b = refs[0:6], refs[6:12]
    of_ref, ob_ref, s_ref = refs[12], refs[13], refs[14]
    c = DN_CHUNK
    n_sub = of_ref.shape[1] // c
    step = pl.program_id(1)

    @pl.when(step == 0)
    def _():
        s_ref[...] = jnp.zeros_like(s_ref)

    ins, outs = (ins_f, ins_b), (of_ref, ob_ref)
    chains = [(d, h) for d in range(2) for h in range(DN_HEADS)]
    sls = [slice(h * HEAD_DIM, (h + 1) * HEAD_DIM) for h in range(DN_HEADS)]
    state = [s_ref[d * DN_HEADS + h] for d, h in chains]
    for j in range(n_sub):
        sub = [j, n_sub - 1 - j]
        rows = [slice(u * c, (u + 1) * c) for u in sub]
        s16 = [s.astype(BF16) for s in state]
        v16 = [(ins[d][0][0, 0, rows[d], sls[h]] - _dot(ins[d][1][0, 0, rows[d], sls[h]], s)).astype(BF16)
               for (d, h), s in zip(chains, s16)]
        o_inter = [_dot(ins[d][2][0, 0, rows[d], sls[h]], s) for (d, h), s in zip(chains, s16)]
        new_state = []
        for (d, h), s, v, oi in zip(chains, state, v16, o_inter):
            qk = ins[d][4][0, 0, rows[d], pl.ds(h * HEAD_DIM, c)]
            outs[d][0, rows[d], sls[h]] = (oi + _dot(qk, v)).astype(outs[d].dtype)
            kgt = ins[d][3][0, 0, sub[d], :, pl.ds(h * HEAD_DIM, c)]
            new_state.append(s * ins[d][5][0, 0, sub[d], h:h + 1, :] + _dot(kgt, v))
        state = new_state
    for i, s in enumerate(state):
        s_ref[i] = s


def _dnscan_call(w, kc, qg, kgt, qk, gl, *, l_lat):
    b, _, l_all, _ = w.shape
    ns = DN_STEP_CHUNKS
    blk = ns * DN_CHUNK
    n_all, n_lat = l_all // blk, l_lat // blk
    n_ctx = n_all - n_lat

    def fwd(s):
        return jnp.where(s < n_ctx, n_lat + s, s - n_ctx)

    def bwd(s):
        return n_all - 1 - s

    in_specs = []
    for d, order in enumerate((fwd, bwd)):
        tok_spec = pl.BlockSpec((1, 1, blk, DN_W), lambda bb, s, d=d, o=order: (bb, d, o(s), 0))
        in_specs += [tok_spec, tok_spec, tok_spec,
                     pl.BlockSpec((1, 1, ns, HEAD_DIM, DN_W), lambda bb, s, d=d, o=order: (bb, d, o(s), 0, 0)),
                     tok_spec,
                     pl.BlockSpec((1, 1, ns, 8, HEAD_DIM), lambda bb, s, d=d, o=order: (bb, d, o(s), 0, 0))]
    return pl.pallas_call(
        _dnscan_kernel,
        grid=(b, n_all),
        in_specs=in_specs,
        out_specs=[pl.BlockSpec((1, blk, DN_W), lambda bb, s: (bb, fwd(s), 0)),
                   pl.BlockSpec((1, blk, DN_W), lambda bb, s: (bb, bwd(s), 0))],
        out_shape=[jax.ShapeDtypeStruct((b, l_all, DN_W), BF16)] * 2,
        scratch_shapes=[pltpu.VMEM((2 * DN_HEADS, HEAD_DIM, HEAD_DIM), F32)],
        compiler_params=_params(("arbitrary", "arbitrary")),
        name="deltanet_scan",
    )(w, kc, qg, kgt, qk, gl, w, kc, qg, kgt, qk, gl)


def _rope_tables(l_lat, l_ctx):
    rows = l_lat // GRID_W
    n_freq = HEAD_DIM // 4
    inv = ROPE_THETA ** (-jnp.arange(n_freq, dtype=F32) / n_freq)
    row_ang = jnp.arange(rows, dtype=F32)[:, None] * inv
    col_ang = jnp.arange(GRID_W, dtype=F32)[:, None] * inv

    def table(fn):
        return jnp.concatenate([jnp.repeat(fn(row_ang), GRID_W, axis=0), jnp.tile(fn(col_ang), (rows, 1))], -1)

    cos, sin = table(jnp.cos), table(jnp.sin)
    cos2 = jnp.concatenate([cos, cos], -1)
    sin2 = jnp.concatenate([-sin, sin], -1)
    cos2 = jnp.concatenate([cos2, jnp.ones((l_ctx, HEAD_DIM), F32)], 0)
    sin2 = jnp.concatenate([sin2, jnp.zeros((l_ctx, HEAD_DIM), F32)], 0)
    return cos2, sin2


def _pad_row(v, width):
    v = v.reshape(1, -1).astype(F32)
    return jnp.pad(v, ((0, 0), (0, width - v.shape[1])))


def _row_tile(l_lat, l_ctx):
    l_all = l_lat + l_ctx
    for tm in (768, 512, 256, 128):
        if l_all % tm == 0:
            return tm
    raise ValueError("token count must be a multiple of 128")


def kernel(x, c, ctx, c_ctx, w_ada, b_ada, w_in, ret_decay_logit, dn_conv_w, dn_a_log, dn_dt_bias, dn_norm_w,
           att_qn_w, att_kn_w, w_o, ln1_w, ln1_b, w_ffn_in, w_ffn_out, ln2_w, ln2_b):
    bsz, l_lat, d = x.shape
    l_ctx = ctx.shape[1]
    l_all = l_lat + l_ctx
    depth = w_ada.shape[0]
    assert l_lat % 256 == 0 and l_ctx % 256 == 0 and l_lat % GRID_W == 0
    tm = _row_tile(l_lat, l_ctx)
    tn = MAIN_W // 2

    cos2, sin2 = _rope_tables(l_lat, l_ctx)

    cond_rows = 8 * pl.cdiv(bsz + 1, 8)
    cond = jnp.concatenate([c, c_ctx[None, :]], 0)
    cond = jnp.pad(cond, ((0, cond_rows - bsz - 1), (0, 0)))
    mod = _ada_call(cond, w_ada, b_ada)

    w_main, w_ab = _repack_call(jnp.pad(w_in.astype(BF16), ((0, 0), (0, 0), (0, -w_in.shape[2] % HEAD_DIM))))
    w_o16, w_ffn_in16, w_ffn_out16 = w_o.astype(BF16), w_ffn_in.astype(BF16), w_ffn_out.astype(BF16)
    for i in range(depth):
        last = i == depth - 1
        ml = mod[i, :bsz].reshape(bsz, 6, d)
        mc = mod[i, bsz].reshape(1, 6, d)
        if i == 0:
            p_main, p_ab, x_all = _inproj_call((x, ctx), ml, mc, w_main, w_ab, layer=i, l_lat=l_lat, tm=tm, tn=tn // 2)
        else:
            p_main, p_ab = _inproj_call(x_all, ml, mc, w_main, w_ab, layer=i, l_lat=l_lat, tm=tm, tn=tn)

        log_gamma = jax.nn.log_sigmoid(ret_decay_logit[i].astype(F32))
        ret_f, ret_b = _ret_call(log_gamma, p_main, cos2, sin2, l_lat=l_lat)

        qn, kn, v1 = _attprep_call(p_main, cos2, sin2, att_qn_w[i].reshape(1, -1), att_kn_w[i].reshape(1, -1),
                                   tm=tm)
        y_att = _att_call(qn, kn, v1, l_lat=l_lat, tq=1024 if l_lat % 1024 == 0 else 256)

        qd, kd, vd, g = _dnprep_call(p_main, p_ab, dn_conv_w[i], _pad_row(dn_a_log[i], GATE_COLS),
                                     _pad_row(dn_dt_bias[i], GATE_COLS), l_lat=l_lat, tm=256)
        gt = g[:, :, :2 * N_DN_GATES].reshape(bsz, l_all // DN_CHUNK, DN_CHUNK, 2 * N_DN_GATES)
        gt = jnp.swapaxes(gt, 2, 3)
        dn_f, dn_b = _dnscan_call(*_dnchunk_call(qd, kd, vd, g, gt), l_lat=l_lat)

        x_all = _wo_call(ret_f, ret_b, dn_f, dn_b, p_main, dn_norm_w[i].reshape(1, -1), y_att, x_all, ml, mc,
                         w_o16, ln1_w[i].reshape(1, -1), ln1_b[i].reshape(1, -1), layer=i, l_lat=l_lat, tm=tm)
        x_all = _ffn_call(x_all, ml, mc, w_ffn_in16, w_ffn_out16, ln2_w[i].reshape(1, -1), ln2_b[i].reshape(1, -1),
                          layer=i, l_lat=l_lat, out_rows=l_lat if last else l_all, tm=tm, tf=512)
    return x_all
```

```python
import functools

import jax
import jax.numpy as jnp
from jax import lax
from jax.experimental import pallas as pl
from jax.experimental.pallas import tpu as pltpu

F32 = jnp.float32
BF16 = jnp.bfloat16

HEAD_DIM = 128
RET_HEADS = 4
DN_HEADS = 4
ATT_HEADS = 8
ATT_KV_HEADS = 2
RET_W = RET_HEADS * HEAD_DIM
DN_W = DN_HEADS * HEAD_DIM
ATT_W = ATT_HEADS * HEAD_DIM
ATT_KV_W = ATT_KV_HEADS * HEAD_DIM
RET_CHUNK = 128
DN_CHUNK = 64
DN_CONV_K = 5
GRID_W = 64
ROPE_THETA = 10000.0
MODEL_DEPTH = 4
DEEPNORM_ALPHA = (2 * MODEL_DEPTH) ** 0.25
EPS = 1e-6
QK_SCALE = HEAD_DIM ** -0.5
LOG2_E = 1.4426950408889634

COL_RET_Q = 0
COL_RET_K = RET_W
COL_RET_V = 2 * RET_W
COL_RET_G = 3 * RET_W
COL_DN_QKV = 4 * RET_W
COL_DN_Z = COL_DN_QKV + 3 * DN_W
COL_ATT_Q = COL_DN_Z + DN_W
COL_ATT_K = COL_ATT_Q + ATT_W
COL_ATT_V = COL_ATT_K + ATT_KV_W
MAIN_W = COL_ATT_V + ATT_KV_W
GATE_COLS = 128
N_DN_GATES = 2 * DN_HEADS

VMEM_LIMIT_MB = 56


def _params(sem, vmem_mb=VMEM_LIMIT_MB):
    return pltpu.CompilerParams(dimension_semantics=sem, vmem_limit_bytes=vmem_mb * 1024 * 1024)


def _dot(a, b):
    return jnp.dot(a, b, preferred_element_type=F32)


def _dot_nt(a, b):
    return lax.dot_general(a, b, (((1,), (1,)), ((), ())), preferred_element_type=F32)


def _silu(x):
    return x * jax.nn.sigmoid(x)


def _is_ctx_rows(i, tm, l_lat):
    rows = i * tm + lax.broadcasted_iota(jnp.int32, (tm, 1), 0)
    return rows >= l_lat


def _mod_row(is_ctx, ml_ref, mc_ref, k):
    return jnp.where(is_ctx, mc_ref[0, k:k + 1, :], ml_ref[0, k:k + 1, :])


def _layer_norm(r, w, b):
    mu = jnp.mean(r, -1, keepdims=True)
    rc = r - mu
    var = jnp.mean(rc * rc, -1, keepdims=True)
    return rc * lax.rsqrt(var + EPS) * w + b


def _split_bf16(a):
    hi = a.astype(BF16)
    return hi, (a - hi.astype(F32)).astype(BF16)


def _ada_kernel(c_ref, w_ref, b_ref, o_ref):
    h_hi, h_lo = _split_bf16(_silu(c_ref[...]))
    w_hi, w_lo = _split_bf16(w_ref[0])
    o_ref[0] = _dot(h_hi, w_hi) + _dot(h_hi, w_lo) + _dot(h_lo, w_hi) + b_ref[0]


def _ada_call(cond, w_ada, b_ada):
    depth, d, n6 = w_ada.shape
    rows = cond.shape[0]
    tn = 1024
    return pl.pallas_call(
        _ada_kernel,
        grid=(depth, n6 // tn),
        in_specs=[pl.BlockSpec((rows, d), lambda l, j: (0, 0)),
                  pl.BlockSpec((1, d, tn), lambda l, j: (l, 0, j)),
                  pl.BlockSpec((1, 1, tn), lambda l, j: (l, 0, j))],
        out_specs=pl.BlockSpec((1, rows, tn), lambda l, j: (l, 0, j)),
        out_shape=jax.ShapeDtypeStruct((depth, rows, n6), F32),
        compiler_params=_params(("arbitrary", "arbitrary")),
        name="ada_mod",
    )(cond, w_ada, b_ada.reshape(depth, 1, n6))


GATE_LO = COL_DN_Z + DN_W
INPROJ_SUB_ROWS = 256


def _repack_kernel(w_ref, wm_ref, wab_ref):
    wm_ref[0, :, :GATE_LO] = w_ref[0, :, :GATE_LO].astype(BF16)
    wm_ref[0, :, GATE_LO:] = w_ref[0, :, GATE_LO + 2 * N_DN_GATES:MAIN_W + 2 * N_DN_GATES].astype(BF16)
    gates = w_ref[0, :, GATE_LO:GATE_LO + GATE_COLS]
    lane = lax.broadcasted_iota(jnp.int32, gates.shape, 1)
    wab_ref[0] = jnp.where(lane < 2 * N_DN_GATES, gates, jnp.zeros_like(gates)).astype(BF16)


def _repack_call(w_in):
    depth, d, proj_w = w_in.shape
    assert proj_w >= MAIN_W + 2 * N_DN_GATES
    tr = 256
    return pl.pallas_call(
        _repack_kernel,
        grid=(depth, d // tr),
        in_specs=[pl.BlockSpec((1, tr, proj_w), lambda l, r: (l, r, 0))],
        out_specs=[pl.BlockSpec((1, tr, MAIN_W), lambda l, r: (l, r, 0)),
                   pl.BlockSpec((1, tr, GATE_COLS), lambda l, r: (l, r, 0))],
        out_shape=[jax.ShapeDtypeStruct((depth, d, MAIN_W), BF16), jax.ShapeDtypeStruct((depth, d, GATE_COLS), BF16)],
        compiler_params=_params(("arbitrary", "arbitrary")),
        name="repack_w_in",
    )(w_in)


def _inproj_kernel(*refs, tm, l_lat, split_input):
    if split_input:
        x_ref, c_ref, ml_ref, mc_ref, w_ref, wab_ref, pm_ref, pab_ref, xo_ref, h_ref = refs
    else:
        x_ref, ml_ref, mc_ref, w_ref, wab_ref, pm_ref, pab_ref, h_ref = refs
    i = pl.program_id(1)
    j = pl.program_id(2)
    tn = w_ref.shape[1]
    step = 512

    def project(rows):
        for lo in range(0, tn, step):
            hi = min(lo + step, tn)
            pm_ref[0, rows, lo:hi] = _dot(h_ref[rows, :], w_ref[:, lo:hi]).astype(BF16)

    @pl.when(j == 0)
    def _():
        if split_input:
            n_full, rem = divmod(l_lat, tm)

            @pl.when(i < n_full)
            def _():
                xo_ref[0] = x_ref[0]

            @pl.when(i >= n_full)
            def _():
                if rem:
                    xo_ref[0, :rem, :] = x_ref[0, :rem, :]
                xo_ref[0, rem:, :] = c_ref[0]

        src = xo_ref if split_input else x_ref
        for lo in range(0, tm, INPROJ_SUB_ROWS):
            rows = slice(lo, lo + INPROJ_SUB_ROWS)
            rid = i * tm + lo + lax.broadcasted_iota(jnp.int32, (INPROJ_SUB_ROWS, 1), 0)
            shift = _mod_row(rid >= l_lat, ml_ref, mc_ref, 0)
            scale = _mod_row(rid >= l_lat, ml_ref, mc_ref, 1)
            h = (src[0, rows, :] * (1.0 + scale) + shift).astype(BF16)
            h_ref[rows, :] = h
            pab_ref[0, rows, :] = _dot(h, wab_ref[...])
            project(rows)

    @pl.when(j > 0)
    def _():
        project(slice(0, tm))


def _inproj_call(xs, ml, mc, w_main, w_ab, *, layer, l_lat, tm, tn):
    split_input = isinstance(xs, tuple)
    xs = list(xs) if split_input else [xs]
    b, _, d = xs[0].shape
    l_all = l_lat + xs[1].shape[1] if split_input else xs[0].shape[1]
    x_spec = pl.BlockSpec((1, tm, d), lambda bb, i, j: (bb, i, 0))
    in_specs = [x_spec]
    out_specs = [pl.BlockSpec((1, tm, tn), lambda bb, i, j: (bb, i, j)),
                 pl.BlockSpec((1, tm, GATE_COLS), lambda bb, i, j: (bb, i, 0))]
    out_shape = [jax.ShapeDtypeStruct((b, l_all, MAIN_W), BF16), jax.ShapeDtypeStruct((b, l_all, GATE_COLS), F32)]
    if split_input:
        l_ctx = l_all - l_lat
        assert l_lat % tm + l_ctx == tm, "the context must exactly fill the last row tile"
        in_specs.append(pl.BlockSpec((1, l_ctx, d), lambda bb, i, j: (bb, 0, 0)))
        out_specs.append(x_spec)
        out_shape.append(jax.ShapeDtypeStruct((b, l_all, d), F32))
    in_specs +=[pl.BlockSpec((1, 6, d), lambda bb, i, j: (bb, 0, 0)),
                 pl.BlockSpec((1, 6, d), lambda bb, i, j: (0, 0, 0)),
                 pl.BlockSpec((None, d, tn), lambda bb, i, j: (layer, 0, j)),
                 pl.BlockSpec((None, d, GATE_COLS), lambda bb, i, j: (layer, 0, 0))]
    return pl.pallas_call(
        functools.partial(_inproj_kernel, tm=tm, l_lat=l_lat, split_input=split_input),
        grid=(b, l_all // tm, MAIN_W // tn),
        in_specs=in_specs,
        out_specs=out_specs,
        out_shape=out_shape,
        scratch_shapes=[pltpu.VMEM((tm, d), BF16)],
        compiler_params=_params(("arbitrary", "arbitrary", "arbitrary")),
        name="in_proj",
    )(*xs, ml, mc, w_main, w_ab)


WO_SUB_ROWS = 256


def _wo_kernel(rf_ref, rb_ref, rg_ref, df_ref, db_ref, dz_ref, nw_ref, a_ref, x_ref, ml_ref, mc_ref, w_ref,
               lnw_ref, lnb_ref, o_ref, *, tm, l_lat):
    i = pl.program_id(1)
    sub = WO_SUB_ROWS
    n_rd = RET_W + DN_W

    def head_norm(o):
        return o * lax.rsqrt(jnp.mean(o * o, -1, keepdims=True) + EPS)

    for lo in range(0, tm, sub):
        rows = slice(lo, lo + sub)
        ys = []
        for h in range(RET_HEADS):
            sl = slice(h * HEAD_DIM, (h + 1) * HEAD_DIM)
            o = rf_ref[0, rows, sl].astype(F32) + rb_ref[0, rows, sl].astype(F32)
            ys.append((head_norm(o) * _silu(rg_ref[0, rows, sl].astype(F32))).astype(BF16))
        for h in range(DN_HEADS):
            sl = slice(h * HEAD_DIM, (h + 1) * HEAD_DIM)
            o = df_ref[0, rows, sl].astype(F32) + db_ref[0, rows, sl].astype(F32)
            ys.append((head_norm(o) * nw_ref[...] * _silu(dz_ref[0, rows, sl].astype(F32))).astype(BF16))
        y_rd = jnp.concatenate(ys, axis=1)
        acc = _dot(y_rd, w_ref[0:n_rd, :]) + _dot(a_ref[0, rows, :], w_ref[n_rd:, :])
        rid = i * tm + lo + lax.broadcasted_iota(jnp.int32, (sub, 1), 0)
        gate = _mod_row(rid >= l_lat, ml_ref, mc_ref, 2)
        r = DEEPNORM_ALPHA * x_ref[0, rows, :] + gate * acc
        o_ref[0, rows, :] = _layer_norm(r, lnw_ref[...], lnb_ref[...])


def _wo_call(ret_f, ret_b, dn_f, dn_b, p_main, dn_norm_w, y_att, x_all, ml, mc, w_o, ln_w, ln_b, *, layer, l_lat,
             tm):
    b, l_all, d = x_all.shape
    kern = functools.partial(_wo_kernel, tm=tm, l_lat=l_lat)
    o_spec = pl.BlockSpec((1, tm, RET_W), lambda bb, i: (bb, i, 0))
    return pl.pallas_call(
        kern,
        grid=(b, l_all // tm),
        in_specs=[o_spec, o_spec,
                  pl.BlockSpec((1, tm, RET_W), lambda bb, i: (bb, i, COL_RET_G // RET_W)),
                  o_spec, o_spec,
                  pl.BlockSpec((1, tm, DN_W), lambda bb, i: (bb, i, COL_DN_Z // DN_W)),
                  pl.BlockSpec((1, HEAD_DIM), lambda bb, i: (0, 0)),
                  pl.BlockSpec((1, tm, ATT_W), lambda bb, i: (bb, i, 0)),
                  pl.BlockSpec((1, tm, d), lambda bb, i: (bb, i, 0)),
                  pl.BlockSpec((1, 6, d), lambda bb, i: (bb, 0, 0)),
                  pl.BlockSpec((1, 6, d), lambda bb, i: (0, 0, 0)),
                  pl.BlockSpec((None, RET_W + DN_W + ATT_W, d), lambda bb, i: (layer, 0, 0),
                               pipeline_mode=pl.Buffered(1)),
                  pl.BlockSpec((1, d), lambda bb, i: (0, 0)),
                  pl.BlockSpec((1, d), lambda bb, i: (0, 0))],
        out_specs=pl.BlockSpec((1, tm, d), lambda bb, i: (bb, i, 0)),
        out_shape=jax.ShapeDtypeStruct((b, l_all, d), F32),
        compiler_params=_params(("arbitrary", "arbitrary")),
        name="w_o_postnorm",
    )(ret_f, ret_b, p_main, dn_f, dn_b, p_main, dn_norm_w, y_att, x_all, ml, mc, w_o, ln_w, ln_b)


FFN_SUB_ROWS = 256


def _ffn_kernel(x_ref, ml_ref, mc_ref, wg_ref, wu_ref, wo_ref, lnw_ref, lnb_ref, o_ref, h_ref, acc_ref,
                *, tm, l_lat):
    i = pl.program_id(1)
    f = pl.program_id(2)
    nf = pl.num_programs(2)
    sub = FFN_SUB_ROWS
    blocks = [slice(lo, lo + sub) for lo in range(0, tm, sub)]

    def mod(rows, k):
        rid = i * tm + rows.start + lax.broadcasted_iota(jnp.int32, (sub, 1), 0)
        return _mod_row(rid >= l_lat, ml_ref, mc_ref, k)

    def swiglu_out(rows):
        h = h_ref[rows, :]
        g = _dot(h, wg_ref[...])
        u = _dot(h, wu_ref[...])
        return _dot((_silu(g) * u).astype(BF16), wo_ref[...])

    @pl.when(f == 0)
    def _():
        for rows in blocks:
            h_ref[rows, :] = (x_ref[0, rows, :] * (1.0 + mod(rows, 4)) + mod(rows, 3)).astype(BF16)
            acc_ref[rows, :] = swiglu_out(rows)

    @pl.when(jnp.logical_and(f > 0, f < nf - 1))
    def _():
        acc_ref[...] += swiglu_out(slice(0, tm))

    @pl.when(f == nf - 1)
    def _():
        for rows in blocks:
            acc = acc_ref[rows, :] + swiglu_out(rows)
            r = DEEPNORM_ALPHA * x_ref[0, rows, :] + mod(rows, 5) * acc
            o_ref[0, rows, :] = _layer_norm(r, lnw_ref[...], lnb_ref[...])


def _ffn_call(x_all, ml, mc, w_in, w_out, ln_w, ln_b, *, layer, l_lat, out_rows, tm, tf):
    b, _, d = x_all.shape
    d_ff = w_out.shape[1]
    nf = d_ff // tf
    assert nf >= 2 and tm % FFN_SUB_ROWS == 0
    kern = functools.partial(_ffn_kernel, tm=tm, l_lat=l_lat)
    return pl.pallas_call(
        kern,
        grid=(b, pl.cdiv(out_rows, tm), nf),
        in_specs=[pl.BlockSpec((1, tm, d), lambda bb, i, f: (bb, i, 0)),
                  pl.BlockSpec((1, 6, d), lambda bb, i, f: (bb, 0, 0)),
                  pl.BlockSpec((1, 6, d), lambda bb, i, f: (0, 0, 0)),
                  pl.BlockSpec((None, d, tf), lambda bb, i, f: (layer, 0, f)),
                  pl.BlockSpec((None, d, tf), lambda bb, i, f: (layer, 0, f + nf)),
                  pl.BlockSpec((None, tf, d), lambda bb, i, f: (layer, f, 0)),
                  pl.BlockSpec((1, d), lambda bb, i, f: (0, 0)),
                  pl.BlockSpec((1, d), lambda bb, i, f: (0, 0))],
        out_specs=pl.BlockSpec((1, tm, d), lambda bb, i, f: (bb, i, 0)),
        out_shape=jax.ShapeDtypeStruct((b, out_rows, d), F32),
        scratch_shapes=[pltpu.VMEM((tm, d), BF16), pltpu.VMEM((tm, d), F32)],
        compiler_params=_params(("arbitrary", "arbitrary", "arbitrary")),
        name="ffn_postnorm",
    )(x_all, ml, mc, w_in, w_in, w_out, ln_w, ln_b)


def _rope(x, cos2, sin2):
    return x * cos2 + pltpu.roll(x, HEAD_DIM // 2, 1) * sin2


def _ret_kernel(lg_ref, qf_ref, kf_ref, vf_ref, cf_ref, sf_ref, qb_ref, kb_ref, vb_ref, cb_ref, sb_ref,
                of_ref, ob_ref, s_ref, dm_ref, qd_ref, kd_ref):
    step = pl.program_id(1)
    c = RET_CHUNK

    @pl.when(step == 0)
    def _():
        s_ref[...] = jnp.zeros_like(s_ref)
        ii = lax.broadcasted_iota(jnp.int32, (c, c), 0).astype(F32)
        jj = lax.broadcasted_iota(jnp.int32, (c, c), 1).astype(F32)
        for d in range(2):
            for h in range(RET_HEADS):
                lg = lg_ref[d, h]
                if d == 0:
                    rel, qe, ke = ii - jj, ii + 1.0, (c - 1.0) - ii
                else:
                    rel, qe, ke = jj - ii, c - ii, ii
                idx = d * RET_HEADS + h
                dm_ref[idx] = jnp.where(rel >= 0, jnp.exp(jnp.maximum(rel, 0.0) * lg), 0.0)
                qd_ref[idx] = jnp.exp(qe * lg)
                kd_ref[idx] = jnp.exp(ke * lg)

    dirs = ((qf_ref, kf_ref, vf_ref, cf_ref, sf_ref, of_ref), (qb_ref, kb_ref, vb_ref, cb_ref, sb_ref, ob_ref))
    n_sub = of_ref.shape[1] // c
    chains = [(d, h) for d in range(2) for h in range(RET_HEADS)]
    sls = [slice(h * HEAD_DIM, (h + 1) * HEAD_DIM) for h in range(RET_HEADS)]
    order = [(j, n_sub - 1 - j) for j in range(n_sub)]
    rows = [[slice(u * c, (u + 1) * c) for u in sub] for sub in order]
    qs, ks, vs, qk, kv = {}, {}, {}, {}, {}
    for j in range(n_sub):
        for i, (d, h) in enumerate(chains):
            r = rows[j][d]
            tab = (dirs[d][3][r, :], dirs[d][4][r, :])
            qs[j, i] = _rope(dirs[d][0][0, r, sls[h]].astype(F32), *tab)
            ks[j, i] = _rope(dirs[d][1][0, r, sls[h]].astype(F32), *tab) * QK_SCALE
            vs[j, i] = dirs[d][2][0, r, sls[h]]
    for key in qs:
        qk[key] = (_dot_nt(qs[key].astype(BF16), ks[key].astype(BF16)) * dm_ref[key[1]]).astype(BF16)
        kv[key] = _dot((ks[key] * kd_ref[key[1]]).T.astype(BF16), vs[key])
    state = [s_ref[i] for i in range(len(chains))]
    for j in range(n_sub):
        o_inter = [_dot((qs[j, i] * qd_ref[i]).astype(BF16), s.astype(BF16)) for i, s in enumerate(state)]
        for i, (d, h) in enumerate(chains):
            dirs[d][5][0, rows[j][d], sls[h]] = (_dot(qk[j, i], vs[j, i]) + o_inter[i]).astype(dirs[d][5].dtype)
            chunk_decay = jnp.exp(jnp.full((1, HEAD_DIM), float(c), F32) * lg_ref[d, h])
            state[i] = state[i] * chunk_decay + kv[j, i]
    for i, st in enumerate(state):
        s_ref[i] = st


RET_STEP_CHUNKS = 2


def _ret_call(log_gamma, p_main, cos2, sin2, *, l_lat):
    b, l_all, _ = p_main.shape
    c = RET_CHUNK
    blk = RET_STEP_CHUNKS * c
    n_all, n_lat = l_all // blk, l_lat // blk
    n_ctx = n_all - n_lat

    def fwd(s):
        return jnp.where(s < n_ctx, n_lat + s, s - n_ctx)

    def bwd(s):
        return n_all - 1 - s

    def pspec(colblk, order):
        return pl.BlockSpec((1, blk, RET_W), lambda bb, s: (bb, order(s), colblk))

    def tspec(order):
        return pl.BlockSpec((blk, HEAD_DIM), lambda bb, s: (order(s), 0))

    in_specs = [pl.BlockSpec(memory_space=pltpu.SMEM)]
    for order in (fwd, bwd):
        in_specs += [pspec(COL_RET_Q // RET_W, order), pspec(COL_RET_K // RET_W, order),
                     pspec(COL_RET_V // RET_W, order), tspec(order), tspec(order)]
    nhd = 2 * RET_HEADS
    return pl.pallas_call(
        _ret_kernel,
        grid=(b, n_all),
        in_specs=in_specs,
        out_specs=[pl.BlockSpec((1, blk, RET_W), lambda bb, s: (bb, fwd(s), 0)),
                   pl.BlockSpec((1, blk, RET_W), lambda bb, s: (bb, bwd(s), 0))],
        out_shape=[jax.ShapeDtypeStruct((b, l_all, RET_W), BF16)] * 2,
        scratch_shapes=[pltpu.VMEM((nhd, HEAD_DIM, HEAD_DIM), F32), pltpu.VMEM((nhd, c, c), F32),
                        pltpu.VMEM((nhd, c, HEAD_DIM), F32), pltpu.VMEM((nhd, c, HEAD_DIM), F32)],
        compiler_params=_params(("arbitrary", "arbitrary")),
        name="retention_scan",
    )(log_gamma, p_main, p_main, p_main, cos2, sin2, p_main, p_main, p_main, cos2, sin2)


def _attprep_kernel(q_ref, k_ref, v_ref, c_ref, s_ref, qw_ref, kw_ref, qo_ref, ko_ref, vo_ref):
    cos2 = c_ref[...]
    sin2 = s_ref[...]

    def norm_rope(x, w):
        xf = x.astype(F32)
        y = xf * lax.rsqrt(jnp.mean(xf * xf, -1, keepdims=True) + EPS) * w
        return _rope(y, cos2, sin2)

    for h in range(ATT_HEADS):
        sl = slice(h * HEAD_DIM, (h + 1) * HEAD_DIM)
        qo_ref[0, :, sl] = (norm_rope(q_ref[0, :, sl], qw_ref[...]) * (QK_SCALE * LOG2_E)).astype(BF16)
    for h in range(ATT_KV_HEADS):
        sl = slice(h * HEAD_DIM, (h + 1) * HEAD_DIM)
        ko_ref[0, sl, :] = norm_rope(k_ref[0, :, sl], kw_ref[...]).T.astype(BF16)
        vo_ref[0, :, 2 * h * HEAD_DIM:(2 * h + 1) * HEAD_DIM] = v_ref[0, :, sl]
        vo_ref[0, :, (2 * h + 1) * HEAD_DIM:(2 * h + 2) * HEAD_DIM] = jnp.ones((v_ref.shape[1], HEAD_DIM), BF16)


def _attprep_call(p_main, cos2, sin2, qn_w, kn_w, *, tm):
    b, l_all, _ = p_main.shape
    return pl.pallas_call(
        _attprep_kernel,
        grid=(b, l_all // tm),
        in_specs=[pl.BlockSpec((1, tm, ATT_W), lambda bb, i: (bb, i, COL_ATT_Q // ATT_W)),
                  pl.BlockSpec((1, tm, ATT_KV_W), lambda bb, i: (bb, i, COL_ATT_K // ATT_KV_W)),
                  pl.BlockSpec((1, tm, ATT_KV_W), lambda bb, i: (bb, i, COL_ATT_V // ATT_KV_W)),
                  pl.BlockSpec((tm, HEAD_DIM), lambda bb, i: (i, 0)),
                  pl.BlockSpec((tm, HEAD_DIM), lambda bb, i: (i, 0)),
                  pl.BlockSpec((1, HEAD_DIM), lambda bb, i: (0, 0)),
                  pl.BlockSpec((1, HEAD_DIM), lambda bb, i: (0, 0))],
        out_specs=[pl.BlockSpec((1, tm, ATT_W), lambda bb, i: (bb, i, 0)),
                   pl.BlockSpec((1, ATT_KV_W, tm), lambda bb, i: (bb, 0, i)),
                   pl.BlockSpec((1, tm, 2 * ATT_KV_W), lambda bb, i: (bb, i, 0))],
        out_shape=[jax.ShapeDtypeStruct((b, l_all, ATT_W), BF16),
                   jax.ShapeDtypeStruct((b, ATT_KV_W, l_all), BF16),
                   jax.ShapeDtypeStruct((b, l_all, 2 * ATT_KV_W), BF16)],
        compiler_params=_params(("arbitrary", "arbitrary")),
        name="attn_prep",
    )(p_main, p_main, p_main, cos2, sin2, qn_w, kn_w)


def _att_kernel(*refs, tk):
    q_ref, kt_ref, v_ref = refs[0], refs[1], refs[2]
    o_ref, s_ref, mx_ref, m_ref, acc_ref = refs[-5], refs[-4], refs[-3], refs[-2], refs[-1]
    tq = q_ref.shape[1]
    n = kt_ref.shape[2] // tk
    lanes = tk // HEAD_DIM
    rb = 64

    def put_scores(c):
        col = pl.multiple_of(c * tk, tk)
        s_ref[c] = _dot(q_ref[0], kt_ref[0, :, pl.ds(col, tk)])

    def fold_max(c):
        for r in range(tq // rb):
            rows = slice(r * rb, (r + 1) * rb)
            acc = mx_ref[rows, :]
            for j in range(lanes):
                acc = jnp.maximum(acc, s_ref[c, rows, j * HEAD_DIM:(j + 1) * HEAD_DIM])
            mx_ref[rows, :] = acc

    mx_ref[...] = jnp.full(mx_ref.shape, -jnp.inf, F32)
    put_scores(0)

    @pl.loop(1, n)
    def _(c):
        put_scores(c)
        fold_max(c - 1)

    fold_max(n - 1)
    m_ref[...] = jnp.max(mx_ref[...], -1, keepdims=True)

    acc_ref[...] = jnp.zeros_like(acc_ref)

    @pl.loop(0, n)
    def _(c):
        row = pl.multiple_of(c * tk, tk)
        p = jnp.exp2((s_ref[c] - m_ref[...]).astype(BF16))
        acc_ref[...] += _dot(p, v_ref[0, pl.ds(row, tk), :])

    o_ref[0] = (acc_ref[:, :HEAD_DIM] / acc_ref[:, HEAD_DIM:]).astype(BF16)


def _key_chunk(n_keys):
    for tk in (4224, 768, 512, 256):
        if n_keys % tk == 0:
            return tk
    raise ValueError("key count must be a multiple of 256")


def _att_call(qn, kn, v1, *, l_lat, tq):
    b, l_all, _ = qn.shape
    l_ctx = l_all - l_lat
    group = ATT_HEADS // ATT_KV_HEADS
    tk = _key_chunk(l_all)
    y_lat = pl.pallas_call(
        functools.partial(_att_kernel, tk=tk),
        grid=(b, ATT_HEADS, l_lat // tq),
        in_specs=[pl.BlockSpec((1, tq, HEAD_DIM), lambda bb, h, i: (bb, i, h)),
                  pl.BlockSpec((1, HEAD_DIM, l_all), lambda bb, h, i: (bb, h // group, 0),
                               pipeline_mode=pl.Buffered(1)),
                  pl.BlockSpec((1, l_all, 2 * HEAD_DIM), lambda bb, h, i: (bb, 0, h // group),
                               pipeline_mode=pl.Buffered(1))],
        out_specs=pl.BlockSpec((1, tq, HEAD_DIM), lambda bb, h, i: (bb, i, h)),
        out_shape=jax.ShapeDtypeStruct((b, l_all, ATT_W), BF16),
        scratch_shapes=[pltpu.VMEM((l_all // tk, tq, tk), F32), pltpu.VMEM((tq, HEAD_DIM), F32), pltpu.VMEM((tq, 1), F32),
                        pltpu.VMEM((tq, 2 * HEAD_DIM), F32)],
        compiler_params=_params(("arbitrary", "arbitrary", "arbitrary")),
        name="attention",
    )(qn, kn, v1)
    ctx_blk = l_lat // l_ctx
    return pl.pallas_call(
        functools.partial(_att_kernel, tk=l_ctx),
        grid=(b, ATT_HEADS),
        in_specs=[pl.BlockSpec((1, l_ctx, HEAD_DIM), lambda bb, h: (bb, ctx_blk, h)),
                  pl.BlockSpec((1, HEAD_DIM, l_ctx), lambda bb, h: (bb, h // group, ctx_blk)),
                  pl.BlockSpec((1, l_ctx, 2 * HEAD_DIM), lambda bb, h: (bb, ctx_blk, h // group)),
                  pl.BlockSpec(memory_space=pl.ANY)],
        out_specs=pl.BlockSpec((1, l_ctx, HEAD_DIM), lambda bb, h: (bb, ctx_blk, h)),
        out_shape=jax.ShapeDtypeStruct((b, l_all, ATT_W), BF16),
        scratch_shapes=[pltpu.VMEM((1, l_ctx, l_ctx), F32), pltpu.VMEM((l_ctx, HEAD_DIM), F32), pltpu.VMEM((l_ctx, 1), F32),
                        pltpu.VMEM((l_ctx, 2 * HEAD_DIM), F32)],
        input_output_aliases={3: 0},
        compiler_params=_params(("arbitrary", "arbitrary")),
        name="attention_ctx",
    )(qn, kn, v1, y_lat)


def _dnprep_kernel(*refs, tm, l_lat, l_all):
    mains, prevs, nexts = refs[0:3], refs[3:6], refs[6:9]
    cw_ref, pab_ref, alog_ref, dtb_ref = refs[9:13]
    outs = refs[13:16]
    g_ref, ext_ref = refs[16], refs[17]
    i = pl.program_id(1)
    halo = 16
    pad = DN_CONV_K // 2
    first = jnp.logical_or(i == 0, i == l_lat // tm)
    last = jnp.logical_or(i == l_lat // tm - 1, i == l_all // tm - 1)

    keep_prev = jnp.where(first, 0.0, 1.0)
    keep_next = jnp.where(last, 0.0, 1.0)

    for part in range(3):
        ext_ref[0:halo, :] = prevs[part][0].astype(F32) * keep_prev
        ext_ref[halo:halo + tm, :] = mains[part][0].astype(F32)
        ext_ref[halo + tm:, :] = nexts[part][0].astype(F32) * keep_next
        acc = jnp.zeros((tm, DN_W), F32)
        for j in range(DN_CONV_K):
            w_j = cw_ref[j:j + 1, part * DN_W:(part + 1) * DN_W]
            acc = acc + w_j * ext_ref[pl.ds(halo - pad + j, tm), :]
        y = _silu(acc)
        for h in range(DN_HEADS):
            sl = slice(h * HEAD_DIM, (h + 1) * HEAD_DIM)
            yh = y[:, sl]
            if part < 2:
                yh = yh * lax.rsqrt(jnp.sum(yh * yh, -1, keepdims=True) + EPS)
            if part == 0:
                yh = yh * QK_SCALE
            outs[part][0, :, sl] = yh.astype(BF16)

    a = pab_ref[0]
    col = lax.broadcasted_iota(jnp.int32, a.shape, 1)
    z = a + dtb_ref[...]
    softplus = jnp.maximum(z, 0.0) + jnp.log(1.0 + jnp.exp(-jnp.abs(z)))
    g = -jnp.exp(alog_ref[...]) * softplus
    ii = lax.broadcasted_iota(jnp.int32, (tm, tm), 0)
    rr = lax.broadcasted_iota(jnp.int32, (tm, tm), 1)
    same_chunk = (ii // DN_CHUNK) == (rr // DN_CHUNK)
    prefix = jnp.logical_and(same_chunk, rr <= ii).astype(F32)
    suffix = jnp.logical_and(same_chunk, rr >= ii).astype(F32)
    hi = lax.Precision.HIGHEST
    cum_f = jnp.dot(prefix, g, preferred_element_type=F32, precision=hi)
    cum_b = jnp.dot(suffix, g, preferred_element_type=F32, precision=hi)
    g_ref[0] = jnp.where(col < DN_HEADS, cum_f, jnp.where(col < N_DN_GATES, cum_b, jax.nn.sigmoid(a)))


def _dnprep_call(p_main, p_ab, conv_w, alog_row, dtb_row, *, l_lat, tm):
    b, l_all, _ = p_main.shape
    halo = 16
    r = tm // halo
    cb = COL_DN_QKV // DN_W
    in_specs = []
    for part in range(3):
        in_specs.append(pl.BlockSpec((1, tm, DN_W), lambda bb, i, p=part: (bb, i, cb + p)))
    for part in range(3):
        in_specs.append(pl.BlockSpec((1, halo, DN_W), lambda bb, i, p=part: (bb, jnp.maximum(i * r - 1, 0), cb + p)))
    for part in range(3):
        in_specs.append(pl.BlockSpec(
            (1, halo, DN_W), lambda bb, i, p=part: (bb, jnp.minimum((i + 1) * r, l_all // halo - 1), cb + p)))
    in_specs += [pl.BlockSpec((DN_CONV_K, 3 * DN_W), lambda bb, i: (0, 0)),
                 pl.BlockSpec((1, tm, GATE_COLS), lambda bb, i: (bb, i, 0)),
                 pl.BlockSpec((1, GATE_COLS), lambda bb, i: (0, 0)),
                 pl.BlockSpec((1, GATE_COLS), lambda bb, i: (0, 0))]
    kern = functools.partial(_dnprep_kernel, tm=tm, l_lat=l_lat, l_all=l_all)
    return pl.pallas_call(
        kern,
        grid=(b, l_all // tm),
        in_specs=in_specs,
        out_specs=[pl.BlockSpec((1, tm, DN_W), lambda bb, i: (bb, i, 0))] * 3
        + [pl.BlockSpec((1, tm, GATE_COLS), lambda bb, i: (bb, i, 0))],
        out_shape=[jax.ShapeDtypeStruct((b, l_all, DN_W), BF16)] * 3
        + [jax.ShapeDtypeStruct((b, l_all, GATE_COLS), F32)],
        scratch_shapes=[pltpu.VMEM((tm + 2 * halo, DN_W), F32)],
        compiler_params=_params(("arbitrary", "arbitrary")),
        name="deltanet_conv_gates",
    )(*([p_main] * 9), conv_w, p_ab, alog_row, dtb_row)


def _neumann_inverse_many(a_list, eye):
    ps = [-a for a in a_list]
    ts = [eye + p for p in ps]
    for _ in range(5):
        pbs = [p.astype(BF16) for p in ps]
        ps = [_dot(pb, pb) for pb in pbs]
        ts = [t + _dot(t.astype(BF16), p.astype(BF16)) for t, p in zip(ts, ps)]
    return ts


def _dnchunk_kernel(q_ref, k_ref, v_ref, g_ref, gt_ref, w_ref, kc_ref, qg_ref, kgt_ref, qk_ref, gl_ref):
    c = DN_CHUNK
    n_sub = q_ref.shape[1] // c
    ii = lax.broadcasted_iota(jnp.int32, (c, c), 0)
    jj = lax.broadcasted_iota(jnp.int32, (c, c), 1)
    eye = (ii == jj).astype(F32)

    qk_ref[...] = jnp.zeros_like(qk_ref)
    kgt_ref[...] = jnp.zeros_like(kgt_ref)
    gl_ref[...] = jnp.zeros_like(gl_ref)

    heads = range(DN_HEADS)
    subs = range(n_sub)
    chains = [(u, d, h) for u in subs for h in heads for d in range(2)]
    rows = [slice(u * c, (u + 1) * c) for u in subs]
    sls = [slice(h * HEAD_DIM, (h + 1) * HEAD_DIM) for h in heads]
    gcol = [g_ref[0, rows[u], :] for u in subs]
    grow = [gt_ref[0, u] for u in subs]
    q16 = {(u, h): q_ref[0, rows[u], sls[h]] for u in subs for h in heads}
    k16 = {(u, h): k_ref[0, rows[u], sls[h]] for u in subs for h in heads}
    kk = {uh: _dot_nt(k16[uh], k16[uh]) for uh in k16}
    qk0 = {uh: _dot_nt(q16[uh], k16[uh]) for uh in k16}

    gcc, tot, beta, decay, a_list = {}, {}, {}, {}, []
    for u, d, h in chains:
        col = d * DN_HEADS + h
        key = (u, d, h)
        gcc[key] = gcol[u][:, col:col + 1]
        gcr = grow[u][col:col + 1, :]
        tot[key] = gcol[u][c - 1:c, col:col + 1] if d == 0 else gcol[u][0:1, col:col + 1]
        beta[key] = gcol[u][:, N_DN_GATES + col:N_DN_GATES + col + 1]
        incl = (ii >= jj) if d == 0 else (ii <= jj)
        strict = (ii > jj) if d == 0 else (ii < jj)
        decay[key] = jnp.where(incl, jnp.exp(jnp.where(incl, gcc[key] - gcr, 0.0)), 0.0)
        a_list.append(jnp.where(strict, kk[u, h] * beta[key] * decay[key], 0.0))
        gl_ref[0, d, u, h:h + 1, :] = jnp.broadcast_to(jnp.exp(tot[key]), (1, HEAD_DIM))

    t16 = [t.astype(BF16) for t in _neumann_inverse_many(a_list, eye)]

    for key, t in zip(chains, t16):
        u, d, h = key
        sl = sls[h]
        k = k16[u, h].astype(F32)
        e_col = jnp.exp(gcc[key])
        rhs = jnp.concatenate([(v_ref[0, rows[u], sl].astype(F32) * beta[key]).astype(BF16),
                               (k * (beta[key] * e_col)).astype(BF16)], axis=1)
        wk = _dot(t, rhs)
        w_ref[0, d, rows[u], sl] = wk[:, :HEAD_DIM]
        kc_ref[0, d, rows[u], sl] = wk[:, HEAD_DIM:].astype(BF16)
        qg_ref[0, d, rows[u], sl] = (q16[u, h].astype(F32) * e_col).astype(BF16)
        kg = k * jnp.exp(tot[key] - gcc[key])
        kgt_ref[0, d, u, :, pl.ds(h * HEAD_DIM, c)] = kg.T.astype(BF16)
        qk_ref[0, d, rows[u], pl.ds(h * HEAD_DIM, c)] = (qk0[u, h] * decay[key]).astype(BF16)


DN_STEP_CHUNKS = 4


def _dnchunk_call(qd, kd, vd, g, gt):
    b, l_all, _ = qd.shape
    c = DN_CHUNK
    n = l_all // c
    ns = DN_STEP_CHUNKS
    tok = lambda dt: jax.ShapeDtypeStruct((b, 2, l_all, DN_W), dt)
    tok_spec = pl.BlockSpec((1, 2, ns * c, DN_W), lambda bb, s: (bb, 0, s, 0))
    return pl.pallas_call(
        _dnchunk_kernel,
        grid=(b, n // ns),
        in_specs=[pl.BlockSpec((1, ns * c, DN_W), lambda bb, s: (bb, s, 0))] * 3
        + [pl.BlockSpec((1, ns * c, GATE_COLS), lambda bb, s: (bb, s, 0)),
           pl.BlockSpec((1, ns, 2 * N_DN_GATES, c), lambda bb, s: (bb, s, 0, 0))],
        out_specs=[tok_spec, tok_spec, tok_spec,
                   pl.BlockSpec((1, 2, ns, HEAD_DIM, DN_W), lambda bb, s: (bb, 0, s, 0, 0)),
                   tok_spec,
                   pl.BlockSpec((1, 2, ns, 8, HEAD_DIM), lambda bb, s: (bb, 0, s, 0, 0))],
        out_shape=[tok(F32), tok(BF16), tok(BF16),
                   jax.ShapeDtypeStruct((b, 2, n, HEAD_DIM, DN_W), BF16),
                   tok(BF16),
                   jax.ShapeDtypeStruct((b, 2, n, 8, HEAD_DIM), F32)],
        compiler_params=_params(("arbitrary", "arbitrary")),
        name="deltanet_chunk_factors",
    )(qd, kd, vd, g, gt)


def _dnscan_kernel(*refs):
    ins_f, ins_b = refs[0:6], refs[6:12]
    of_ref, ob_ref, s_ref = refs[12], refs[13], refs[14]
    c = DN_CHUNK
    n_sub = of_ref.shape[1] // c
    step = pl.program_id(1)

    @pl.when(step == 0)
    def _():
        s_ref[...] = jnp.zeros_like(s_ref)

    ins, outs = (ins_f, ins_b), (of_ref, ob_ref)
    chains = [(d, h) for d in range(2) for h in range(DN_HEADS)]
    sls = [slice(h * HEAD_DIM, (h + 1) * HEAD_DIM) for h in range(DN_HEADS)]
    state = [s_ref[d * DN_HEADS + h] for d, h in chains]
    for j in range(n_sub):
        sub = [j, n_sub - 1 - j]
        rows = [slice(u * c, (u + 1) * c) for u in sub]
        s16 = [s.astype(BF16) for s in state]
        v16 = [(ins[d][0][0, 0, rows[d], sls[h]] - _dot(ins[d][1][0, 0, rows[d], sls[h]], s)).astype(BF16)
               for (d, h), s in zip(chains, s16)]
        o_inter = [_dot(ins[d][2][0, 0, rows[d], sls[h]], s) for (d, h), s in zip(chains, s16)]
        new_state = []
        for (d, h), s, v, oi in zip(chains, state, v16, o_inter):
            qk = ins[d][4][0, 0, rows[d], pl.ds(h * HEAD_DIM, c)]
            outs[d][0, rows[d], sls[h]] = (oi + _dot(qk, v)).astype(outs[d].dtype)
            kgt = ins[d][3][0, 0, sub[d], :, pl.ds(h * HEAD_DIM, c)]
            new_state.append(s * ins[d][5][0, 0, sub[d], h:h + 1, :] + _dot(kgt, v))
        state = new_state
    for i, s in enumerate(state):
        s_ref[i] = s


def _dnscan_call(w, kc, qg, kgt, qk, gl, *, l_lat):
    b, _, l_all, _ = w.shape
    ns = DN_STEP_CHUNKS
    blk = ns * DN_CHUNK
    n_all, n_lat = l_all // blk, l_lat // blk
    n_ctx = n_all - n_lat

    def fwd(s):
        return jnp.where(s < n_ctx, n_lat + s, s - n_ctx)

    def bwd(s):
        return n_all - 1 - s

    in_specs = []
    for d, order in enumerate((fwd, bwd)):
        tok_spec = pl.BlockSpec((1, 1, blk, DN_W), lambda bb, s, d=d, o=order: (bb, d, o(s), 0))
        in_specs += [tok_spec, tok_spec, tok_spec,
                     pl.BlockSpec((1, 1, ns, HEAD_DIM, DN_W), lambda bb, s, d=d, o=order: (bb, d, o(s), 0, 0)),
                     tok_spec,
                     pl.BlockSpec((1, 1, ns, 8, HEAD_DIM), lambda bb, s, d=d, o=order: (bb, d, o(s), 0, 0))]
    return pl.pallas_call(
        _dnscan_kernel,
        grid=(b, n_all),
        in_specs=in_specs,
        out_specs=[pl.BlockSpec((1, blk, DN_W), lambda bb, s: (bb, fwd(s), 0)),
                   pl.BlockSpec((1, blk, DN_W), lambda bb, s: (bb, bwd(s), 0))],
        out_shape=[jax.ShapeDtypeStruct((b, l_all, DN_W), BF16)] * 2,
        scratch_shapes=[pltpu.VMEM((2 * DN_HEADS, HEAD_DIM, HEAD_DIM), F32)],
        compiler_params=_params(("arbitrary", "arbitrary")),
        name="deltanet_scan",
    )(w, kc, qg, kgt, qk, gl, w, kc, qg, kgt, qk, gl)


def _rope_tables(l_lat, l_ctx):
    rows = l_lat // GRID_W
    n_freq = HEAD_DIM // 4
    inv = ROPE_THETA ** (-jnp.arange(n_freq, dtype=F32) / n_freq)
    row_ang = jnp.arange(rows, dtype=F32)[:, None] * inv
    col_ang = jnp.arange(GRID_W, dtype=F32)[:, None] * inv

    def table(fn):
        return jnp.concatenate([jnp.repeat(fn(row_ang), GRID_W, axis=0), jnp.tile(fn(col_ang), (rows, 1))], -1)

    cos, sin = table(jnp.cos), table(jnp.sin)
    cos2 = jnp.concatenate([cos, cos], -1)
    sin2 = jnp.concatenate([-sin, sin], -1)
    cos2 = jnp.concatenate([cos2, jnp.ones((l_ctx, HEAD_DIM), F32)], 0)
    sin2 = jnp.concatenate([sin2, jnp.zeros((l_ctx, HEAD_DIM), F32)], 0)
    return cos2, sin2


def _pad_row(v, width):
    v = v.reshape(1, -1).astype(F32)
    return jnp.pad(v, ((0, 0), (0, width - v.shape[1])))


def _row_tile(l_lat, l_ctx):
    l_all = l_lat + l_ctx
    for tm in (768, 512, 256, 128):
        if l_all % tm == 0:
            return tm
    raise ValueError("token count must be a multiple of 128")


def kernel(x, c, ctx, c_ctx, w_ada, b_ada, w_in, ret_decay_logit, dn_conv_w, dn_a_log, dn_dt_bias, dn_norm_w,
           att_qn_w, att_kn_w, w_o, ln1_w, ln1_b, w_ffn_in, w_ffn_out, ln2_w, ln2_b):
    bsz, l_lat, d = x.shape
    l_ctx = ctx.shape[1]
    l_all = l_lat + l_ctx
    depth = w_ada.shape[0]
    assert l_lat % 256 == 0 and l_ctx % 256 == 0 and l_lat % GRID_W == 0
    tm = _row_tile(l_lat, l_ctx)
    tn = MAIN_W // 2

    cos2, sin2 = _rope_tables(l_lat, l_ctx)

    cond_rows = 8 * pl.cdiv(bsz + 1, 8)
    cond = jnp.concatenate([c, c_ctx[None, :]], 0)
    cond = jnp.pad(cond, ((0, cond_rows - bsz - 1), (0, 0)))
    mod = _ada_call(cond, w_ada, b_ada)

    w_main, w_ab = _repack_call(jnp.pad(w_in.astype(BF16), ((0, 0), (0, 0), (0, -w_in.shape[2] % HEAD_DIM))))
    w_o16, w_ffn_in16, w_ffn_out16 = w_o.astype(BF16), w_ffn_in.astype(BF16), w_ffn_out.astype(BF16)
    for i in range(depth):
        last = i == depth - 1
        ml = mod[i, :bsz].reshape(bsz, 6, d)
        mc = mod[i, bsz].reshape(1, 6, d)
        if i == 0:
            p_main, p_ab, x_all = _inproj_call((x, ctx), ml, mc, w_main, w_ab, layer=i, l_lat=l_lat, tm=tm, tn=tn // 2)
        else:
            p_main, p_ab = _inproj_call(x_all, ml, mc, w_main, w_ab, layer=i, l_lat=l_lat, tm=tm, tn=tn)

        log_gamma = jax.nn.log_sigmoid(ret_decay_logit[i].astype(F32))
        ret_f, ret_b = _ret_call(log_gamma, p_main, cos2, sin2, l_lat=l_lat)

        qn, kn, v1 = _attprep_call(p_main, cos2, sin2, att_qn_w[i].reshape(1, -1), att_kn_w[i].reshape(1, -1),
                                   tm=tm)
        y_att = _att_call(qn, kn, v1, l_lat=l_lat, tq=1024 if l_lat % 1024 == 0 else 256)

        qd, kd, vd, g = _dnprep_call(p_main, p_ab, dn_conv_w[i], _pad_row(dn_a_log[i], GATE_COLS),
                                     _pad_row(dn_dt_bias[i], GATE_COLS), l_lat=l_lat, tm=256)
        gt = g[:, :, :2 * N_DN_GATES].reshape(bsz, l_all // DN_CHUNK, DN_CHUNK, 2 * N_DN_GATES)
        gt = jnp.swapaxes(gt, 2, 3)
        dn_f, dn_b = _dnscan_call(*_dnchunk_call(qd, kd, vd, g, gt), l_lat=l_lat)

        x_all = _wo_call(ret_f, ret_b, dn_f, dn_b, p_main, dn_norm_w[i].reshape(1, -1), y_att, x_all, ml, mc,
                         w_o16, ln1_w[i].reshape(1, -1), ln1_b[i].reshape(1, -1), layer=i, l_lat=l_lat, tm=tm)
        x_all = _ffn_call(x_all, ml, mc, w_ffn_in16, w_ffn_out16, ln2_w[i].reshape(1, -1), ln2_b[i].reshape(1, -1),
                          layer=i, l_lat=l_lat, out_rows=l_lat if last else l_all, tm=tm, tf=512)
    return x_all
```

```python
import functools

import jax
import jax.numpy as jnp
from jax import lax
from jax.experimental import pallas as pl
from jax.experimental.pallas import tpu as pltpu

F32 = jnp.float32
BF16 = jnp.bfloat16

HEAD_DIM = 128
RET_HEADS = 4
DN_HEADS = 4
ATT_HEADS = 8
ATT_KV_HEADS = 2
RET_W = RET_HEADS * HEAD_DIM
DN_W = DN_HEADS * HEAD_DIM
ATT_W = ATT_HEADS * HEAD_DIM
ATT_KV_W = ATT_KV_HEADS * HEAD_DIM
RET_CHUNK = 128
DN_CHUNK = 64
DN_CONV_K = 5
GRID_W = 64
ROPE_THETA = 10000.0
MODEL_DEPTH = 4
DEEPNORM_ALPHA = (2 * MODEL_DEPTH) ** 0.25
EPS = 1e-6
QK_SCALE = HEAD_DIM ** -0.5
LOG2_E = 1.4426950408889634

COL_RET_Q = 0
COL_RET_K = RET_W
COL_RET_V = 2 * RET_W
COL_RET_G = 3 * RET_W
COL_DN_QKV = 4 * RET_W
COL_DN_Z = COL_DN_QKV + 3 * DN_W
COL_ATT_Q = COL_DN_Z + DN_W
COL_ATT_K = COL_ATT_Q + ATT_W
COL_ATT_V = COL_ATT_K + ATT_KV_W
MAIN_W = COL_ATT_V + ATT_KV_W
GATE_COLS = 128
N_DN_GATES = 2 * DN_HEADS

VMEM_LIMIT_MB = 56


def _params(sem, vmem_mb=VMEM_LIMIT_MB):
    return pltpu.CompilerParams(dimension_semantics=sem, vmem_limit_bytes=vmem_mb * 1024 * 1024)


def _dot(a, b):
    return jnp.dot(a, b, preferred_element_type=F32)


def _dot_nt(a, b):
    return lax.dot_general(a, b, (((1,), (1,)), ((), ())), preferred_element_type=F32)


def _silu(x):
    return x * jax.nn.sigmoid(x)


def _is_ctx_rows(i, tm, l_lat):
    rows = i * tm + lax.broadcasted_iota(jnp.int32, (tm, 1), 0)
    return rows >= l_lat


def _mod_row(is_ctx, ml_ref, mc_ref, k):
    return jnp.where(is_ctx, mc_ref[0, k:k + 1, :], ml_ref[0, k:k + 1, :])


def _layer_norm(r, w, b):
    mu = jnp.mean(r, -1, keepdims=True)
    rc = r - mu
    var = jnp.mean(rc * rc, -1, keepdims=True)
    return rc * lax.rsqrt(var + EPS) * w + b


def _split_bf16(a):
    hi = a.astype(BF16)
    return hi, (a - hi.astype(F32)).astype(BF16)


def _ada_kernel(c_ref, w_ref, b_ref, o_ref):
    h_hi, h_lo = _split_bf16(_silu(c_ref[...]))
    w_hi, w_lo = _split_bf16(w_ref[0])
    o_ref[0] = _dot(h_hi, w_hi) + _dot(h_hi, w_lo) + _dot(h_lo, w_hi) + b_ref[0]


def _ada_call(cond, w_ada, b_ada):
    depth, d, n6 = w_ada.shape
    rows = cond.shape[0]
    tn = 1024
    return pl.pallas_call(
        _ada_kernel,
        grid=(depth, n6 // tn),
        in_specs=[pl.BlockSpec((rows, d), lambda l, j: (0, 0)),
                  pl.BlockSpec((1, d, tn), lambda l, j: (l, 0, j)),
                  pl.BlockSpec((1, 1, tn), lambda l, j: (l, 0, j))],
        out_specs=pl.BlockSpec((1, rows, tn), lambda l, j: (l, 0, j)),
        out_shape=jax.ShapeDtypeStruct((depth, rows, n6), F32),
        compiler_params=_params(("arbitrary", "arbitrary")),
        name="ada_mod",
    )(cond, w_ada, b_ada.reshape(depth, 1, n6))


GATE_LO = COL_DN_Z + DN_W
INPROJ_SUB_ROWS = 256


def _repack_kernel(w_ref, wm_ref, wab_ref):
    wm_ref[0, :, :GATE_LO] = w_ref[0, :, :GATE_LO].astype(BF16)
    wm_ref[0, :, GATE_LO:] = w_ref[0, :, GATE_LO + 2 * N_DN_GATES:MAIN_W + 2 * N_DN_GATES].astype(BF16)
    gates = w_ref[0, :, GATE_LO:GATE_LO + GATE_COLS]
    lane = lax.broadcasted_iota(jnp.int32, gates.shape, 1)
    wab_ref[0] = jnp.where(lane < 2 * N_DN_GATES, gates, jnp.zeros_like(gates)).astype(BF16)


def _repack_call(w_in):
    depth, d, proj_w = w_in.shape
    assert proj_w >= MAIN_W + 2 * N_DN_GATES
    tr = 256
    return pl.pallas_call(
        _repack_kernel,
        grid=(depth, d // tr),
        in_specs=[pl.BlockSpec((1, tr, proj_w), lambda l, r: (l, r, 0))],
        out_specs=[pl.BlockSpec((1, tr, MAIN_W), lambda l, r: (l, r, 0)),
                   pl.BlockSpec((1, tr, GATE_COLS), lambda l, r: (l, r, 0))],
        out_shape=[jax.ShapeDtypeStruct((depth, d, MAIN_W), BF16), jax.ShapeDtypeStruct((depth, d, GATE_COLS), BF16)],
        compiler_params=_params(("arbitrary", "arbitrary")),
        name="repack_w_in",
    )(w_in)


def _inproj_kernel(*refs, tm, l_lat, split_input):
    if split_input:
        x_ref, c_ref, ml_ref, mc_ref, w_ref, wab_ref, pm_ref, pab_ref, xo_ref, h_ref = refs
    else:
        x_ref, ml_ref, mc_ref, w_ref, wab_ref, pm_ref, pab_ref, h_ref = refs
    i = pl.program_id(1)
    j = pl.program_id(2)
    tn = w_ref.shape[1]
    step = 512

    def project(rows):
        for lo in range(0, tn, step):
            hi = min(lo + step, tn)
            pm_ref[0, rows, lo:hi] = _dot(h_ref[rows, :], w_ref[:, lo:hi]).astype(BF16)

    @pl.when(j == 0)
    def _():
        if split_input:
            n_full, rem = divmod(l_lat, tm)

            @pl.when(i < n_full)
            def _():
                xo_ref[0] = x_ref[0]

            @pl.when(i >= n_full)
            def _():
                if rem:
                    xo_ref[0, :rem, :] = x_ref[0, :rem, :]
                xo_ref[0, rem:, :] = c_ref[0]

        src = xo_ref if split_input else x_ref
        for lo in range(0, tm, INPROJ_SUB_ROWS):
            rows = slice(lo, lo + INPROJ_SUB_ROWS)
            rid = i * tm + lo + lax.broadcasted_iota(jnp.int32, (INPROJ_SUB_ROWS, 1), 0)
            shift = _mod_row(rid >= l_lat, ml_ref, mc_ref, 0)
            scale = _mod_row(rid >= l_lat, ml_ref, mc_ref, 1)
            h = (src[0, rows, :] * (1.0 + scale) + shift).astype(BF16)
            h_ref[rows, :] = h
            pab_ref[0, rows, :] = _dot(h, wab_ref[...])
            project(rows)

    @pl.when(j > 0)
    def _():
        project(slice(0, tm))


def _inproj_call(xs, ml, mc, w_main, w_ab, *, layer, l_lat, tm, tn):
    split_input = isinstance(xs, tuple)
    xs = list(xs) if split_input else [xs]
    b, _, d = xs[0].shape
    l_all = l_lat + xs[1].shape[1] if split_input else xs[0].shape[1]
    x_spec = pl.BlockSpec((1, tm, d), lambda bb, i, j: (bb, i, 0))
    in_specs = [x_spec]
    out_specs = [pl.BlockSpec((1, tm, tn), lambda bb, i, j: (bb, i, j)),
                 pl.BlockSpec((1, tm, GATE_COLS), lambda bb, i, j: (bb, i, 0))]
    out_shape = [jax.ShapeDtypeStruct((b, l_all, MAIN_W), BF16), jax.ShapeDtypeStruct((b, l_all, GATE_COLS), F32)]
    if split_input:
        l_ctx = l_all - l_lat
        assert l_lat % tm + l_ctx == tm, "the context must exactly fill the last row tile"
        in_specs.append(pl.BlockSpec((1, l_ctx, d), lambda bb, i, j: (bb, 0, 0)))
        out_specs.append(x_spec)
        out_shape.append(jax.ShapeDtypeStruct((b, l_all, d), F32))
    in_specs +=[pl.BlockSpec((1, 6, d), lambda bb, i, j: (bb, 0, 0)),
                 pl.BlockSpec((1, 6, d), lambda bb, i, j: (0, 0, 0)),
                 pl.BlockSpec((None, d, tn), lambda bb, i, j: (layer, 0, j)),
                 pl.BlockSpec((None, d, GATE_COLS), lambda bb, i, j: (layer, 0, 0))]
    return pl.pallas_call(
        functools.partial(_inproj_kernel, tm=tm, l_lat=l_lat, split_input=split_input),
        grid=(b, l_all // tm, MAIN_W // tn),
        in_specs=in_specs,
        out_specs=out_specs,
        out_shape=out_shape,
        scratch_shapes=[pltpu.VMEM((tm, d), BF16)],
        compiler_params=_params(("arbitrary", "arbitrary", "arbitrary")),
        name="in_proj",
    )(*xs, ml, mc, w_main, w_ab)


WO_SUB_ROWS = 256


def _wo_kernel(rf_ref, rb_ref, rg_ref, df_ref, db_ref, dz_ref, nw_ref, a_ref, x_ref, ml_ref, mc_ref, w_ref,
               lnw_ref, lnb_ref, o_ref, *, tm, l_lat):
    i = pl.program_id(1)
    sub = WO_SUB_ROWS
    n_rd = RET_W + DN_W

    def head_norm(o):
        return o * lax.rsqrt(jnp.mean(o * o, -1, keepdims=True) + EPS)

    for lo in range(0, tm, sub):
        rows = slice(lo, lo + sub)
        ys = []
        for h in range(RET_HEADS):
            sl = slice(h * HEAD_DIM, (h + 1) * HEAD_DIM)
            o = rf_ref[0, rows, sl].astype(F32) + rb_ref[0, rows, sl].astype(F32)
            ys.append((head_norm(o) * _silu(rg_ref[0, rows, sl].astype(F32))).astype(BF16))
        for h in range(DN_HEADS):
            sl = slice(h * HEAD_DIM, (h + 1) * HEAD_DIM)
            o = df_ref[0, rows, sl].astype(F32) + db_ref[0, rows, sl].astype(F32)
            ys.append((head_norm(o) * nw_ref[...] * _silu(dz_ref[0, rows, sl].astype(F32))).astype(BF16))
        y_rd = jnp.concatenate(ys, axis=1)
        acc = _dot(y_rd, w_ref[0:n_rd, :]) + _dot(a_ref[0, rows, :], w_ref[n_rd:, :])
        rid = i * tm + lo + lax.broadcasted_iota(jnp.int32, (sub, 1), 0)
        gate = _mod_row(rid >= l_lat, ml_ref, mc_ref, 2)
        r = DEEPNORM_ALPHA * x_ref[0, rows, :] + gate * acc
        o_ref[0, rows, :] = _layer_norm(r, lnw_ref[...], lnb_ref[...])


def _wo_call(ret_f, ret_b, dn_f, dn_b, p_main, dn_norm_w, y_att, x_all, ml, mc, w_o, ln_w, ln_b, *, layer, l_lat,
             tm):
    b, l_all, d = x_all.shape
    kern = functools.partial(_wo_kernel, tm=tm, l_lat=l_lat)
    o_spec = pl.BlockSpec((1, tm, RET_W), lambda bb, i: (bb, i, 0))
    return pl.pallas_call(
        kern,
        grid=(b, l_all // tm),
        in_specs=[o_spec, o_spec,
                  pl.BlockSpec((1, tm, RET_W), lambda bb, i: (bb, i, COL_RET_G // RET_W)),
                  o_spec, o_spec,
                  pl.BlockSpec((1, tm, DN_W), lambda bb, i: (bb, i, COL_DN_Z // DN_W)),
                  pl.BlockSpec((1, HEAD_DIM), lambda bb, i: (0, 0)),
                  pl.BlockSpec((1, tm, ATT_W), lambda bb, i: (bb, i, 0)),
                  pl.BlockSpec((1, tm, d), lambda bb, i: (bb, i, 0)),
                  pl.BlockSpec((1, 6, d), lambda bb, i: (bb, 0, 0)),
                  pl.BlockSpec((1, 6, d), lambda bb, i: (0, 0, 0)),
                  pl.BlockSpec((None, RET_W + DN_W + ATT_W, d), lambda bb, i: (layer, 0, 0),
                               pipeline_mode=pl.Buffered(1)),
                  pl.BlockSpec((1, d), lambda bb, i: (0, 0)),
                  pl.BlockSpec((1, d), lambda bb, i: (0, 0))],
        out_specs=pl.BlockSpec((1, tm, d), lambda bb, i: (bb, i, 0)),
        out_shape=jax.ShapeDtypeStruct((b, l_all, d), F32),
        compiler_params=_params(("arbitrary", "arbitrary")),
        name="w_o_postnorm",
    )(ret_f, ret_b, p_main, dn_f, dn_b, p_main, dn_norm_w, y_att, x_all, ml, mc, w_o, ln_w, ln_b)


FFN_SUB_ROWS = 256


def _ffn_kernel(x_ref, ml_ref, mc_ref, wg_ref, wu_ref, wo_ref, lnw_ref, lnb_ref, o_ref, h_ref, acc_ref,
                *, tm, l_lat):
    i = pl.program_id(1)
    f = pl.program_id(2)
    nf = pl.num_programs(2)
    sub = FFN_SUB_ROWS
    blocks = [slice(lo, lo + sub) for lo in range(0, tm, sub)]

    def mod(rows, k):
        rid = i * tm + rows.start + lax.broadcasted_iota(jnp.int32, (sub, 1), 0)
        return _mod_row(rid >= l_lat, ml_ref, mc_ref, k)

    def swiglu_out(rows):
        h = h_ref[rows, :]
        g = _dot(h, wg_ref[...])
        u = _dot(h, wu_ref[...])
        return _dot((_silu(g) * u).astype(BF16), wo_ref[...])

    @pl.when(f == 0)
    def _():
        for rows in blocks:
            h_ref[rows, :] = (x_ref[0, rows, :] * (1.0 + mod(rows, 4)) + mod(rows, 3)).astype(BF16)
            acc_ref[rows, :] = swiglu_out(rows)

    @pl.when(jnp.logical_and(f > 0, f < nf - 1))
    def _():
        acc_ref[...] += swiglu_out(slice(0, tm))

    @pl.when(f == nf - 1)
    def _():
        for rows in blocks:
            acc = acc_ref[rows, :] + swiglu_out(rows)
            r = DEEPNORM_ALPHA * x_ref[0, rows, :] + mod(rows, 5) * acc
            o_ref[0, rows, :] = _layer_norm(r, lnw_ref[...], lnb_ref[...])


def _ffn_call(x_all, ml, mc, w_in, w_out, ln_w, ln_b, *, layer, l_lat, out_rows, tm, tf):
    b, _, d = x_all.shape
    d_ff = w_out.shape[1]
    nf = d_ff // tf
    assert nf >= 2 and tm % FFN_SUB_ROWS == 0
    kern = functools.partial(_ffn_kernel, tm=tm, l_lat=l_lat)
    return pl.pallas_call(
        kern,
        grid=(b, pl.cdiv(out_rows, tm), nf),
        in_specs=[pl.BlockSpec((1, tm, d), lambda bb, i, f: (bb, i, 0)),
                  pl.BlockSpec((1, 6, d), lambda bb, i, f: (bb, 0, 0)),
                  pl.BlockSpec((1, 6, d), lambda bb, i, f: (0, 0, 0)),
                  pl.BlockSpec((None, d, tf), lambda bb, i, f: (layer, 0, f)),
                  pl.BlockSpec((None, d, tf), lambda bb, i, f: (layer, 0, f + nf)),
                  pl.BlockSpec((None, tf, d), lambda bb, i, f: (layer, f, 0)),
                  pl.BlockSpec((1, d), lambda bb, i, f: (0, 0)),
                  pl.BlockSpec((1, d), lambda bb, i, f: (0, 0))],
        out_specs=pl.BlockSpec((1, tm, d), lambda bb, i, f: (bb, i, 0)),
        out_shape=jax.ShapeDtypeStruct((b, out_rows, d), F32),
        scratch_shapes=[pltpu.VMEM((tm, d), BF16), pltpu.VMEM((tm, d), F32)],
        compiler_params=_params(("arbitrary", "arbitrary", "arbitrary")),
        name="ffn_postnorm",
    )(x_all, ml, mc, w_in, w_in, w_out, ln_w, ln_b)


def _rope(x, cos2, sin2):
    return x * cos2 + pltpu.roll(x, HEAD_DIM // 2, 1) * sin2


def _ret_kernel(lg_ref, qf_ref, kf_ref, vf_ref, cf_ref, sf_ref, qb_ref, kb_ref, vb_ref, cb_ref, sb_ref,
                of_ref, ob_ref, s_ref, dm_ref, qd_ref, kd_ref):
    step = pl.program_id(1)
    c = RET_CHUNK

    @pl.when(step == 0)
    def _():
        s_ref[...] = jnp.zeros_like(s_ref)
        ii = lax.broadcasted_iota(jnp.int32, (c, c), 0).astype(F32)
        jj = lax.broadcasted_iota(jnp.int32, (c, c), 1).astype(F32)
        for d in range(2):
            for h in range(RET_HEADS):
                lg = lg_ref[d, h]
                if d == 0:
                    rel, qe, ke = ii - jj, ii + 1.0, (c - 1.0) - ii
                else:
                    rel, qe, ke = jj - ii, c - ii, ii
                idx = d * RET_HEADS + h
                dm_ref[idx] = jnp.where(rel >= 0, jnp.exp(jnp.maximum(rel, 0.0) * lg), 0.0)
                qd_ref[idx] = jnp.exp(qe * lg)
                kd_ref[idx] = jnp.exp(ke * lg)

    dirs = ((qf_ref, kf_ref, vf_ref, cf_ref, sf_ref, of_ref), (qb_ref, kb_ref, vb_ref, cb_ref, sb_ref, ob_ref))
    n_sub = of_ref.shape[1] // c
    chains = [(d, h) for d in range(2) for h in range(RET_HEADS)]
    sls = [slice(h * HEAD_DIM, (h + 1) * HEAD_DIM) for h in range(RET_HEADS)]
    order = [(j, n_sub - 1 - j) for j in range(n_sub)]
    rows = [[slice(u * c, (u + 1) * c) for u in sub] for sub in order]
    qs, ks, vs, qk, kv = {}, {}, {}, {}, {}
    for j in range(n_sub):
        for i, (d, h) in enumerate(chains):
            r = rows[j][d]
            tab = (dirs[d][3][r, :], dirs[d][4][r, :])
            qs[j, i] = _rope(dirs[d][0][0, r, sls[h]].astype(F32), *tab)
            ks[j, i] = _rope(dirs[d][1][0, r, sls[h]].astype(F32), *tab) * QK_SCALE
            vs[j, i] = dirs[d][2][0, r, sls[h]]
    for key in qs:
        qk[key] = (_dot_nt(qs[key].astype(BF16), ks[key].astype(BF16)) * dm_ref[key[1]]).astype(BF16)
        kv[key] = _dot((ks[key] * kd_ref[key[1]]).T.astype(BF16), vs[key])
    state = [s_ref[i] for i in range(len(chains))]
    for j in range(n_sub):
        o_inter = [_dot((qs[j, i] * qd_ref[i]).astype(BF16), s.astype(BF16)) for i, s in enumerate(state)]
        for i, (d, h) in enumerate(chains):
            dirs[d][5][0, rows[j][d], sls[h]] = (_dot(qk[j, i], vs[j, i]) + o_inter[i]).astype(dirs[d][5].dtype)
            chunk_decay = jnp.exp(jnp.full((1, HEAD_DIM), float(c), F32) * lg_ref[d, h])
            state[i] = state[i] * chunk_decay + kv[j, i]
    for i, st in enumerate(state):
        s_ref[i] = st


RET_STEP_CHUNKS = 2


def _ret_call(log_gamma, p_main, cos2, sin2, *, l_lat):
    b, l_all, _ = p_main.shape
    c = RET_CHUNK
    blk = RET_STEP_CHUNKS * c
    n_all, n_lat = l_all // blk, l_lat // blk
    n_ctx = n_all - n_lat

    def fwd(s):
        return jnp.where(s < n_ctx, n_lat + s, s - n_ctx)

    def bwd(s):
        return n_all - 1 - s

    def pspec(colblk, order):
        return pl.BlockSpec((1, blk, RET_W), lambda bb, s: (bb, order(s), colblk))

    def tspec(order):
        return pl.BlockSpec((blk, HEAD_DIM), lambda bb, s: (order(s), 0))

    in_specs = [pl.BlockSpec(memory_space=pltpu.SMEM)]
    for order in (fwd, bwd):
        in_specs += [pspec(COL_RET_Q // RET_W, order), pspec(COL_RET_K // RET_W, order),
                     pspec(COL_RET_V // RET_W, order), tspec(order), tspec(order)]
    nhd = 2 * RET_HEADS
    return pl.pallas_call(
        _ret_kernel,
        grid=(b, n_all),
        in_specs=in_specs,
        out_specs=[pl.BlockSpec((1, blk, RET_W), lambda bb, s: (bb, fwd(s), 0)),
                   pl.BlockSpec((1, blk, RET_W), lambda bb, s: (bb, bwd(s), 0))],
        out_shape=[jax.ShapeDtypeStruct((b, l_all, RET_W), BF16)] * 2,
        scratch_shapes=[pltpu.VMEM((nhd, HEAD_DIM, HEAD_DIM), F32), pltpu.VMEM((nhd, c, c), F32),
                        pltpu.VMEM((nhd, c, HEAD_DIM), F32), pltpu.VMEM((nhd, c, HEAD_DIM), F32)],
        compiler_params=_params(("arbitrary", "arbitrary")),
        name="retention_scan",
    )(log_gamma, p_main, p_main, p_main, cos2, sin2, p_main, p_main, p_main, cos2, sin2)


def _attprep_kernel(q_ref, k_ref, v_ref, c_ref, s_ref, qw_ref, kw_ref, qo_ref, ko_ref, vo_ref):
    cos2 = c_ref[...]
    sin2 = s_ref[...]

    def norm_rope(x, w):
        xf = x.astype(F32)
        y = xf * lax.rsqrt(jnp.mean(xf * xf, -1, keepdims=True) + EPS) * w
        return _rope(y, cos2, sin2)

    for h in range(ATT_HEADS):
        sl = slice(h * HEAD_DIM, (h + 1) * HEAD_DIM)
        qo_ref[0, :, sl] = (norm_rope(q_ref[0, :, sl], qw_ref[...]) * (QK_SCALE * LOG2_E)).astype(BF16)
    for h in range(ATT_KV_HEADS):
        sl = slice(h * HEAD_DIM, (h + 1) * HEAD_DIM)
        ko_ref[0, sl, :] = norm_rope(k_ref[0, :, sl], kw_ref[...]).T.astype(BF16)
        vo_ref[0, :, 2 * h * HEAD_DIM:(2 * h + 1) * HEAD_DIM] = v_ref[0, :, sl]
        vo_ref[0, :, (2 * h + 1) * HEAD_DIM:(2 * h + 2) * HEAD_DIM] = jnp.ones((v_ref.shape[1], HEAD_DIM), BF16)


def _attprep_call(p_main, cos2, sin2, qn_w, kn_w, *, tm):
    b, l_all, _ = p_main.shape
    return pl.pallas_call(
        _attprep_kernel,
        grid=(b, l_all // tm),
        in_specs=[pl.BlockSpec((1, tm, ATT_W), lambda bb, i: (bb, i, COL_ATT_Q // ATT_W)),
                  pl.BlockSpec((1, tm, ATT_KV_W), lambda bb, i: (bb, i, COL_ATT_K // ATT_KV_W)),
                  pl.BlockSpec((1, tm, ATT_KV_W), lambda bb, i: (bb, i, COL_ATT_V // ATT_KV_W)),
                  pl.BlockSpec((tm, HEAD_DIM), lambda bb, i: (i, 0)),
                  pl.BlockSpec((tm, HEAD_DIM), lambda bb, i: (i, 0)),
                  pl.BlockSpec((1, HEAD_DIM), lambda bb, i: (0, 0)),
                  pl.BlockSpec((1, HEAD_DIM), lambda bb, i: (0, 0))],
        out_specs=[pl.BlockSpec((1, tm, ATT_W), lambda bb, i: (bb, i, 0)),
                   pl.BlockSpec((1, ATT_KV_W, tm), lambda bb, i: (bb, 0, i)),
                   pl.BlockSpec((1, tm, 2 * ATT_KV_W), lambda bb, i: (bb, i, 0))],
        out_shape=[jax.ShapeDtypeStruct((b, l_all, ATT_W), BF16),
                   jax.ShapeDtypeStruct((b, ATT_KV_W, l_all), BF16),
                   jax.ShapeDtypeStruct((b, l_all, 2 * ATT_KV_W), BF16)],
        compiler_params=_params(("arbitrary", "arbitrary")),
        name="attn_prep",
    )(p_main, p_main, p_main, cos2, sin2, qn_w, kn_w)


def _att_kernel(*refs, tk):
    q_ref, kt_ref, v_ref = refs[0], refs[1], refs[2]
    o_ref, s_ref, mx_ref, m_ref, acc_ref = refs[-5], refs[-4], refs[-3], refs[-2], refs[-1]
    tq = q_ref.shape[1]
    n = kt_ref.shape[2] // tk
    lanes = tk // HEAD_DIM
    rb = 64

    def put_scores(c):
        col = pl.multiple_of(c * tk, tk)
        s_ref[c] = _dot(q_ref[0], kt_ref[0, :, pl.ds(col, tk)])

    def fold_max(c):
        for r in range(tq // rb):
            rows = slice(r * rb, (r + 1) * rb)
            acc = mx_ref[rows, :]
            for j in range(lanes):
                acc = jnp.maximum(acc, s_ref[c, rows, j * HEAD_DIM:(j + 1) * HEAD_DIM])
            mx_ref[rows, :] = acc

    mx_ref[...] = jnp.full(mx_ref.shape, -jnp.inf, F32)
    put_scores(0)

    @pl.loop(1, n)
    def _(c):
        put_scores(c)
        fold_max(c - 1)

    fold_max(n - 1)
    m_ref[...] = jnp.max(mx_ref[...], -1, keepdims=True)

    acc_ref[...] = jnp.zeros_like(acc_ref)

    @pl.loop(0, n)
    def _(c):
        row = pl.multiple_of(c * tk, tk)
        p = jnp.exp2((s_ref[c] - m_ref[...]).astype(BF16))
        acc_ref[...] += _dot(p, v_ref[0, pl.ds(row, tk), :])

    o_ref[0] = (acc_ref[:, :HEAD_DIM] / acc_ref[:, HEAD_DIM:]).astype(BF16)


def _key_chunk(n_keys):
    for tk in (4224, 768, 512, 256):
        if n_keys % tk == 0:
            return tk
    raise ValueError("key count must be a multiple of 256")


def _att_call(qn, kn, v1, *, l_lat, tq):
    b, l_all, _ = qn.shape
    l_ctx = l_all - l_lat
    group = ATT_HEADS // ATT_KV_HEADS
    tk = _key_chunk(l_all)
    y_lat = pl.pallas_call(
        functools.partial(_att_kernel, tk=tk),
        grid=(b, ATT_HEADS, l_lat // tq),
        in_specs=[pl.BlockSpec((1, tq, HEAD_DIM), lambda bb, h, i: (bb, i, h)),
                  pl.BlockSpec((1, HEAD_DIM, l_all), lambda bb, h, i: (bb, h // group, 0),
                               pipeline_mode=pl.Buffered(1)),
                  pl.BlockSpec((1, l_all, 2 * HEAD_DIM), lambda bb, h, i: (bb, 0, h // group),
                               pipeline_mode=pl.Buffered(1))],
        out_specs=pl.BlockSpec((1, tq, HEAD_DIM), lambda bb, h, i: (bb, i, h)),
        out_shape=jax.ShapeDtypeStruct((b, l_all, ATT_W), BF16),
        scratch_shapes=[pltpu.VMEM((l_all // tk, tq, tk), F32), pltpu.VMEM((tq, HEAD_DIM), F32), pltpu.VMEM((tq, 1), F32),
                        pltpu.VMEM((tq, 2 * HEAD_DIM), F32)],
        compiler_params=_params(("arbitrary", "arbitrary", "arbitrary")),
        name="attention",
    )(qn, kn, v1)
    ctx_blk = l_lat // l_ctx
    return pl.pallas_call(
        functools.partial(_att_kernel, tk=l_ctx),
        grid=(b, ATT_HEADS),
        in_specs=[pl.BlockSpec((1, l_ctx, HEAD_DIM), lambda bb, h: (bb, ctx_blk, h)),
                  pl.BlockSpec((1, HEAD_DIM, l_ctx), lambda bb, h: (bb, h // group, ctx_blk)),
                  pl.BlockSpec((1, l_ctx, 2 * HEAD_DIM), lambda bb, h: (bb, ctx_blk, h // group)),
                  pl.BlockSpec(memory_space=pl.ANY)],
        out_specs=pl.BlockSpec((1, l_ctx, HEAD_DIM), lambda bb, h: (bb, ctx_blk, h)),
        out_shape=jax.ShapeDtypeStruct((b, l_all, ATT_W), BF16),
        scratch_shapes=[pltpu.VMEM((1, l_ctx, l_ctx), F32), pltpu.VMEM((l_ctx, HEAD_DIM), F32), pltpu.VMEM((l_ctx, 1), F32),
                        pltpu.VMEM((l_ctx, 2 * HEAD_DIM), F32)],
        input_output_aliases={3: 0},
        compiler_params=_params(("arbitrary", "arbitrary")),
        name="attention_ctx",
    )(qn, kn, v1, y_lat)


def _dnprep_kernel(*refs, tm, l_lat, l_all):
    mains, prevs, nexts = refs[0:3], refs[3:6], refs[6:9]
    cw_ref, pab_ref, alog_ref, dtb_ref = refs[9:13]
    outs = refs[13:16]
    g_ref, ext_ref = refs[16], refs[17]
    i = pl.program_id(1)
    halo = 16
    pad = DN_CONV_K // 2
    first = jnp.logical_or(i == 0, i == l_lat // tm)
    last = jnp.logical_or(i == l_lat // tm - 1, i == l_all // tm - 1)

    keep_prev = jnp.where(first, 0.0, 1.0)
    keep_next = jnp.where(last, 0.0, 1.0)

    for part in range(3):
        ext_ref[0:halo, :] = prevs[part][0].astype(F32) * keep_prev
        ext_ref[halo:halo + tm, :] = mains[part][0].astype(F32)
        ext_ref[halo + tm:, :] = nexts[part][0].astype(F32) * keep_next
        acc = jnp.zeros((tm, DN_W), F32)
        for j in range(DN_CONV_K):
            w_j = cw_ref[j:j + 1, part * DN_W:(part + 1) * DN_W]
            acc = acc + w_j * ext_ref[pl.ds(halo - pad + j, tm), :]
        y = _silu(acc)
        for h in range(DN_HEADS):
            sl = slice(h * HEAD_DIM, (h + 1) * HEAD_DIM)
            yh = y[:, sl]
            if part < 2:
                yh = yh * lax.rsqrt(jnp.sum(yh * yh, -1, keepdims=True) + EPS)
            if part == 0:
                yh = yh * QK_SCALE
            outs[part][0, :, sl] = yh.astype(BF16)

    a = pab_ref[0]
    col = lax.broadcasted_iota(jnp.int32, a.shape, 1)
    z = a + dtb_ref[...]
    softplus = jnp.maximum(z, 0.0) + jnp.log(1.0 + jnp.exp(-jnp.abs(z)))
    g = -jnp.exp(alog_ref[...]) * softplus
    ii = lax.broadcasted_iota(jnp.int32, (tm, tm), 0)
    rr = lax.broadcasted_iota(jnp.int32, (tm, tm), 1)
    same_chunk = (ii // DN_CHUNK) == (rr // DN_CHUNK)
    prefix = jnp.logical_and(same_chunk, rr <= ii).astype(F32)
    suffix = jnp.logical_and(same_chunk, rr >= ii).astype(F32)
    hi = lax.Precision.HIGHEST
    cum_f = jnp.dot(prefix, g, preferred_element_type=F32, precision=hi)
    cum_b = jnp.dot(suffix, g, preferred_element_type=F32, precision=hi)
    g_ref[0] = jnp.where(col < DN_HEADS, cum_f, jnp.where(col < N_DN_GATES, cum_b, jax.nn.sigmoid(a)))


def _neumann_inverse_many(a_list, eye):
    ps = [-a for a in a_list]
    ts = [eye + p for p in ps]
    for _ in range(5):
        pbs = [p.astype(BF16) for p in ps]
        ps = [_dot(pb, pb) for pb in pbs]
        ts = [t + _dot(t.astype(BF16), p.astype(BF16)) for t, p in zip(ts, ps)]
    return ts


def _dnchunk_kernel(q_ref, k_ref, v_ref, g_ref, gt_ref, w_ref, kc_ref, qg_ref, kgt_ref, qk_ref, gl_ref):
    c = DN_CHUNK
    n_sub = q_ref.shape[1] // c
    ii = lax.broadcasted_iota(jnp.int32, (c, c), 0)
    jj = lax.broadcasted_iota(jnp.int32, (c, c), 1)
    eye = (ii == jj).astype(F32)

    qk_ref[...] = jnp.zeros_like(qk_ref)
    kgt_ref[...] = jnp.zeros_like(kgt_ref)
    gl_ref[...] = jnp.zeros_like(gl_ref)

    heads = range(DN_HEADS)
    subs = range(n_sub)
    chains = [(u, d, h) for u in subs for h in heads for d in range(2)]
    rows = [slice(u * c, (u + 1) * c) for u in subs]
    sls = [slice(h * HEAD_DIM, (h + 1) * HEAD_DIM) for h in heads]
    gcol = [g_ref[0, rows[u], :] for u in subs]
    grow = [gt_ref[0, u] for u in subs]
    q16 = {(u, h): q_ref[0, rows[u], sls[h]] for u in subs for h in heads}
    k16 = {(u, h): k_ref[0, rows[u], sls[h]] for u in subs for h in heads}
    kk = {uh: _dot_nt(k16[uh], k16[uh]) for uh in k16}
    qk0 = {uh: _dot_nt(q16[uh], k16[uh]) for uh in k16}

    gcc, tot, beta, decay, a_list = {}, {}, {}, {}, []
    for u, d, h in chains:
        col = d * DN_HEADS + h
        key = (u, d, h)
        gcc[key] = gcol[u][:, col:col + 1]
        gcr = grow[u][col:col + 1, :]
        tot[key] = gcol[u][c - 1:c, col:col + 1] if d == 0 else gcol[u][0:1, col:col + 1]
        beta[key] = gcol[u][:, N_DN_GATES + col:N_DN_GATES + col + 1]
        incl = (ii >= jj) if d == 0 else (ii <= jj)
        strict = (ii > jj) if d == 0 else (ii < jj)
        decay[key] = jnp.where(incl, jnp.exp(jnp.where(incl, gcc[key] - gcr, 0.0)), 0.0)
        a_list.append(jnp.where(strict, kk[u, h] * beta[key] * decay[key], 0.0))
        gl_ref[0, d, u, h:h + 1, :] = jnp.broadcast_to(jnp.exp(tot[key]), (1, HEAD_DIM))

    t16 = [t.astype(BF16) for t in _neumann_inverse_many(a_list, eye)]

    for key, t in zip(chains, t16):
        u, d, h = key
        sl = sls[h]
        k = k16[u, h].astype(F32)
        e_col = jnp.exp(gcc[key])
        rhs = jnp.concatenate([(v_ref[0, rows[u], sl].astype(F32) * beta[key]).astype(BF16),
                               (k * (beta[key] * e_col)).astype(BF16)], axis=1)
        wk = _dot(t, rhs)
        w_ref[0, d, rows[u], sl] = wk[:, :HEAD_DIM]
        kc_ref[0, d, rows[u], sl] = wk[:, HEAD_DIM:].astype(BF16)
        qg_ref[0, d, rows[u], sl] = (q16[u, h].astype(F32) * e_col).astype(BF16)
        kg = k * jnp.exp(tot[key] - gcc[key])
        kgt_ref[0, d, u, :, pl.ds(h * HEAD_DIM, c)] = kg.T.astype(BF16)
        qk_ref[0, d, rows[u], pl.ds(h * HEAD_DIM, c)] = (qk0[u, h] * decay[key]).astype(BF16)


DN_STEP_CHUNKS = 4


def _dnfactors_kernel(*refs, tm, l_lat, l_all):
    n_in, n_out = 13, 6
    outs = refs[n_in:n_in + n_out]
    ext_ref, q_s, k_s, v_s, g_s, gt_s = refs[n_in + n_out:]
    _dnprep_kernel(*refs[:n_in], q_s, k_s, v_s, g_s, ext_ref, tm=tm, l_lat=l_lat, l_all=l_all)
    for u in range(tm // DN_CHUNK):
        gt_s[0, u] = g_s[0, u * DN_CHUNK:(u + 1) * DN_CHUNK, :].T[0:2 * N_DN_GATES, :]
    _dnchunk_kernel(q_s, k_s, v_s, g_s, gt_s, *outs)


def _dnfactors_call(p_main, p_ab, conv_w, alog_row, dtb_row, *, l_lat):
    b, l_all, _ = p_main.shape
    c = DN_CHUNK
    ns = DN_STEP_CHUNKS
    tm = ns * c
    n = l_all // c
    halo = 16
    r = tm // halo
    cb = COL_DN_QKV // DN_W
    in_specs = []
    for part in range(3):
        in_specs.append(pl.BlockSpec((1, tm, DN_W), lambda bb, i, p=part: (bb, i, cb + p)))
    for part in range(3):
        in_specs.append(pl.BlockSpec((1, halo, DN_W), lambda bb, i, p=part: (bb, jnp.maximum(i * r - 1, 0), cb + p)))
    for part in range(3):
        in_specs.append(pl.BlockSpec(
            (1, halo, DN_W), lambda bb, i, p=part: (bb, jnp.minimum((i + 1) * r, l_all // halo - 1), cb + p)))
    in_specs += [pl.BlockSpec((DN_CONV_K, 3 * DN_W), lambda bb, i: (0, 0)),
                 pl.BlockSpec((1, tm, GATE_COLS), lambda bb, i: (bb, i, 0)),
                 pl.BlockSpec((1, GATE_COLS), lambda bb, i: (0, 0)),
                 pl.BlockSpec((1, GATE_COLS), lambda bb, i: (0, 0))]
    tok = lambda dt: jax.ShapeDtypeStruct((b, 2, l_all, DN_W), dt)
    tok_spec = pl.BlockSpec((1, 2, tm, DN_W), lambda bb, s: (bb, 0, s, 0))
    return pl.pallas_call(
        functools.partial(_dnfactors_kernel, tm=tm, l_lat=l_lat, l_all=l_all),
        grid=(b, l_all // tm),
        in_specs=in_specs,
        out_specs=[tok_spec, tok_spec, tok_spec,
                   pl.BlockSpec((1, 2, ns, HEAD_DIM, DN_W), lambda bb, s: (bb, 0, s, 0, 0)),
                   tok_spec,
                   pl.BlockSpec((1, 2, ns, 8, HEAD_DIM), lambda bb, s: (bb, 0, s, 0, 0))],
        out_shape=[tok(F32), tok(BF16), tok(BF16),
                   jax.ShapeDtypeStruct((b, 2, n, HEAD_DIM, DN_W), BF16),
                   tok(BF16),
                   jax.ShapeDtypeStruct((b, 2, n, 8, HEAD_DIM), F32)],
        scratch_shapes=[pltpu.VMEM((tm + 2 * halo, DN_W), F32)]
        + [pltpu.VMEM((1, tm, DN_W), BF16)] * 3
        + [pltpu.VMEM((1, tm, GATE_COLS), F32), pltpu.VMEM((1, ns, 2 * N_DN_GATES, c), F32)],
        compiler_params=_params(("arbitrary", "arbitrary")),
        name="deltanet_conv_factors",
    )(*([p_main] * 9), conv_w, p_ab, alog_row, dtb_row)


def _dnscan_kernel(*refs):
    ins_f, ins_b = refs[0:6], refs[6:12]
    of_ref, ob_ref, s_ref = refs[12], refs[13], refs[14]
    c = DN_CHUNK
    n_sub = of_ref.shape[1] // c
    step = pl.program_id(1)

    @pl.when(step == 0)
    def _():
        s_ref[...] = jnp.zeros_like(s_ref)

    ins, outs = (ins_f, ins_b), (of_ref, ob_ref)
    chains = [(d, h) for d in range(2) for h in range(DN_HEADS)]
    sls = [slice(h * HEAD_DIM, (h + 1) * HEAD_DIM) for h in range(DN_HEADS)]
    state = [s_ref[d * DN_HEADS + h] for d, h in chains]
    for j in range(n_sub):
        sub = [j, n_sub - 1 - j]
        rows = [slice(u * c, (u + 1) * c) for u in sub]
        s16 = [s.astype(BF16) for s in state]
        v16 = [(ins[d][0][0, 0, rows[d], sls[h]] - _dot(ins[d][1][0, 0, rows[d], sls[h]], s)).astype(BF16)
               for (d, h), s in zip(chains, s16)]
        o_inter = [_dot(ins[d][2][0, 0, rows[d], sls[h]], s) for (d, h), s in zip(chains, s16)]
        new_state = []
        for (d, h), s, v, oi in zip(chains, state, v16, o_inter):
            qk = ins[d][4][0, 0, rows[d], pl.ds(h * HEAD_DIM, c)]
            outs[d][0, rows[d], sls[h]] = (oi + _dot(qk, v)).astype(outs[d].dtype)
            kgt = ins[d][3][0, 0, sub[d], :, pl.ds(h * HEAD_DIM, c)]
            new_state.append(s * ins[d][5][0, 0, sub[d], h:h + 1, :] + _dot(kgt, v))
        state = new_state
    for i, s in enumerate(state):
        s_ref[i] = s


def _dnscan_call(w, kc, qg, kgt, qk, gl, *, l_lat):
    b, _, l_all, _ = w.shape
    ns = DN_STEP_CHUNKS
    blk = ns * DN_CHUNK
    n_all, n_lat = l_all // blk, l_lat // blk
    n_ctx = n_all - n_lat

    def fwd(s):
        return jnp.where(s < n_ctx, n_lat + s, s - n_ctx)

    def bwd(s):
        return n_all - 1 - s

    in_specs = []
    for d, order in enumerate((fwd, bwd)):
        tok_spec = pl.BlockSpec((1, 1, blk, DN_W), lambda bb, s, d=d, o=order: (bb, d, o(s), 0))
        in_specs += [tok_spec, tok_spec, tok_spec,
                     pl.BlockSpec((1, 1, ns, HEAD_DIM, DN_W), lambda bb, s, d=d, o=order: (bb, d, o(s), 0, 0)),
                     tok_spec,
                     pl.BlockSpec((1, 1, ns, 8, HEAD_DIM), lambda bb, s, d=d, o=order: (bb, d, o(s), 0, 0))]
    return pl.pallas_call(
        _dnscan_kernel,
        grid=(b, n_all),
        in_specs=in_specs,
        out_specs=[pl.BlockSpec((1, blk, DN_W), lambda bb, s: (bb, fwd(s), 0)),
                   pl.BlockSpec((1, blk, DN_W), lambda bb, s: (bb, bwd(s), 0))],
        out_shape=[jax.ShapeDtypeStruct((b, l_all, DN_W), BF16)] * 2,
        scratch_shapes=[pltpu.VMEM((2 * DN_HEADS, HEAD_DIM, HEAD_DIM), F32)],
        compiler_params=_params(("arbitrary", "arbitrary")),
        name="deltanet_scan",
    )(w, kc, qg, kgt, qk, gl, w, kc, qg, kgt, qk, gl)


def _rope_tables(l_lat, l_ctx):
    rows = l_lat // GRID_W
    n_freq = HEAD_DIM // 4
    inv = ROPE_THETA ** (-jnp.arange(n_freq, dtype=F32) / n_freq)
    row_ang = jnp.arange(rows, dtype=F32)[:, None] * inv
    col_ang = jnp.arange(GRID_W, dtype=F32)[:, None] * inv

    def table(fn):
        return jnp.concatenate([jnp.repeat(fn(row_ang), GRID_W, axis=0), jnp.tile(fn(col_ang), (rows, 1))], -1)

    cos, sin = table(jnp.cos), table(jnp.sin)
    cos2 = jnp.concatenate([cos, cos], -1)
    sin2 = jnp.concatenate([-sin, sin], -1)
    cos2 = jnp.concatenate([cos2, jnp.ones((l_ctx, HEAD_DIM), F32)], 0)
    sin2 = jnp.concatenate([sin2, jnp.zeros((l_ctx, HEAD_DIM), F32)], 0)
    return cos2, sin2


def _pad_row(v, width):
    v = v.reshape(1, -1).astype(F32)
    return jnp.pad(v, ((0, 0), (0, width - v.shape[1])))


def _row_tile(l_lat, l_ctx):
    l_all = l_lat + l_ctx
    for tm in (768, 512, 256, 128):
        if l_all % tm == 0:
            return tm
    raise ValueError("token count must be a multiple of 128")


def kernel(x, c, ctx, c_ctx, w_ada, b_ada, w_in, ret_decay_logit, dn_conv_w, dn_a_log, dn_dt_bias, dn_norm_w,
           att_qn_w, att_kn_w, w_o, ln1_w, ln1_b, w_ffn_in, w_ffn_out, ln2_w, ln2_b):
    bsz, l_lat, d = x.shape
    l_ctx = ctx.shape[1]
    l_all = l_lat + l_ctx
    depth = w_ada.shape[0]
    assert l_lat % 256 == 0 and l_ctx % 256 == 0 and l_lat % GRID_W == 0
    tm = _row_tile(l_lat, l_ctx)
    tn = MAIN_W // 2

    cos2, sin2 = _rope_tables(l_lat, l_ctx)

    cond_rows = 8 * pl.cdiv(bsz + 1, 8)
    cond = jnp.concatenate([c, c_ctx[None, :]], 0)
    cond = jnp.pad(cond, ((0, cond_rows - bsz - 1), (0, 0)))
    mod = _ada_call(cond, w_ada, b_ada)

    w_main, w_ab = _repack_call(jnp.pad(w_in.astype(BF16), ((0, 0), (0, 0), (0, -w_in.shape[2] % HEAD_DIM))))
    w_o16, w_ffn_in16, w_ffn_out16 = w_o.astype(BF16), w_ffn_in.astype(BF16), w_ffn_out.astype(BF16)
    for i in range(depth):
        last = i == depth - 1
        ml = mod[i, :bsz].reshape(bsz, 6, d)
        mc = mod[i, bsz].reshape(1, 6, d)
        if i == 0:
            p_main, p_ab, x_all = _inproj_call((x, ctx), ml, mc, w_main, w_ab, layer=i, l_lat=l_lat, tm=tm, tn=tn // 2)
        else:
            p_main, p_ab = _inproj_call(x_all, ml, mc, w_main, w_ab, layer=i, l_lat=l_lat, tm=tm, tn=tn)

        log_gamma = jax.nn.log_sigmoid(ret_decay_logit[i].astype(F32))
        ret_f, ret_b = _ret_call(log_gamma, p_main, cos2, sin2, l_lat=l_lat)

        qn, kn, v1 = _attprep_call(p_main, cos2, sin2, att_qn_w[i].reshape(1, -1), att_kn_w[i].reshape(1, -1),
                                   tm=tm)
        y_att = _att_call(qn, kn, v1, l_lat=l_lat, tq=1024 if l_lat % 1024 == 0 else 256)

        factors = _dnfactors_call(p_main, p_ab, dn_conv_w[i], _pad_row(dn_a_log[i], GATE_COLS),
                                  _pad_row(dn_dt_bias[i], GATE_COLS), l_lat=l_lat)
        dn_f, dn_b = _dnscan_call(*factors, l_lat=l_lat)

        x_all = _wo_call(ret_f, ret_b, dn_f, dn_b, p_main, dn_norm_w[i].reshape(1, -1), y_att, x_all, ml, mc,
                         w_o16, ln1_w[i].reshape(1, -1), ln1_b[i].reshape(1, -1), layer=i, l_lat=l_lat, tm=tm)
        x_all = _ffn_call(x_all, ml, mc, w_ffn_in16, w_ffn_out16, ln2_w[i].reshape(1, -1), ln2_b[i].reshape(1, -1),
                          layer=i, l_lat=l_lat, out_rows=l_lat if last else l_all, tm=tm, tf=512)
    return x_all
```
